```python
import math
import jax, jax.numpy as jnp
from jax import lax
import numpy as np

D_MODEL = 4096
BATCH = 1
SEQ = 16384
DEPTH = 4

TOTAL_HEADS = 32
HEAD_DIM = D_MODEL // TOTAL_HEADS
GRID_W = 64
A_HEADS = 12
A_PATTERNS = ((128, 1), (512, 4), (2048, 16))
B_HEADS = 8
B_KV_HEADS = 2
B_HALF_WINDOW = 128
C_HEADS = 12
C_KV_HEADS = 4
Q_BLOCK = 128
ROPE_THETA = 10000.0
NUM_BUCKETS = 32
MAX_DISTANCE = 1024
EPS = 1e-6
NEG_INF = -1e30

A_W = A_HEADS * HEAD_DIM
B_W = B_HEADS * HEAD_DIM
B_KV_W = B_KV_HEADS * HEAD_DIM
C_W = C_HEADS * HEAD_DIM
C_KV_W = C_KV_HEADS * HEAD_DIM
MIX_WIDTH = A_W + B_W + C_W
SPLITS = (A_W, A_W, A_W, A_W, B_W, B_KV_W, B_KV_W, B_W, C_W, C_KV_W, C_KV_W, C_W)
IN_WIDTH = 4 * A_W + 2 * B_W + 2 * B_KV_W + 2 * C_W + 2 * C_KV_W

kernel_name = 'hybrid_dilated_window_axial_encoder'


def rms_norm(x, g):
    xf = x.astype(jnp.float32)
    y = xf * lax.rsqrt(jnp.mean(xf * xf, axis=-1, keepdims=True) + EPS)
    return (y * g.astype(jnp.float32)).astype(x.dtype)


def t5_bucket(rel):
    half_b = NUM_BUCKETS // 2
    max_exact = half_b // 2
    ret = jnp.where(rel > 0, half_b, 0)
    n = jnp.abs(rel)
    nf = jnp.maximum(n, 1).astype(jnp.float32)
    large = max_exact + (jnp.log(nf / max_exact) / math.log(MAX_DISTANCE / max_exact)
                         * (half_b - max_exact)).astype(jnp.int32)
    large = jnp.minimum(large, half_b - 1)
    return ret + jnp.where(n < max_exact, n, large)


def band_bias(table, half, dil):
    r = jnp.arange(half)[:, None]
    cix = jnp.arange(3 * half)[None, :]
    rel = (cix - half - r) * dil
    return table[t5_bucket(rel)].astype(jnp.float32).transpose(2, 0, 1)


def banded_attention(q, k, v, half, bias):
    N, L, H, Dh = q.shape
    Hkv = k.shape[2]
    G = H // Hkv
    blk = half
    nb = -(-L // blk)
    pad = nb * blk - L
    qf = jnp.pad(q.astype(jnp.float32), ((0, 0), (0, pad), (0, 0), (0, 0)))
    kf = jnp.pad(k.astype(jnp.float32), ((0, 0), (blk, blk + pad), (0, 0), (0, 0)))
    vf = jnp.pad(v.astype(jnp.float32), ((0, 0), (blk, blk + pad), (0, 0), (0, 0)))
    qb = qf.reshape(N, nb, blk, Hkv, G, Dh)
    kr = kf.reshape(N, nb + 2, blk, Hkv, Dh)
    vr = vf.reshape(N, nb + 2, blk, Hkv, Dh)
    kb = jnp.concatenate([kr[:, :-2], kr[:, 1:-1], kr[:, 2:]], axis=2)
    vb = jnp.concatenate([vr[:, :-2], vr[:, 1:-1], vr[:, 2:]], axis=2)
    s = jnp.einsum('nbqkgd,nbckd->nbkgqc', qb, kb) * (Dh ** -0.5)
    s = s + bias.reshape(Hkv, G, blk, 3 * blk)
    r = jnp.arange(blk)[:, None]
    cix = jnp.arange(3 * blk)[None, :]
    kglob = jnp.arange(nb)[:, None, None] * blk + cix[None] - blk
    valid = (jnp.abs(cix - blk - r) <= half)[None] & (kglob >= 0) & (kglob < L)
    s = jnp.where(valid[None, :, None, None], s, NEG_INF)
    m = jnp.max(s, axis=-1)
    p = jnp.exp(s - m[..., None])
    l = jnp.sum(p, axis=-1)
    acc = jnp.einsum('nbkgqc,nbckd->nbqkgd', p, vb).reshape(N, nb * blk, H, Dh)[:, :L]
    m = m.transpose(0, 1, 4, 2, 3).reshape(N, nb * blk, H)[:, :L]
    l = l.transpose(0, 1, 4, 2, 3).reshape(N, nb * blk, H)[:, :L]
    return acc, m, l


def dilated_attention(q, k, v, biases):
    B, S, H, Dh = q.shape
    accs, ms, ls = [], [], []
    for (w, d), bias in zip(A_PATTERNS, biases):
        L = S // d
        def to_res(t):
            return t.reshape(B, L, d, t.shape[2], Dh).transpose(0, 2, 1, 3, 4).reshape(B * d, L, t.shape[2], Dh)
        acc, m, l = banded_attention(to_res(q), to_res(k), to_res(v), w // (2 * d), bias)
        accs.append(acc.reshape(B, d, L, H, Dh).transpose(0, 2, 1, 3, 4).reshape(B, S, H, Dh))
        ms.append(m.reshape(B, d, L, H).transpose(0, 2, 1, 3).reshape(B, S, H))
        ls.append(l.reshape(B, d, L, H).transpose(0, 2, 1, 3).reshape(B, S, H))
    mx = jnp.maximum(jnp.maximum(ms[0], ms[1]), ms[2])
    ws = [jnp.exp(m - mx) for m in ms]
    num = accs[0] * ws[0][..., None] + accs[1] * ws[1][..., None] + accs[2] * ws[2][..., None]
    den = ls[0] * ws[0] + ls[1] * ws[1] + ls[2] * ws[2]
    return (num / den[..., None]).astype(q.dtype)


def sink_window_attention(q, k, v, bias, sink):
    acc, m, l = banded_attention(q, k, v, B_HALF_WINDOW, bias)
    sk = sink.astype(jnp.float32)
    mx = jnp.maximum(m, sk)
    scale = jnp.exp(m - mx)
    den = l * scale + jnp.exp(sk - mx)
    return (acc * (scale / den)[..., None]).astype(q.dtype)


def rope_tables(pos, dim):
    freqs = ROPE_THETA ** (-jnp.arange(0, dim, 2, dtype=jnp.float32) / dim)
    ang = pos.astype(jnp.float32)[:, None] * freqs[None, :]
    return jnp.cos(ang), jnp.sin(ang)


def rotate(x, cos, sin):
    x1, x2 = jnp.split(x, 2, axis=-1)
    cos = cos[:, None, :]
    sin = sin[:, None, :]
    return jnp.concatenate([x1 * cos - x2 * sin, x2 * cos + x1 * sin], axis=-1)


def rope2d(x, tabs):
    cr, sr, cc, sc = tabs
    xf = x.astype(jnp.float32)
    hd = HEAD_DIM // 2
    return jnp.concatenate([rotate(xf[..., :hd], cr, sr), rotate(xf[..., hd:], cc, sc)], axis=-1).astype(x.dtype)


def dense_block_attention(q, k, v):
    B, S, H, Dh = q.shape
    Hkv = k.shape[2]
    G = H // Hkv
    nq = S // Q_BLOCK
    kf = k.astype(jnp.float32)
    vf = v.astype(jnp.float32)
    qb = q.astype(jnp.float32).reshape(B, nq, Q_BLOCK, Hkv, G, Dh).transpose(1, 0, 2, 3, 4, 5)
    def one_block(qblk):
        s = jnp.einsum('bqkgd,bskd->bkgqs', qblk, kf) * (Dh ** -0.5)
        p = jax.nn.softmax(s, axis=-1)
        return jnp.einsum('bkgqs,bskd->bqkgd', p, vf)
    o = lax.map(one_block, qb)
    return o.transpose(1, 0, 2, 3, 4, 5).reshape(B, S, H, Dh).astype(q.dtype)


def split_columns(t):
    outs, start = [], 0
    for w in SPLITS:
        outs.append(t[..., start:start + w])
        start += w
    return outs


def mixer_sublayer(h, w_in, w_out, q_gain, k_gain, sink, biases_a, bias_b, tabs):
    B, S, _ = h.shape
    proj = h @ w_in
    aq, ak, av, ag, bq, bk, bv, bg, cq, ck, cv, cg = split_columns(proj)
    heads = lambda t: t.reshape(B, S, -1, HEAD_DIM)
    oa = dilated_attention(heads(aq), heads(ak), heads(av), biases_a)
    ob = sink_window_attention(heads(bq), heads(bk), heads(bv), bias_b, sink)
    qc = rope2d(rms_norm(heads(cq), q_gain), tabs)
    kc = rope2d(rms_norm(heads(ck), k_gain), tabs)
    oc = dense_block_attention(qc, kc, heads(cv))
    o = jnp.concatenate([oa.reshape(B, S, A_W) * jax.nn.silu(ag),
                         ob.reshape(B, S, B_W) * jax.nn.silu(bg),
                         oc.reshape(B, S, C_W) * jax.nn.silu(cg)], axis=-1)
    return o @ w_out


def setup_inputs(seed: int = 0) -> dict:
    key = jax.random.key(seed)
    ks = jax.random.split(key, 12)
    nrm = jax.random.normal
    f32 = jnp.float32
    x = nrm(ks[0], (BATCH, SEQ, D_MODEL), f32)
    c = nrm(ks[1], (BATCH, D_MODEL), f32)
    w_mod = nrm(ks[2], (DEPTH, D_MODEL, 3 * D_MODEL), f32) * (0.5 * D_MODEL ** -0.5)
    b_mod = nrm(ks[3], (DEPTH, 3 * D_MODEL), f32) * 0.02
    pre_norm = 1.0 + 0.05 * nrm(ks[4], (DEPTH, D_MODEL), f32)
    post_norm = 1.0 + 0.05 * nrm(ks[5], (DEPTH, D_MODEL), f32)
    w_in = nrm(ks[6], (DEPTH, D_MODEL, IN_WIDTH), f32) * (D_MODEL ** -0.5)
    w_out = nrm(ks[7], (DEPTH, MIX_WIDTH, D_MODEL), f32) * (MIX_WIDTH ** -0.5)
    q_norm = 1.0 + 0.05 * nrm(ks[8], (DEPTH, HEAD_DIM), f32)
    k_norm = 1.0 + 0.05 * nrm(ks[9], (DEPTH, HEAD_DIM), f32)
    sink = nrm(ks[10], (DEPTH, B_HEADS), f32) * 0.5
    rel_bias = nrm(ks[11], (NUM_BUCKETS, A_HEADS + B_HEADS), f32) * 0.5
    return dict(x=x, c=c, w_mod=w_mod, b_mod=b_mod, pre_norm=pre_norm, post_norm=post_norm,
                w_in=w_in, w_out=w_out, q_norm=q_norm, k_norm=k_norm, sink=sink, rel_bias=rel_bias)


def reference(x, c, w_mod, b_mod, pre_norm, post_norm, w_in, w_out, q_norm, k_norm, sink, rel_bias):
    B, S, _ = x.shape
    rows = S // GRID_W
    row = jnp.repeat(jnp.arange(rows), GRID_W)
    col = jnp.tile(jnp.arange(GRID_W), rows)
    cr, sr = rope_tables(row, HEAD_DIM // 2)
    cc, sc = rope_tables(col, HEAD_DIM // 2)
    tabs = (cr, sr, cc, sc)
    biases_a = [band_bias(rel_bias[:, :A_HEADS], w // (2 * d), d) for (w, d) in A_PATTERNS]
    bias_b = band_bias(rel_bias[:, A_HEADS:], B_HALF_WINDOW, 1)
    cs = jax.nn.silu(c)
    for i in range(DEPTH):
        mod = cs @ w_mod[i] + b_mod[i]
        shift, scale, gate = jnp.split(mod, 3, axis=-1)
        h = rms_norm(x, pre_norm[i]) * (1.0 + scale[:, None, :]) + shift[:, None, :]
        y = mixer_sublayer(h, w_in[i], w_out[i], q_norm[i], k_norm[i], sink[i], biases_a, bias_b, tabs)
        x = x + gate[:, None, :] * rms_norm(y, post_norm[i])
    return x
```

```python
import functools
import math

import numpy as np
import jax
import jax.numpy as jnp
from jax import lax
from jax.experimental import pallas as pl
from jax.experimental.pallas import tpu as pltpu

HEAD_DIM = 128
GRID_W = 64
A_HEADS = 12
A_PATTERNS = ((128, 1), (512, 4), (2048, 16))
B_HEADS = 8
B_KV_HEADS = 2
B_HALF_WINDOW = 128
C_HEADS = 12
C_KV_HEADS = 4
ROPE_THETA = 10000.0
NUM_BUCKETS = 32
MAX_DISTANCE = 1024
EPS = 1e-6
LOG2E = math.log2(math.e)
MASKED = -1e30
M_INIT = -1e29

A_W = A_HEADS * HEAD_DIM
B_W = B_HEADS * HEAD_DIM
C_W = C_HEADS * HEAD_DIM
AQ, AK, AV, AG = 0, 12, 24, 36
BQ, BK, BV, BG = 48, 56, 58, 60
CQ, CK, CV, CG = 68, 80, 84, 88
IN_WIDTH = 100 * HEAD_DIM

A_REACH = max(w // 2 for w, _ in A_PATTERNS)
A_TILE = 512
B_TILE = 256
C_TQ = 256
C_TK = 512
V7X_VMEM_LIMIT = 48 * 1024 * 1024


def _cparams(n_axes):
    return pltpu.CompilerParams(dimension_semantics=("arbitrary",) * n_axes,
                                vmem_limit_bytes=V7X_VMEM_LIMIT)


def _tile(n, pref):
    t = min(n, pref)
    while n % t:
        t //= 2
    return t


def _mod_kernel(c_ref, w_ref, b_ref, o_ref):
    c = c_ref[...]
    cs = jnp.broadcast_to(c * jax.nn.sigmoid(c), (8, c.shape[1]))
    r = jnp.dot(cs, w_ref[0], preferred_element_type=jnp.float32)
    o_ref[0] = r[0:1] + b_ref[0]


def _modulation(c, w_mod, b_mod):
    depth, d, n = w_mod.shape
    tn = _tile(n, 1024)
    out = pl.pallas_call(
        _mod_kernel,
        grid=(depth, n // tn),
        in_specs=[pl.BlockSpec((1, d), lambda i, j: (0, 0)),
                  pl.BlockSpec((1, d, tn), lambda i, j: (i, 0, j)),
                  pl.BlockSpec((1, 1, tn), lambda i, j: (i, 0, j))],
        out_specs=pl.BlockSpec((1, 1, tn), lambda i, j: (i, 0, j)),
        out_shape=jax.ShapeDtypeStruct((depth, 1, n), jnp.float32),
        compiler_params=_cparams(2),
        name="modulation",
    )(c, w_mod, b_mod.reshape(depth, 1, n))
    return out


def _rms(x):
    return x * lax.rsqrt(jnp.mean(x * x, axis=-1, keepdims=True) + EPS)


def _prenorm_kernel(x_ref, g_ref, sc_ref, sh_ref, h_ref):
    y = _rms(x_ref[...]) * g_ref[...]
    h_ref[...] = (y * (1.0 + sc_ref[...]) + sh_ref[...]).astype(h_ref.dtype)


def _prenorm(x, g, scale, shift):
    s, d = x.shape
    tm = _tile(s, 512)
    vec = pl.BlockSpec((1, d), lambda i: (0, 0))
    return pl.pallas_call(
        _prenorm_kernel,
        grid=(s // tm,),
        in_specs=[pl.BlockSpec((tm, d), lambda i: (i, 0)), vec, vec, vec],
        out_specs=pl.BlockSpec((tm, d), lambda i: (i, 0)),
        out_shape=jax.ShapeDtypeStruct((s, d), jnp.bfloat16),
        compiler_params=_cparams(1),
        name="prenorm",
    )(x, g, scale, shift)


def _post_kernel(x_ref, y_ref, pg_ref, gate_ref, xo_ref):
    yn = _rms(y_ref[...].astype(jnp.float32)) * pg_ref[...]
    xo_ref[...] = x_ref[...] + gate_ref[...] * yn


def _postpre_kernel(x_ref, y_ref, pg_ref, gate_ref, g_ref, sc_ref, sh_ref, xo_ref, h_ref):
    yn = _rms(y_ref[...].astype(jnp.float32)) * pg_ref[...]
    xn = x_ref[...] + gate_ref[...] * yn
    xo_ref[...] = xn
    h = _rms(xn) * g_ref[...]
    h_ref[...] = (h * (1.0 + sc_ref[...]) + sh_ref[...]).astype(h_ref.dtype)


def _post(x, y, post_g, gate, nxt=None):
    s, d = x.shape
    tm = _tile(s, 256)
    row = pl.BlockSpec((tm, d), lambda i: (i, 0))
    vec = pl.BlockSpec((1, d), lambda i: (0, 0))
    if nxt is None:
        return pl.pallas_call(
            _post_kernel, grid=(s // tm,),
            in_specs=[row, row, vec, vec], out_specs=row,
            out_shape=jax.ShapeDtypeStruct((s, d), jnp.float32),
            compiler_params=_cparams(1), name="postnorm",
        )(x, y, post_g, gate), None
    g, scale, shift = nxt
    return pl.pallas_call(
        _postpre_kernel, grid=(s // tm,),
        in_specs=[row, row, vec, vec, vec, vec, vec], out_specs=[row, row],
        out_shape=[jax.ShapeDtypeStruct((s, d), jnp.float32),
                   jax.ShapeDtypeStruct((s, d), jnp.bfloat16)],
        compiler_params=_cparams(1), name="postnorm_prenorm",
    )(x, y, post_g, gate, g, scale, shift)


def _mm_kernel(a_ref, b_ref, o_ref):
    o_ref[...] = jnp.dot(a_ref[...], b_ref[...],
                         preferred_element_type=jnp.float32).astype(o_ref.dtype)


def _in_projection(h, w):
    m, k = h.shape
    n = w.shape[1]
    tm, tn = _tile(m, 1024), _tile(n, 512)
    return pl.pallas_call(
        _mm_kernel,
        grid=(m // tm, n // tn),
        in_specs=[pl.BlockSpec((tm, k), lambda i, j: (i, 0)),
                  pl.BlockSpec((k, tn), lambda i, j: (0, j))],
        out_specs=pl.BlockSpec((tm, tn), lambda i, j: (i, j)),
        out_shape=jax.ShapeDtypeStruct((m, n), jnp.bfloat16),
        compiler_params=_cparams(2),
        name="in_projection",
    )(h, w)


def _mm3_kernel(a1, a2, a3, b1, b2, b3, o_ref):
    acc = jnp.dot(a1[...], b1[...], preferred_element_type=jnp.float32)
    acc += jnp.dot(a2[...], b2[...], preferred_element_type=jnp.float32)
    acc += jnp.dot(a3[...], b3[...], preferred_element_type=jnp.float32)
    o_ref[...] = acc.astype(o_ref.dtype)


def _out_projection(oa, ob, oc, wa, wb, wc):
    m = oa.shape[0]
    n = wa.shape[1]
    tm, tn = _tile(m, 1024), _tile(n, 512)
    a_spec = lambda a: pl.BlockSpec((tm, a.shape[1]), lambda i, j: (i, 0))
    b_spec = lambda b: pl.BlockSpec((b.shape[0], tn), lambda i, j: (0, j))
    return pl.pallas_call(
        _mm3_kernel,
        grid=(m // tm, n // tn),
        in_specs=[a_spec(oa), a_spec(ob), a_spec(oc), b_spec(wa), b_spec(wb), b_spec(wc)],
        out_specs=pl.BlockSpec((tm, tn), lambda i, j: (i, j)),
        out_shape=jax.ShapeDtypeStruct((m, n), jnp.float32),
        compiler_params=_cparams(2),
        name="out_projection",
    )(oa, ob, oc, wa, wb, wc)


def _rope_tables(s):
    rows = s // GRID_W
    row = jnp.repeat(jnp.arange(rows), GRID_W)
    col = jnp.tile(jnp.arange(GRID_W), rows)
    quarter = HEAD_DIM // 4
    freqs = ROPE_THETA ** (-jnp.arange(0, 2 * quarter, 2, dtype=jnp.float32) / (2 * quarter))
    ar = row.astype(jnp.float32)[:, None] * freqs[None, :]
    ac = col.astype(jnp.float32)[:, None] * freqs[None, :]
    cr, sr, cc, sc = jnp.cos(ar), jnp.sin(ar), jnp.cos(ac), jnp.sin(ac)
    cos = jnp.concatenate([cr, cr, cc, cc], axis=-1)
    sin = jnp.concatenate([-sr, sr, -sc, sc], axis=-1)
    return cos, sin


def _rope(x, cos, sin):
    quarter = HEAD_DIM // 4
    fwd = pltpu.roll(x, HEAD_DIM - quarter, 1)
    bwd = pltpu.roll(x, quarter, 1)
    lane = lax.broadcasted_iota(jnp.int32, x.shape, 1)
    partner = jnp.where((lane % (2 * quarter)) < quarter, fwd, bwd)
    return x * cos + partner * sin


def _prep_kernel(av_ref, bv_ref, cv_ref, ck_ref, cos_ref, sin_ref, kg_ref,
                 vta_ref, vtb_ref, vtc_ref, kc_ref, *, ts):
    def put_t(src_ref, n_heads, dst_ref, chunk):
        for hh in range(n_heads):
            vt = src_ref[:, hh * HEAD_DIM:(hh + 1) * HEAD_DIM].astype(jnp.float32).T
            for cidx in range(ts // chunk):
                dst_ref[hh, cidx] = vt[:, cidx * chunk:(cidx + 1) * chunk].astype(dst_ref.dtype)

    put_t(av_ref, A_HEADS, vta_ref, A_TILE)
    put_t(bv_ref, B_KV_HEADS, vtb_ref, B_TILE)
    put_t(cv_ref, C_KV_HEADS, vtc_ref, C_TK)
    cos, sin, kg = cos_ref[...], sin_ref[...], kg_ref[...]
    for hh in range(C_KV_HEADS):
        k = ck_ref[:, hh * HEAD_DIM:(hh + 1) * HEAD_DIM].astype(jnp.float32)
        k = _rope(_rms(k) * kg, cos, sin)
        kc_ref[:, hh * HEAD_DIM:(hh + 1) * HEAD_DIM] = k.astype(kc_ref.dtype)


def _prep(proj, cos, sin, k_gain):
    s = proj.shape[0]
    ts = 512
    assert s % ts == 0
    bf = jnp.bfloat16
    n = s // ts
    return pl.pallas_call(
        functools.partial(_prep_kernel, ts=ts),
        grid=(n,),
        in_specs=[pl.BlockSpec((ts, A_W), lambda i: (i, AV * HEAD_DIM // A_W)),
                  pl.BlockSpec((ts, B_KV_HEADS * HEAD_DIM), lambda i: (i, BV // B_KV_HEADS)),
                  pl.BlockSpec((ts, C_KV_HEADS * HEAD_DIM), lambda i: (i, CV // C_KV_HEADS)),
                  pl.BlockSpec((ts, C_KV_HEADS * HEAD_DIM), lambda i: (i, CK // C_KV_HEADS)),
                  pl.BlockSpec((ts, HEAD_DIM), lambda i: (i, 0)),
                  pl.BlockSpec((ts, HEAD_DIM), lambda i: (i, 0)),
                  pl.BlockSpec((1, HEAD_DIM), lambda i: (0, 0))],
        out_specs=[pl.BlockSpec((A_HEADS, ts // A_TILE, HEAD_DIM, A_TILE), lambda i: (0, i, 0, 0)),
                   pl.BlockSpec((B_KV_HEADS, ts // B_TILE, HEAD_DIM, B_TILE), lambda i: (0, i, 0, 0)),
                   pl.BlockSpec((C_KV_HEADS, ts // C_TK, HEAD_DIM, C_TK), lambda i: (0, i, 0, 0)),
                   pl.BlockSpec((ts, C_KV_HEADS * HEAD_DIM), lambda i: (i, 0))],
        out_shape=[jax.ShapeDtypeStruct((A_HEADS, s // A_TILE, HEAD_DIM, A_TILE), bf),
                   jax.ShapeDtypeStruct((B_KV_HEADS, s // B_TILE, HEAD_DIM, B_TILE), bf),
                   jax.ShapeDtypeStruct((C_KV_HEADS, s // C_TK, HEAD_DIM, C_TK), bf),
                   jax.ShapeDtypeStruct((s, C_KV_HEADS * HEAD_DIM), bf)],
        compiler_params=_cparams(1),
        name="vt_krope_prep",
    )(proj, proj, proj, proj, cos, sin, k_gain)


def _flash_kernel(*refs, group, tq, tk, nkv, reach, has_bias, has_sink, has_qnorm):
    it = iter(refs)
    q_refs = [next(it) for _ in range(group)]
    g_refs = [next(it) for _ in range(group)]
    k_ref = next(it)
    vt_ref = next(it)
    bias_ref = next(it) if has_bias else None
    sink_ref = next(it) if has_sink else None
    if has_qnorm:
        cos_ref, sin_ref, qg_ref = next(it), next(it), next(it)
    o_ref = next(it)
    m_ref, l_ref, acc_ref = next(it), next(it), next(it)

    h = pl.program_id(0)
    qb = pl.program_id(1)
    scale2 = HEAD_DIM ** -0.5 * LOG2E

    qts = []
    for g in range(group):
        q = q_refs[g][...].astype(jnp.float32)
        if has_qnorm:
            q = _rope(_rms(q) * qg_ref[...], cos_ref[...], sin_ref[...])
        qts.append((q * scale2).T.astype(jnp.bfloat16))
        if has_sink:
            m_ref[g] = jnp.full((1, tq), sink_ref[h * group + g] * LOG2E, jnp.float32)
            l_ref[g] = jnp.ones((1, tq), jnp.float32)
        else:
            m_ref[g] = jnp.full((1, tq), M_INIT, jnp.float32)
            l_ref[g] = jnp.zeros((1, tq), jnp.float32)
        acc_ref[g] = jnp.zeros((HEAD_DIM, tq), jnp.float32)

    def body(j, carry):
        kt = k_ref[pl.ds(pl.multiple_of(j * tk, tk), tk), :]
        vt = vt_ref[0, j]
        for g in range(group):
            s = jnp.dot(kt, qts[g], preferred_element_type=jnp.float32)
            if has_bias:
                s = s + bias_ref[g if bias_ref.shape[0] > 1 else 0, j - qb + reach]
            m_old = m_ref[g]
            m_new = jnp.maximum(m_old, jnp.max(s, axis=0, keepdims=True))
            alpha = jnp.exp2(m_old - m_new)
            p = jnp.exp2(s - m_new)
            l_ref[g] = alpha * l_ref[g] + jnp.sum(p, axis=0, keepdims=True)
            acc_ref[g] = alpha * acc_ref[g] + jnp.dot(vt, p.astype(jnp.bfloat16),
                                                      preferred_element_type=jnp.float32)
            m_ref[g] = m_new
        return carry

    if reach is None:
        lax.fori_loop(0, nkv, body, 0)
    else:
        lax.fori_loop(jnp.maximum(qb - reach, 0), jnp.minimum(qb + reach, nkv - 1) + 1, body, 0)

    for g in range(group):
        o = (acc_ref[g] / l_ref[g]).T
        gate = g_refs[g][...].astype(jnp.float32)
        o_ref[:, g * HEAD_DIM:(g + 1) * HEAD_DIM] = (o * (gate * jax.nn.sigmoid(gate))).astype(o_ref.dtype)


def _flash(proj, k_src, k_col, vt, q_col, g_col, n_kv_heads, group, tq, tk, reach=None,
           bias=None, sink=None, qnorm=None):
    s = proj.shape[0]
    assert s % tq == 0 and s % tk == 0
    nkv = s // tk
    if reach is not None:
        assert tq == tk
    in_specs, args = [], []
    for col in (q_col, g_col):
        for g in range(group):
            in_specs.append(pl.BlockSpec((tq, HEAD_DIM),
                                         lambda h, qb, col=col, g=g: (qb, col + h * group + g)))
            args.append(proj)
    in_specs.append(pl.BlockSpec((s, HEAD_DIM), lambda h, qb: (0, k_col + h)))
    args.append(k_src)
    in_specs.append(pl.BlockSpec((1, nkv, HEAD_DIM, tk), lambda h, qb: (h, 0, 0, 0)))
    args.append(vt)
    if bias is not None:
        per = bias.shape[0] // n_kv_heads
        in_specs.append(pl.BlockSpec((per,) + bias.shape[1:], lambda h, qb: (h, 0, 0, 0)))
        args.append(bias)
    if sink is not None:
        in_specs.append(pl.BlockSpec(memory_space=pltpu.SMEM))
        args.append(sink)
    if qnorm is not None:
        cos, sin, qg = qnorm
        in_specs += [pl.BlockSpec((tq, HEAD_DIM), lambda h, qb: (qb, 0)),
                     pl.BlockSpec((tq, HEAD_DIM), lambda h, qb: (qb, 0)),
                     pl.BlockSpec((1, HEAD_DIM), lambda h, qb: (0, 0))]
        args += [cos, sin, qg]
    kern = functools.partial(_flash_kernel, group=group, tq=tq, tk=tk, nkv=nkv, reach=reach,
                             has_bias=bias is not None, has_sink=sink is not None,
                             has_qnorm=qnorm is not None)
    return pl.pallas_call(
        kern,
        grid=(n_kv_heads, s // tq),
        in_specs=in_specs,
        out_specs=pl.BlockSpec((tq, group * HEAD_DIM), lambda h, qb: (qb, h)),
        out_shape=jax.ShapeDtypeStruct((s, n_kv_heads * group * HEAD_DIM), jnp.bfloat16),
        scratch_shapes=[pltpu.VMEM((group, 1, tq), jnp.float32),
                        pltpu.VMEM((group, 1, tq), jnp.float32),
                        pltpu.VMEM((group, HEAD_DIM, tq), jnp.float32)],
        compiler_params=_cparams(2),
        name="flash_g%d_%s" % (group, "dense" if reach is None else "band%d" % reach),
    )(*args)


def _t5_bucket_np(rel):
    half_b = NUM_BUCKETS // 2
    max_exact = half_b // 2
    ret = np.where(rel > 0, half_b, 0)
    n = np.abs(rel)
    nf = np.maximum(n, 1).astype(np.float32)
    large = max_exact + (np.log(nf / np.float32(max_exact)) / np.float32(math.log(MAX_DISTANCE / max_exact))
                         * np.float32(half_b - max_exact)).astype(np.int32)
    large = np.minimum(large, half_b - 1)
    return (ret + np.where(n < max_exact, n, large)).astype(np.int32)


def _toeplitz_tiles(vec, tile, reach):
    r = (vec.shape[1] - 1) // 2
    off = np.arange(-reach, reach + 1)[:, None, None] * tile
    d = off + np.arange(tile)[None, :, None] - np.arange(tile)[None, None, :]
    idx = np.clip(d + r, 0, 2 * r).astype(np.int32)
    tiles = jnp.take(vec, jnp.asarray(idx), axis=1)
    return jnp.where(jnp.asarray(np.abs(d) <= r)[None], tiles, MASKED)


def _bias_tiles(rel_bias):
    d = np.arange(-A_REACH, A_REACH + 1)
    bucket = _t5_bucket_np(d)
    count = np.zeros(d.shape, np.float64)
    for w, dil in A_PATTERNS:
        count += (d % dil == 0) & (np.abs(d) <= w // 2)
    log_count = np.where(count > 0, np.log(np.maximum(count, 1.0)), 0.0).astype(np.float32)
    table = rel_bias.astype(jnp.float32)
    vec_a = (table[bucket, :A_HEADS].T + log_count[None]) * LOG2E
    vec_a = jnp.where(jnp.asarray(count > 0)[None], vec_a, MASKED)
    tiles_a = _toeplitz_tiles(vec_a, A_TILE, -(-A_REACH // A_TILE))
    db = np.arange(-B_HALF_WINDOW, B_HALF_WINDOW + 1)
    vec_b = table[_t5_bucket_np(db), A_HEADS:].T * LOG2E
    tiles_b = _toeplitz_tiles(vec_b, B_TILE, -(-B_HALF_WINDOW // B_TILE))
    return tiles_a, tiles_b


def kernel(x, c, w_mod, b_mod, pre_norm, post_norm, w_in, w_out, q_norm, k_norm, sink, rel_bias):
    bsz, s, d = x.shape
    assert bsz == 1 and c.shape[0] == 1
    depth = w_in.shape[0]
    assert w_in.shape[2] == IN_WIDTH and w_out.shape[1] == A_W + B_W + C_W
    bf = jnp.bfloat16
    x2 = x[0]
    w_in_b = w_in.astype(bf)
    w_out_b = w_out.astype(bf)
    cos, sin = _rope_tables(s)
    tiles_a, tiles_b = _bias_tiles(rel_bias)
    mod = _modulation(c, w_mod, b_mod)[:, 0]
    shift, scale, gate = mod[:, :d], mod[:, d:2 * d], mod[:, 2 * d:]
    vec = lambda a, i: a[i][None, :]

    h = _prenorm(x2, vec(pre_norm, 0), vec(scale, 0), vec(shift, 0))
    for i in range(depth):
        proj = _in_projection(h, w_in_b[i])
        vta, vtb, vtc, kc = _prep(proj, cos, sin, vec(k_norm, i))
        oa = _flash(proj, proj, AK, vta, AQ, AG, A_HEADS, 1, A_TILE, A_TILE,
                    reach=-(-A_REACH // A_TILE), bias=tiles_a)
        ob = _flash(proj, proj, BK, vtb, BQ, BG, B_KV_HEADS, B_HEADS // B_KV_HEADS, B_TILE, B_TILE,
                    reach=-(-B_HALF_WINDOW // B_TILE), bias=tiles_b, sink=sink[i])
        oc = _flash(proj, kc, 0, vtc, CQ, CG, C_KV_HEADS, C_HEADS // C_KV_HEADS, C_TQ, C_TK,
                    qnorm=(cos, sin, vec(q_norm, i)))
        y = _out_projection(oa, ob, oc, w_out_b[i, :A_W], w_out_b[i, A_W:A_W + B_W],
                            w_out_b[i, A_W + B_W:])
        nxt = None if i == depth - 1 else (vec(pre_norm, i + 1), vec(scale, i + 1), vec(shift, i + 1))
        x2, h = _post(x2, y, vec(post_norm, i), vec(gate, i), nxt)
    return x2[None]
```

```python
import functools
import math

import numpy as np
import jax
import jax.numpy as jnp
from jax import lax
from jax.experimental import pallas as pl
from jax.experimental.pallas import tpu as pltpu

HEAD_DIM = 128
GRID_W = 64
A_HEADS = 12
A_PATTERNS = ((128, 1), (512, 4), (2048, 16))
B_HEADS = 8
B_KV_HEADS = 2
B_HALF_WINDOW = 128
C_HEADS = 12
C_KV_HEADS = 4
ROPE_THETA = 10000.0
NUM_BUCKETS = 32
MAX_DISTANCE = 1024
EPS = 1e-6
LOG2E = math.log2(math.e)
MASKED = -1e30
M_INIT = -1e29

A_W = A_HEADS * HEAD_DIM
B_W = B_HEADS * HEAD_DIM
C_W = C_HEADS * HEAD_DIM
AQ, AK, AV, AG = 0, 12, 24, 36
BQ, BK, BV, BG = 48, 56, 58, 60
CQ, CK, CV, CG = 68, 80, 84, 88
IN_WIDTH = 100 * HEAD_DIM

A_REACH = max(w // 2 for w, _ in A_PATTERNS)
A_TILE = 512
B_TILE = 256
C_TQ = 256
C_TK = 512
V7X_VMEM_LIMIT = 48 * 1024 * 1024


def _cparams(n_axes):
    return pltpu.CompilerParams(dimension_semantics=("arbitrary",) * n_axes,
                                vmem_limit_bytes=V7X_VMEM_LIMIT)


def _tile(n, pref):
    t = min(n, pref)
    while n % t:
        t //= 2
    return t


def _mod_kernel(c_ref, w_ref, b_ref, o_ref):
    c = c_ref[...]
    cs = jnp.broadcast_to(c * jax.nn.sigmoid(c), (8, c.shape[1]))
    r = jnp.dot(cs, w_ref[0], preferred_element_type=jnp.float32)
    o_ref[0] = r[0:1] + b_ref[0]


def _modulation(c, w_mod, b_mod):
    depth, d, n = w_mod.shape
    tn = _tile(n, 1024)
    out = pl.pallas_call(
        _mod_kernel,
        grid=(depth, n // tn),
        in_specs=[pl.BlockSpec((1, d), lambda i, j: (0, 0)),
                  pl.BlockSpec((1, d, tn), lambda i, j: (i, 0, j)),
                  pl.BlockSpec((1, 1, tn), lambda i, j: (i, 0, j))],
        out_specs=pl.BlockSpec((1, 1, tn), lambda i, j: (i, 0, j)),
        out_shape=jax.ShapeDtypeStruct((depth, 1, n), jnp.float32),
        compiler_params=_cparams(2),
        name="modulation",
    )(c, w_mod, b_mod.reshape(depth, 1, n))
    return out


def _rms(x):
    return x * lax.rsqrt(jnp.mean(x * x, axis=-1, keepdims=True) + EPS)


def _prenorm_kernel(x_ref, g_ref, sc_ref, sh_ref, h_ref):
    y = _rms(x_ref[...]) * g_ref[...]
    h_ref[...] = (y * (1.0 + sc_ref[...]) + sh_ref[...]).astype(h_ref.dtype)


def _prenorm(x, g, scale, shift):
    s, d = x.shape
    tm = _tile(s, 512)
    vec = pl.BlockSpec((1, d), lambda i: (0, 0))
    return pl.pallas_call(
        _prenorm_kernel,
        grid=(s // tm,),
        in_specs=[pl.BlockSpec((tm, d), lambda i: (i, 0)), vec, vec, vec],
        out_specs=pl.BlockSpec((tm, d), lambda i: (i, 0)),
        out_shape=jax.ShapeDtypeStruct((s, d), jnp.bfloat16),
        compiler_params=_cparams(1),
        name="prenorm",
    )(x, g, scale, shift)


def _post_kernel(x_ref, y_ref, pg_ref, gate_ref, xo_ref):
    yn = _rms(y_ref[...].astype(jnp.float32)) * pg_ref[...]
    xo_ref[...] = x_ref[...] + gate_ref[...] * yn


def _postpre_kernel(x_ref, y_ref, pg_ref, gate_ref, g_ref, sc_ref, sh_ref, xo_ref, h_ref):
    yn = _rms(y_ref[...].astype(jnp.float32)) * pg_ref[...]
    xn = x_ref[...] + gate_ref[...] * yn
    xo_ref[...] = xn
    h = _rms(xn) * g_ref[...]
    h_ref[...] = (h * (1.0 + sc_ref[...]) + sh_ref[...]).astype(h_ref.dtype)


def _post(x, y, post_g, gate, nxt=None):
    s, d = x.shape
    tm = _tile(s, 256)
    row = pl.BlockSpec((tm, d), lambda i: (i, 0))
    vec = pl.BlockSpec((1, d), lambda i: (0, 0))
    if nxt is None:
        return pl.pallas_call(
            _post_kernel, grid=(s // tm,),
            in_specs=[row, row, vec, vec], out_specs=row,
            out_shape=jax.ShapeDtypeStruct((s, d), jnp.float32),
            compiler_params=_cparams(1), name="postnorm",
        )(x, y, post_g, gate), None
    g, scale, shift = nxt
    return pl.pallas_call(
        _postpre_kernel, grid=(s // tm,),
        in_specs=[row, row, vec, vec, vec, vec, vec], out_specs=[row, row],
        out_shape=[jax.ShapeDtypeStruct((s, d), jnp.float32),
                   jax.ShapeDtypeStruct((s, d), jnp.bfloat16)],
        compiler_params=_cparams(1), name="postnorm_prenorm",
    )(x, y, post_g, gate, g, scale, shift)


def _mm_kernel(a_ref, b_ref, o_ref):
    o_ref[...] = jnp.dot(a_ref[...], b_ref[...],
                         preferred_element_type=jnp.float32).astype(o_ref.dtype)


def _in_projection(h, w):
    m, k = h.shape
    n = w.shape[1]
    tm, tn = _tile(m, 1024), _tile(n, 512)
    return pl.pallas_call(
        _mm_kernel,
        grid=(m // tm, n // tn),
        in_specs=[pl.BlockSpec((tm, k), lambda i, j: (i, 0)),
                  pl.BlockSpec((k, tn), lambda i, j: (0, j))],
        out_specs=pl.BlockSpec((tm, tn), lambda i, j: (i, j)),
        out_shape=jax.ShapeDtypeStruct((m, n), jnp.bfloat16),
        compiler_params=_cparams(2),
        name="in_projection",
    )(h, w)


def _mm3_kernel(a1, a2, a3, b1, b2, b3, o_ref):
    acc = jnp.dot(a1[...], b1[...], preferred_element_type=jnp.float32)
    acc += jnp.dot(a2[...], b2[...], preferred_element_type=jnp.float32)
    acc += jnp.dot(a3[...], b3[...], preferred_element_type=jnp.float32)
    o_ref[...] = acc.astype(o_ref.dtype)


def _out_projection(oa, ob, oc, wa, wb, wc):
    m = oa.shape[0]
    n = wa.shape[1]
    tm, tn = _tile(m, 1024), _tile(n, 512)
    a_spec = lambda a: pl.BlockSpec((tm, a.shape[1]), lambda i, j: (i, 0))
    b_spec = lambda b: pl.BlockSpec((b.shape[0], tn), lambda i, j: (0, j))
    return pl.pallas_call(
        _mm3_kernel,
        grid=(m // tm, n // tn),
        in_specs=[a_spec(oa), a_spec(ob), a_spec(oc), b_spec(wa), b_spec(wb), b_spec(wc)],
        out_specs=pl.BlockSpec((tm, tn), lambda i, j: (i, j)),
        out_shape=jax.ShapeDtypeStruct((m, n), jnp.float32),
        compiler_params=_cparams(2),
        name="out_projection",
    )(oa, ob, oc, wa, wb, wc)


def _rope_tables(s):
    rows = s // GRID_W
    row = jnp.repeat(jnp.arange(rows), GRID_W)
    col = jnp.tile(jnp.arange(GRID_W), rows)
    quarter = HEAD_DIM // 4
    freqs = ROPE_THETA ** (-jnp.arange(0, 2 * quarter, 2, dtype=jnp.float32) / (2 * quarter))
    ar = row.astype(jnp.float32)[:, None] * freqs[None, :]
    ac = col.astype(jnp.float32)[:, None] * freqs[None, :]
    cr, sr, cc, sc = jnp.cos(ar), jnp.sin(ar), jnp.cos(ac), jnp.sin(ac)
    cos = jnp.concatenate([cr, cr, cc, cc], axis=-1)
    sin = jnp.concatenate([-sr, sr, -sc, sc], axis=-1)
    return cos, sin


def _rope(x, cos, sin):
    quarter = HEAD_DIM // 4
    fwd = pltpu.roll(x, HEAD_DIM - quarter, 1)
    bwd = pltpu.roll(x, quarter, 1)
    lane = lax.broadcasted_iota(jnp.int32, x.shape, 1)
    partner = jnp.where((lane % (2 * quarter)) < quarter, fwd, bwd)
    return x * cos + partner * sin


def _prep_kernel(av_ref, bv_ref, cv_ref, ck_ref, cos_ref, sin_ref, kg_ref,
                 vta_ref, vtb_ref, vtc_ref, kc_ref, *, ts):
    def put_t(src_ref, n_heads, dst_ref, chunk):
        for hh in range(n_heads):
            vt = src_ref[:, hh * HEAD_DIM:(hh + 1) * HEAD_DIM].astype(jnp.float32).T
            for cidx in range(ts // chunk):
                dst_ref[hh, cidx] = vt[:, cidx * chunk:(cidx + 1) * chunk].astype(dst_ref.dtype)

    put_t(av_ref, A_HEADS, vta_ref, A_TILE)
    put_t(bv_ref, B_KV_HEADS, vtb_ref, B_TILE)
    put_t(cv_ref, C_KV_HEADS, vtc_ref, C_TK)
    cos, sin, kg = cos_ref[...], sin_ref[...], kg_ref[...]
    for hh in range(C_KV_HEADS):
        k = ck_ref[:, hh * HEAD_DIM:(hh + 1) * HEAD_DIM].astype(jnp.float32)
        k = _rope(_rms(k) * kg, cos, sin)
        kc_ref[:, hh * HEAD_DIM:(hh + 1) * HEAD_DIM] = k.astype(kc_ref.dtype)


def _prep(proj, cos, sin, k_gain):
    s = proj.shape[0]
    ts = 512
    assert s % ts == 0
    bf = jnp.bfloat16
    n = s // ts
    return pl.pallas_call(
        functools.partial(_prep_kernel, ts=ts),
        grid=(n,),
        in_specs=[pl.BlockSpec((ts, A_W), lambda i: (i, AV * HEAD_DIM // A_W)),
                  pl.BlockSpec((ts, B_KV_HEADS * HEAD_DIM), lambda i: (i, BV // B_KV_HEADS)),
                  pl.BlockSpec((ts, C_KV_HEADS * HEAD_DIM), lambda i: (i, CV // C_KV_HEADS)),
                  pl.BlockSpec((ts, C_KV_HEADS * HEAD_DIM), lambda i: (i, CK // C_KV_HEADS)),
                  pl.BlockSpec((ts, HEAD_DIM), lambda i: (i, 0)),
                  pl.BlockSpec((ts, HEAD_DIM), lambda i: (i, 0)),
                  pl.BlockSpec((1, HEAD_DIM), lambda i: (0, 0))],
        out_specs=[pl.BlockSpec((A_HEADS, ts // A_TILE, HEAD_DIM, A_TILE), lambda i: (0, i, 0, 0)),
                   pl.BlockSpec((B_KV_HEADS, ts // B_TILE, HEAD_DIM, B_TILE), lambda i: (0, i, 0, 0)),
                   pl.BlockSpec((C_KV_HEADS, ts // C_TK, HEAD_DIM, C_TK), lambda i: (0, i, 0, 0)),
                   pl.BlockSpec((ts, C_KV_HEADS * HEAD_DIM), lambda i: (i, 0))],
        out_shape=[jax.ShapeDtypeStruct((A_HEADS, s // A_TILE, HEAD_DIM, A_TILE), bf),
                   jax.ShapeDtypeStruct((B_KV_HEADS, s // B_TILE, HEAD_DIM, B_TILE), bf),
                   jax.ShapeDtypeStruct((C_KV_HEADS, s // C_TK, HEAD_DIM, C_TK), bf),
                   jax.ShapeDtypeStruct((s, C_KV_HEADS * HEAD_DIM), bf)],
        compiler_params=_cparams(1),
        name="vt_krope_prep",
    )(proj, proj, proj, proj, cos, sin, k_gain)


def _flash_kernel(*refs, group, tq, tk, nkv, reach, has_bias, has_sink, has_qnorm):
    it = iter(refs)
    q_refs = [next(it) for _ in range(group)]
    g_refs = [next(it) for _ in range(group)]
    k_ref = next(it)
    vt_ref = next(it)
    bias_ref = next(it) if has_bias else None
    sink_ref = next(it) if has_sink else None
    if has_qnorm:
        cos_ref, sin_ref, qg_ref = next(it), next(it), next(it)
    o_ref = next(it)
    m_ref, l_ref, acc_ref = next(it), next(it), next(it)

    h = pl.program_id(0)
    qb = pl.program_id(1)
    scale2 = HEAD_DIM ** -0.5 * LOG2E

    qts = []
    for g in range(group):
        q = q_refs[g][...].astype(jnp.float32)
        if has_qnorm:
            q = _rope(_rms(q) * qg_ref[...], cos_ref[...], sin_ref[...])
        qts.append((q * scale2).T.astype(jnp.bfloat16))
        if has_sink:
            m_ref[g] = jnp.full((1, tq), sink_ref[h * group + g] * LOG2E, jnp.float32)
            l_ref[g] = jnp.ones((1, tq), jnp.float32)
        else:
            m_ref[g] = jnp.full((1, tq), M_INIT, jnp.float32)
            l_ref[g] = jnp.zeros((1, tq), jnp.float32)
        acc_ref[g] = jnp.zeros((HEAD_DIM, tq), jnp.float32)

    def body(j, carry):
        kt = k_ref[pl.ds(pl.multiple_of(j * tk, tk), tk), :]
        vt = vt_ref[0, j]
        for g in range(group):
            s = jnp.dot(kt, qts[g], preferred_element_type=jnp.float32)
            if has_bias:
                s = s + bias_ref[g if bias_ref.shape[0] > 1 else 0, j - qb + reach]
            m_old = m_ref[g]
            m_new = jnp.maximum(m_old, jnp.max(s, axis=0, keepdims=True))
            alpha = jnp.exp2(m_old - m_new)
            p = jnp.exp2(s - m_new)
            l_ref[g] = alpha * l_ref[g] + jnp.sum(p, axis=0, keepdims=True)
            acc_ref[g] = alpha * acc_ref[g] + jnp.dot(vt, p.astype(jnp.bfloat16),
                                                      preferred_element_type=jnp.float32)
            m_ref[g] = m_new
        return carry

    if reach is None:
        lax.fori_loop(0, nkv, body, 0)
    else:
        lax.fori_loop(jnp.maximum(qb - reach, 0), jnp.minimum(qb + reach, nkv - 1) + 1, body, 0)

    for g in range(group):
        o = (acc_ref[g] / l_ref[g]).T
        gate = g_refs[g][...].astype(jnp.float32)
        o_ref[:, g * HEAD_DIM:(g + 1) * HEAD_DIM] = (o * (gate * jax.nn.sigmoid(gate))).astype(o_ref.dtype)


def _flash(proj, k_src, k_col, vt, q_col, g_col, n_kv_heads, group, tq, tk, reach=None,
           bias=None, sink=None, qnorm=None):
    s = proj.shape[0]
    assert s % tq == 0 and s % tk == 0
    nkv = s // tk
    if reach is not None:
        assert tq == tk
    in_specs, args = [], []
    for col in (q_col, g_col):
        for g in range(group):
            in_specs.append(pl.BlockSpec((tq, HEAD_DIM),
                                         lambda h, qb, col=col, g=g: (qb, col + h * group + g)))
            args.append(proj)
    in_specs.append(pl.BlockSpec((s, HEAD_DIM), lambda h, qb: (0, k_col + h)))
    args.append(k_src)
    in_specs.append(pl.BlockSpec((1, nkv, HEAD_DIM, tk), lambda h, qb: (h, 0, 0, 0)))
    args.append(vt)
    if bias is not None:
        per = bias.shape[0] // n_kv_heads
        in_specs.append(pl.BlockSpec((per,) + bias.shape[1:], lambda h, qb: (h, 0, 0, 0)))
        args.append(bias)
    if sink is not None:
        in_specs.append(pl.BlockSpec(memory_space=pltpu.SMEM))
        args.append(sink)
    if qnorm is not None:
        cos, sin, qg = qnorm
        in_specs += [pl.BlockSpec((tq, HEAD_DIM), lambda h, qb: (qb, 0)),
                     pl.BlockSpec((tq, HEAD_DIM), lambda h, qb: (qb, 0)),
                     pl.BlockSpec((1, HEAD_DIM), lambda h, qb: (0, 0))]
        args += [cos, sin, qg]
    kern = functools.partial(_flash_kernel, group=group, tq=tq, tk=tk, nkv=nkv, reach=reach,
                             has_bias=bias is not None, has_sink=sink is not None,
                             has_qnorm=qnorm is not None)
    return pl.pallas_call(
        kern,
        grid=(n_kv_heads, s // tq),
        in_specs=in_specs,
        out_specs=pl.BlockSpec((tq, group * HEAD_DIM), lambda h, qb: (qb, h)),
        out_shape=jax.ShapeDtypeStruct((s, n_kv_heads * group * HEAD_DIM), jnp.bfloat16),
        scratch_shapes=[pltpu.VMEM((group, 1, tq), jnp.float32),
                        pltpu.VMEM((group, 1, tq), jnp.float32),
                        pltpu.VMEM((group, HEAD_DIM, tq), jnp.float32)],
        compiler_params=_cparams(2),
        name="flash_g%d_%s" % (group, "dense" if reach is None else "band%d" % reach),
    )(*args)


def _dense_kernel(*refs, group, tq, tk, nkv):
    it = iter(refs)
    q_refs = [next(it) for _ in range(group)]
    g_refs = [next(it) for _ in range(group)]
    k_ref, vt_ref, cos_ref, sin_ref, qg_ref, o_ref = (next(it) for _ in range(6))
    qt_ref, s_ref, mx_ref, m_ref, l_ref, acc_ref = (next(it) for _ in range(6))
    scale2 = HEAD_DIM ** -0.5 * LOG2E

    for g in range(group):
        q = q_refs[g][...].astype(jnp.float32)
        q = _rope(_rms(q) * qg_ref[...], cos_ref[...], sin_ref[...])
        qt_ref[:, g * tq:(g + 1) * tq] = (q * scale2).T.astype(jnp.bfloat16)
    m_ref[...] = jnp.full(m_ref.shape, M_INIT, jnp.float32)
    l_ref[...] = jnp.zeros(l_ref.shape, jnp.float32)
    acc_ref[...] = jnp.zeros(acc_ref.shape, jnp.float32)

    def scores(jblk, slot):
        kt = k_ref[pl.ds(pl.multiple_of(jblk * tk, tk), tk), :]
        s = jnp.dot(kt, qt_ref[...], preferred_element_type=jnp.float32)
        s_ref[slot] = s
        mx_ref[slot] = jnp.max(s, axis=0, keepdims=True)

    def accumulate(jblk, slot):
        m_old = m_ref[...]
        m_new = jnp.maximum(m_old, mx_ref[slot])
        alpha = jnp.exp2(m_old - m_new)
        p = jnp.exp2(s_ref[slot] - m_new)
        l_ref[...] = alpha * l_ref[...] + jnp.sum(p, axis=0, keepdims=True)
        acc_ref[...] = alpha * acc_ref[...] + jnp.dot(vt_ref[0, jblk], p.astype(jnp.bfloat16),
                                                      preferred_element_type=jnp.float32)
        m_ref[...] = m_new

    scores(0, 0)

    def pair(jj, carry):
        j0 = 2 * jj
        scores(j0 + 1, 1)
        accumulate(j0, 0)
        scores(jnp.minimum(j0 + 2, nkv - 1), 0)
        accumulate(j0 + 1, 1)
        return carry

    lax.fori_loop(0, nkv // 2, pair, 0)

    for g in range(group):
        sl = slice(g * tq, (g + 1) * tq)
        o = (acc_ref[:, sl] / l_ref[:, sl]).T
        gate = g_refs[g][...].astype(jnp.float32)
        o_ref[:, g * HEAD_DIM:(g + 1) * HEAD_DIM] = (o * (gate * jax.nn.sigmoid(gate))).astype(o_ref.dtype)


def _flash_dense(proj, kc, vt, q_col, g_col, n_kv_heads, group, tq, tk, qnorm):
    s = proj.shape[0]
    nkv = s // tk
    assert s % tq == 0 and s % tk == 0 and nkv % 2 == 0
    in_specs, args = [], []
    for col in (q_col, g_col):
        for g in range(group):
            in_specs.append(pl.BlockSpec((tq, HEAD_DIM),
                                         lambda h, qb, col=col, g=g: (qb, col + h * group + g)))
            args.append(proj)
    cos, sin, qg = qnorm
    in_specs += [pl.BlockSpec((s, HEAD_DIM), lambda h, qb: (0, h)),
                 pl.BlockSpec((1, nkv, HEAD_DIM, tk), lambda h, qb: (h, 0, 0, 0)),
                 pl.BlockSpec((tq, HEAD_DIM), lambda h, qb: (qb, 0)),
                 pl.BlockSpec((tq, HEAD_DIM), lambda h, qb: (qb, 0)),
                 pl.BlockSpec((1, HEAD_DIM), lambda h, qb: (0, 0))]
    args += [kc, vt, cos, sin, qg]
    n = group * tq
    return pl.pallas_call(
        functools.partial(_dense_kernel, group=group, tq=tq, tk=tk, nkv=nkv),
        grid=(n_kv_heads, s // tq),
        in_specs=in_specs,
        out_specs=pl.BlockSpec((tq, group * HEAD_DIM), lambda h, qb: (qb, h)),
        out_shape=jax.ShapeDtypeStruct((s, n_kv_heads * group * HEAD_DIM), jnp.bfloat16),
        scratch_shapes=[pltpu.VMEM((HEAD_DIM, n), jnp.bfloat16),
                        pltpu.VMEM((2, tk, n), jnp.float32),
                        pltpu.VMEM((2, 1, n), jnp.float32),
                        pltpu.VMEM((1, n), jnp.float32),
                        pltpu.VMEM((1, n), jnp.float32),
                        pltpu.VMEM((HEAD_DIM, n), jnp.float32)],
        compiler_params=_cparams(2),
        name="flash_dense_g%d" % group,
    )(*args)


def _t5_bucket_np(rel):
    half_b = NUM_BUCKETS // 2
    max_exact = half_b // 2
    ret = np.where(rel > 0, half_b, 0)
    n = np.abs(rel)
    nf = np.maximum(n, 1).astype(np.float32)
    large = max_exact + (np.log(nf / np.float32(max_exact)) / np.float32(math.log(MAX_DISTANCE / max_exact))
                         * np.float32(half_b - max_exact)).astype(np.int32)
    large = np.minimum(large, half_b - 1)
    return (ret + np.where(n < max_exact, n, large)).astype(np.int32)


def _toeplitz_tiles(vec, tile, reach):
    heads = vec.shape[0]
    r = (vec.shape[1] - 1) // 2
    n_off = 2 * reach + 1
    span = 2 * tile - 1
    lo = -reach * tile - (tile - 1)
    hi = reach * tile + (tile - 1)
    padded = jnp.pad(vec, ((0, 0), (-lo - r, hi - r)), constant_values=MASKED)
    rows = [padded[:, o * tile:o * tile + span] for o in range(n_off)]
    u = jnp.stack(rows, axis=1)[:, :, ::-1]
    w = jnp.pad(u, ((0, 0), (0, 0), (0, 1)))
    flat = jnp.broadcast_to(w[:, :, None, :], (heads, n_off, tile, span + 1))
    flat = flat.reshape(heads, n_off, tile * (span + 1))[:, :, :tile * span]
    skew = flat.reshape(heads, n_off, tile, span)
    return skew[:, :, :, tile - 1:]


def _bias_tiles(rel_bias):
    hi = lax.Precision.HIGHEST
    d = np.arange(-A_REACH, A_REACH + 1)
    count = np.zeros(d.shape, np.float64)
    for w, dil in A_PATTERNS:
        count += (d % dil == 0) & (np.abs(d) <= w // 2)
    log_count = np.where(count > 0, np.log(np.maximum(count, 1.0)), 0.0).astype(np.float32)
    table = rel_bias.astype(jnp.float32)
    onehot_a = jnp.asarray(np.eye(NUM_BUCKETS, dtype=np.float32)[_t5_bucket_np(d)])
    vec_a = (jnp.dot(onehot_a, table[:, :A_HEADS], precision=hi).T + log_count[None]) * LOG2E
    vec_a = jnp.where(jnp.asarray(count > 0)[None], vec_a, MASKED)
    tiles_a = _toeplitz_tiles(vec_a, A_TILE, -(-A_REACH // A_TILE))
    db = np.arange(-B_HALF_WINDOW, B_HALF_WINDOW + 1)
    onehot_b = jnp.asarray(np.eye(NUM_BUCKETS, dtype=np.float32)[_t5_bucket_np(db)])
    vec_b = jnp.dot(onehot_b, table[:, A_HEADS:], precision=hi).T * LOG2E
    tiles_b = _toeplitz_tiles(vec_b, B_TILE, -(-B_HALF_WINDOW // B_TILE))
    return tiles_a, tiles_b


def kernel(x, c, w_mod, b_mod, pre_norm, post_norm, w_in, w_out, q_norm, k_norm, sink, rel_bias):
    bsz, s, d = x.shape
    assert bsz == 1 and c.shape[0] == 1
    depth = w_in.shape[0]
    assert w_in.shape[2] == IN_WIDTH and w_out.shape[1] == A_W + B_W + C_W
    bf = jnp.bfloat16
    x2 = x[0]
    w_in_b = w_in.astype(bf)
    w_out_b = w_out.astype(bf)
    cos, sin = _rope_tables(s)
    tiles_a, tiles_b = _bias_tiles(rel_bias)
    mod = _modulation(c, w_mod, b_mod)[:, 0]
    shift, scale, gate = mod[:, :d], mod[:, d:2 * d], mod[:, 2 * d:]
    vec = lambda a, i: a[i][None, :]

    h = _prenorm(x2, vec(pre_norm, 0), vec(scale, 0), vec(shift, 0))
    for i in range(depth):
        proj = _in_projection(h, w_in_b[i])
        vta, vtb, vtc, kc = _prep(proj, cos, sin, vec(k_norm, i))
        oa = _flash(proj, proj, AK, vta, AQ, AG, A_HEADS, 1, A_TILE, A_TILE,
                    reach=-(-A_REACH // A_TILE), bias=tiles_a)
        ob = _flash(proj, proj, BK, vtb, BQ, BG, B_KV_HEADS, B_HEADS // B_KV_HEADS, B_TILE, B_TILE,
                    reach=-(-B_HALF_WINDOW // B_TILE), bias=tiles_b, sink=sink[i])
        oc = _flash_dense(proj, kc, vtc, CQ, CG, C_KV_HEADS, C_HEADS // C_KV_HEADS, C_TQ, C_TK,
                          (cos, sin, vec(q_norm, i)))
        y = _out_projection(oa, ob, oc, w_out_b[i, :A_W], w_out_b[i, A_W:A_W + B_W],
                            w_out_b[i, A_W + B_W:])
        nxt = None if i == depth - 1 else (vec(pre_norm, i + 1), vec(scale, i + 1), vec(shift, i + 1))
        x2, h = _post(x2, y, vec(post_norm, i), vec(gate, i), nxt)
    return x2[None]
```

```python
import functools
import math

import numpy as np
import jax
import jax.numpy as jnp
from jax import lax
from jax.experimental import pallas as pl
from jax.experimental.pallas import tpu as pltpu

HEAD_DIM = 128
GRID_W = 64
A_HEADS = 12
A_PATTERNS = ((128, 1), (512, 4), (2048, 16))
B_HEADS = 8
B_KV_HEADS = 2
B_HALF_WINDOW = 128
C_HEADS = 12
C_KV_HEADS = 4
ROPE_THETA = 10000.0
NUM_BUCKETS = 32
MAX_DISTANCE = 1024
EPS = 1e-6
LOG2E = math.log2(math.e)
MASKED = -1e30
M_INIT = -1e29

A_W = A_HEADS * HEAD_DIM
B_W = B_HEADS * HEAD_DIM
C_W = C_HEADS * HEAD_DIM
AQ, AK, AV, AG = 0, 12, 24, 36
BQ, BK, BV, BG = 48, 56, 58, 60
CQ, CK, CV, CG = 68, 80, 84, 88
IN_WIDTH = 100 * HEAD_DIM

A_REACH = max(w // 2 for w, _ in A_PATTERNS)
A_TILE = 512
B_TILE = 256
C_TQ = 256
C_TK = 1024
VT_ROWS = HEAD_DIM + 16
V7X_VMEM_LIMIT = 48 * 1024 * 1024


def _cparams(n_axes):
    return pltpu.CompilerParams(dimension_semantics=("arbitrary",) * n_axes,
                                vmem_limit_bytes=V7X_VMEM_LIMIT)


def _tile(n, pref):
    t = min(n, pref)
    while n % t:
        t //= 2
    return t


def _mod_kernel(c_ref, w_ref, b_ref, o_ref):
    c = c_ref[...]
    cs = jnp.broadcast_to(c * jax.nn.sigmoid(c), (8, c.shape[1]))
    r = jnp.dot(cs, w_ref[0], preferred_element_type=jnp.float32)
    o_ref[0] = r[0:1] + b_ref[0]


def _modulation(c, w_mod, b_mod):
    depth, d, n = w_mod.shape
    tn = _tile(n, 1024)
    out = pl.pallas_call(
        _mod_kernel,
        grid=(depth, n // tn),
        in_specs=[pl.BlockSpec((1, d), lambda i, j: (0, 0)),
                  pl.BlockSpec((1, d, tn), lambda i, j: (i, 0, j)),
                  pl.BlockSpec((1, 1, tn), lambda i, j: (i, 0, j))],
        out_specs=pl.BlockSpec((1, 1, tn), lambda i, j: (i, 0, j)),
        out_shape=jax.ShapeDtypeStruct((depth, 1, n), jnp.float32),
        compiler_params=_cparams(2),
        name="modulation",
    )(c, w_mod, b_mod.reshape(depth, 1, n))
    return out


def _rms(x):
    return x * lax.rsqrt(jnp.mean(x * x, axis=-1, keepdims=True) + EPS)


def _prenorm_kernel(x_ref, g_ref, sc_ref, sh_ref, h_ref):
    y = _rms(x_ref[...]) * g_ref[...]
    h_ref[...] = (y * (1.0 + sc_ref[...]) + sh_ref[...]).astype(h_ref.dtype)


def _prenorm(x, g, scale, shift):
    s, d = x.shape
    tm = _tile(s, 512)
    vec = pl.BlockSpec((1, d), lambda i: (0, 0))
    return pl.pallas_call(
        _prenorm_kernel,
        grid=(s // tm,),
        in_specs=[pl.BlockSpec((tm, d), lambda i: (i, 0)), vec, vec, vec],
        out_specs=pl.BlockSpec((tm, d), lambda i: (i, 0)),
        out_shape=jax.ShapeDtypeStruct((s, d), jnp.bfloat16),
        compiler_params=_cparams(1),
        name="prenorm",
    )(x, g, scale, shift)


def _post_kernel(x_ref, y_ref, pg_ref, gate_ref, xo_ref):
    yn = _rms(y_ref[...].astype(jnp.float32)) * pg_ref[...]
    xo_ref[...] = x_ref[...] + gate_ref[...] * yn


def _postpre_kernel(x_ref, y_ref, pg_ref, gate_ref, g_ref, sc_ref, sh_ref, xo_ref, h_ref):
    yn = _rms(y_ref[...].astype(jnp.float32)) * pg_ref[...]
    xn = x_ref[...] + gate_ref[...] * yn
    xo_ref[...] = xn
    h = _rms(xn) * g_ref[...]
    h_ref[...] = (h * (1.0 + sc_ref[...]) + sh_ref[...]).astype(h_ref.dtype)


def _post(x, y, post_g, gate, nxt=None):
    s, d = x.shape
    tm = _tile(s, 256)
    row = pl.BlockSpec((tm, d), lambda i: (i, 0))
    vec = pl.BlockSpec((1, d), lambda i: (0, 0))
    if nxt is None:
        return pl.pallas_call(
            _post_kernel, grid=(s // tm,),
            in_specs=[row, row, vec, vec], out_specs=row,
            out_shape=jax.ShapeDtypeStruct((s, d), jnp.float32),
            compiler_params=_cparams(1), name="postnorm",
        )(x, y, post_g, gate), None
    g, scale, shift = nxt
    return pl.pallas_call(
        _postpre_kernel, grid=(s // tm,),
        in_specs=[row, row, vec, vec, vec, vec, vec], out_specs=[row, row],
        out_shape=[jax.ShapeDtypeStruct((s, d), jnp.float32),
                   jax.ShapeDtypeStruct((s, d), jnp.bfloat16)],
        compiler_params=_cparams(1), name="postnorm_prenorm",
    )(x, y, post_g, gate, g, scale, shift)


def _mm_kernel(a_ref, b_ref, o_ref):
    o_ref[...] = jnp.dot(a_ref[...], b_ref[...],
                         preferred_element_type=jnp.float32).astype(o_ref.dtype)


def _in_projection(h, w):
    m, k = h.shape
    n = w.shape[1]
    tm, tn = _tile(m, 1024), _tile(n, 512)
    return pl.pallas_call(
        _mm_kernel,
        grid=(m // tm, n // tn),
        in_specs=[pl.BlockSpec((tm, k), lambda i, j: (i, 0)),
                  pl.BlockSpec((k, tn), lambda i, j: (0, j))],
        out_specs=pl.BlockSpec((tm, tn), lambda i, j: (i, j)),
        out_shape=jax.ShapeDtypeStruct((m, n), jnp.bfloat16),
        compiler_params=_cparams(2),
        name="in_projection",
    )(h, w)


def _mm3_kernel(a1, a2, a3, b1, b2, b3, o_ref):
    acc = jnp.dot(a1[...], b1[...], preferred_element_type=jnp.float32)
    acc += jnp.dot(a2[...], b2[...], preferred_element_type=jnp.float32)
    acc += jnp.dot(a3[...], b3[...], preferred_element_type=jnp.float32)
    o_ref[...] = acc.astype(o_ref.dtype)


def _out_projection(oa, ob, oc, wa, wb, wc):
    m = oa.shape[0]
    n = wa.shape[1]
    tm, tn = _tile(m, 1024), _tile(n, 512)
    a_spec = lambda a: pl.BlockSpec((tm, a.shape[1]), lambda i, j: (i, 0))
    b_spec = lambda b: pl.BlockSpec((b.shape[0], tn), lambda i, j: (0, j))
    return pl.pallas_call(
        _mm3_kernel,
        grid=(m // tm, n // tn),
        in_specs=[a_spec(oa), a_spec(ob), a_spec(oc), b_spec(wa), b_spec(wb), b_spec(wc)],
        out_specs=pl.BlockSpec((tm, tn), lambda i, j: (i, j)),
        out_shape=jax.ShapeDtypeStruct((m, n), jnp.float32),
        compiler_params=_cparams(2),
        name="out_projection",
    )(oa, ob, oc, wa, wb, wc)


def _rope_tables(s):
    rows = s // GRID_W
    row = jnp.repeat(jnp.arange(rows), GRID_W)
    col = jnp.tile(jnp.arange(GRID_W), rows)
    quarter = HEAD_DIM // 4
    freqs = ROPE_THETA ** (-jnp.arange(0, 2 * quarter, 2, dtype=jnp.float32) / (2 * quarter))
    ar = row.astype(jnp.float32)[:, None] * freqs[None, :]
    ac = col.astype(jnp.float32)[:, None] * freqs[None, :]
    cr, sr, cc, sc = jnp.cos(ar), jnp.sin(ar), jnp.cos(ac), jnp.sin(ac)
    cos = jnp.concatenate([cr, cr, cc, cc], axis=-1)
    sin = jnp.concatenate([-sr, sr, -sc, sc], axis=-1)
    return cos, sin


def _rope(x, cos, sin):
    quarter = HEAD_DIM // 4
    fwd = pltpu.roll(x, HEAD_DIM - quarter, 1)
    bwd = pltpu.roll(x, quarter, 1)
    lane = lax.broadcasted_iota(jnp.int32, x.shape, 1)
    partner = jnp.where((lane % (2 * quarter)) < quarter, fwd, bwd)
    return x * cos + partner * sin


def _prep_kernel(av_ref, bv_ref, cv_ref, ck_ref, cos_ref, sin_ref, kg_ref,
                 vta_ref, vtb_ref, vtc_ref, kc_ref, *, ts):
    def put_t(src_ref, n_heads, dst_ref, chunk):
        for hh in range(n_heads):
            vt = src_ref[:, hh * HEAD_DIM:(hh + 1) * HEAD_DIM].astype(jnp.float32).T
            for cidx in range(ts // chunk):
                dst_ref[hh, cidx, :HEAD_DIM] = vt[:, cidx * chunk:(cidx + 1) * chunk].astype(dst_ref.dtype)
                dst_ref[hh, cidx, HEAD_DIM:] = jnp.ones((VT_ROWS - HEAD_DIM, chunk), dst_ref.dtype)

    put_t(av_ref, A_HEADS, vta_ref, A_TILE)
    put_t(bv_ref, B_KV_HEADS, vtb_ref, B_TILE)
    put_t(cv_ref, C_KV_HEADS, vtc_ref, C_TK)
    cos, sin, kg = cos_ref[...], sin_ref[...], kg_ref[...]
    for hh in range(C_KV_HEADS):
        k = ck_ref[:, hh * HEAD_DIM:(hh + 1) * HEAD_DIM].astype(jnp.float32)
        k = _rope(_rms(k) * kg, cos, sin)
        kc_ref[:, hh * HEAD_DIM:(hh + 1) * HEAD_DIM] = k.astype(kc_ref.dtype)


def _prep(proj, cos, sin, k_gain):
    s = proj.shape[0]
    ts = max(A_TILE, B_TILE, C_TK)
    assert s % ts == 0
    bf = jnp.bfloat16
    n = s // ts
    vt_spec = lambda heads, t: pl.BlockSpec((heads, ts // t, VT_ROWS, t), lambda i: (0, i, 0, 0))
    vt_shape = lambda heads, t: jax.ShapeDtypeStruct((heads, s // t, VT_ROWS, t), bf)
    return pl.pallas_call(
        functools.partial(_prep_kernel, ts=ts),
        grid=(n,),
        in_specs=[pl.BlockSpec((ts, A_W), lambda i: (i, AV * HEAD_DIM // A_W)),
                  pl.BlockSpec((ts, B_KV_HEADS * HEAD_DIM), lambda i: (i, BV // B_KV_HEADS)),
                  pl.BlockSpec((ts, C_KV_HEADS * HEAD_DIM), lambda i: (i, CV // C_KV_HEADS)),
                  pl.BlockSpec((ts, C_KV_HEADS * HEAD_DIM), lambda i: (i, CK // C_KV_HEADS)),
                  pl.BlockSpec((ts, HEAD_DIM), lambda i: (i, 0)),
                  pl.BlockSpec((ts, HEAD_DIM), lambda i: (i, 0)),
                  pl.BlockSpec((1, HEAD_DIM), lambda i: (0, 0))],
        out_specs=[vt_spec(A_HEADS, A_TILE), vt_spec(B_KV_HEADS, B_TILE), vt_spec(C_KV_HEADS, C_TK),
                   pl.BlockSpec((ts, C_KV_HEADS * HEAD_DIM), lambda i: (i, 0))],
        out_shape=[vt_shape(A_HEADS, A_TILE), vt_shape(B_KV_HEADS, B_TILE), vt_shape(C_KV_HEADS, C_TK),
                   jax.ShapeDtypeStruct((s, C_KV_HEADS * HEAD_DIM), bf)],
        compiler_params=_cparams(1),
        name="vt_krope_prep",
    )(proj, proj, proj, proj, cos, sin, k_gain)


def _scores(k_ref, row0, tk, qt_ref, s_ref, mx_ref, slot, bias=None):
    kt = k_ref[pl.ds(pl.multiple_of(row0, tk), tk), :]
    s = jnp.dot(kt, qt_ref[...], preferred_element_type=jnp.float32)
    if bias is not None:
        s = s + bias
    s_ref[slot] = s
    mx_ref[slot] = jnp.max(s, axis=0, keepdims=True)


def _accumulate(vt, s_ref, mx_ref, m_ref, acc_ref, slot):
    m_old = m_ref[...]
    m_new = jnp.maximum(m_old, mx_ref[slot])
    alpha = jnp.exp2(m_old - m_new)
    p = jnp.exp2(s_ref[slot] - m_new).astype(jnp.bfloat16)
    acc_ref[...] = alpha * acc_ref[...] + jnp.dot(vt, p, preferred_element_type=jnp.float32)
    m_ref[...] = m_new


def _load_q(q_refs, qt_ref, tq, prep=lambda q: q):
    scale2 = HEAD_DIM ** -0.5 * LOG2E
    for g, q_ref in enumerate(q_refs):
        q = prep(q_ref[...].astype(jnp.float32))
        qt_ref[:, g * tq:(g + 1) * tq] = (q * scale2).T.astype(qt_ref.dtype)


def _store_gated(acc_ref, g_refs, o_ref, tq):
    for g, g_ref in enumerate(g_refs):
        sl = slice(g * tq, (g + 1) * tq)
        o = (acc_ref[:HEAD_DIM, sl] / acc_ref[HEAD_DIM:HEAD_DIM + 1, sl]).T
        gate = g_ref[...].astype(jnp.float32)
        o_ref[:, g * HEAD_DIM:(g + 1) * HEAD_DIM] = (o * (gate * jax.nn.sigmoid(gate))).astype(o_ref.dtype)


def _band_kernel(*refs, group, t, nkv, reach, has_sink):
    it = iter(refs)
    q_refs = [next(it) for _ in range(group)]
    g_refs = [next(it) for _ in range(group)]
    k_ref, vt_ref, bias_ref = next(it), next(it), next(it)
    sink_ref = next(it) if has_sink else None
    o_ref = next(it)
    qt_ref, s_ref, mx_ref, m_ref, acc_ref = (next(it) for _ in range(5))
    h = pl.program_id(0)
    qb = pl.program_id(1)
    n_off = 2 * reach + 1

    _load_q(q_refs, qt_ref, t)
    acc_ref[:HEAD_DIM] = jnp.zeros((HEAD_DIM, group * t), jnp.float32)
    if has_sink:
        for g in range(group):
            m_ref[:, g * t:(g + 1) * t] = jnp.full((1, t), sink_ref[h * group + g] * LOG2E, jnp.float32)
        acc_ref[HEAD_DIM:] = jnp.ones((VT_ROWS - HEAD_DIM, group * t), jnp.float32)
    else:
        m_ref[...] = jnp.full(m_ref.shape, M_INIT, jnp.float32)
        acc_ref[HEAD_DIM:] = jnp.zeros((VT_ROWS - HEAD_DIM, group * t), jnp.float32)

    def block(o):
        kb = qb + (o - reach)
        valid = jnp.logical_and(kb >= 0, kb < nkv)
        return jnp.clip(kb, 0, nkv - 1), jnp.where(valid, o, n_off)

    def scores(o):
        kb, bidx = block(o)
        _scores(k_ref, kb * t, t, qt_ref, s_ref, mx_ref, o % 2, bias_ref[0, bidx])

    scores(0)
    for o in range(n_off):
        if o + 1 < n_off:
            scores(o + 1)
        _accumulate(vt_ref[0, block(o)[0]], s_ref, mx_ref, m_ref, acc_ref, o % 2)
    _store_gated(acc_ref, g_refs, o_ref, t)


def _dense_kernel(*refs, group, tq, tk, nkv):
    it = iter(refs)
    q_refs = [next(it) for _ in range(group)]
    g_refs = [next(it) for _ in range(group)]
    k_ref, vt_ref, cos_ref, sin_ref, qg_ref, o_ref = (next(it) for _ in range(6))
    qt_ref, s_ref, mx_ref, m_ref, acc_ref = (next(it) for _ in range(5))

    _load_q(q_refs, qt_ref, tq, lambda q: _rope(_rms(q) * qg_ref[...], cos_ref[...], sin_ref[...]))
    m_ref[...] = jnp.full(m_ref.shape, M_INIT, jnp.float32)
    acc_ref[...] = jnp.zeros(acc_ref.shape, jnp.float32)

    def scores(jblk, slot):
        _scores(k_ref, jblk * tk, tk, qt_ref, s_ref, mx_ref, slot)

    def accumulate(jblk, slot):
        _accumulate(vt_ref[0, jblk], s_ref, mx_ref, m_ref, acc_ref, slot)

    scores(0, 0)

    def pair(jj, carry):
        j0 = 2 * jj
        scores(j0 + 1, 1)
        accumulate(j0, 0)
        scores(jnp.minimum(j0 + 2, nkv - 1), 0)
        accumulate(j0 + 1, 1)
        return carry

    lax.fori_loop(0, nkv // 2, pair, 0)
    _store_gated(acc_ref, g_refs, o_ref, tq)


def _attention(kern, proj, k_src, k_col, vt, q_col, g_col, n_kv_heads, group, tq, tk, extra_specs,
               extra_args, name):
    s = proj.shape[0]
    assert s % tq == 0 and s % tk == 0
    in_specs, args = [], []
    for col in (q_col, g_col):
        for g in range(group):
            in_specs.append(pl.BlockSpec((tq, HEAD_DIM),
                                         lambda h, qb, col=col, g=g: (qb, col + h * group + g)))
            args.append(proj)
    in_specs += [pl.BlockSpec((s, HEAD_DIM), lambda h, qb: (0, k_col + h)),
                 pl.BlockSpec((1, s // tk, VT_ROWS, tk), lambda h, qb: (h, 0, 0, 0))] + extra_specs
    args += [k_src, vt] + extra_args
    n = group * tq
    return pl.pallas_call(
        kern,
        grid=(n_kv_heads, s // tq),
        in_specs=in_specs,
        out_specs=pl.BlockSpec((tq, group * HEAD_DIM), lambda h, qb: (qb, h)),
        out_shape=jax.ShapeDtypeStruct((s, n_kv_heads * group * HEAD_DIM), jnp.bfloat16),
        scratch_shapes=[pltpu.VMEM((HEAD_DIM, n), jnp.bfloat16),
                        pltpu.VMEM((2, tk, n), jnp.float32),
                        pltpu.VMEM((2, 1, n), jnp.float32),
                        pltpu.VMEM((1, n), jnp.float32),
                        pltpu.VMEM((VT_ROWS, n), jnp.float32)],
        compiler_params=_cparams(2),
        name=name,
    )(*args)


def _band_attention(proj, k_col, vt, q_col, g_col, n_kv_heads, group, t, reach, bias, sink=None):
    nkv = proj.shape[0] // t
    extra_specs = [pl.BlockSpec((1,) + bias.shape[1:], lambda h, qb: (h, 0, 0, 0))]
    extra_args = [bias]
    if sink is not None:
        extra_specs.append(pl.BlockSpec(memory_space=pltpu.SMEM))
        extra_args.append(sink)
    kern = functools.partial(_band_kernel, group=group, t=t, nkv=nkv, reach=reach,
                             has_sink=sink is not None)
    return _attention(kern, proj, proj, k_col, vt, q_col, g_col, n_kv_heads, group, t, t,
                      extra_specs, extra_args, "band_attention_g%d" % group)


def _dense_attention(proj, kc, vt, q_col, g_col, n_kv_heads, group, tq, tk, cos, sin, q_gain):
    nkv = proj.shape[0] // tk
    assert nkv % 2 == 0
    extra_specs = [pl.BlockSpec((tq, HEAD_DIM), lambda h, qb: (qb, 0)),
                   pl.BlockSpec((tq, HEAD_DIM), lambda h, qb: (qb, 0)),
                   pl.BlockSpec((1, HEAD_DIM), lambda h, qb: (0, 0))]
    kern = functools.partial(_dense_kernel, group=group, tq=tq, tk=tk, nkv=nkv)
    return _attention(kern, proj, kc, 0, vt, q_col, g_col, n_kv_heads, group, tq, tk,
                      extra_specs, [cos, sin, q_gain], "dense_attention_g%d" % group)


def _t5_bucket_np(rel):
    half_b = NUM_BUCKETS // 2
    max_exact = half_b // 2
    ret = np.where(rel > 0, half_b, 0)
    n = np.abs(rel)
    nf = np.maximum(n, 1).astype(np.float32)
    large = max_exact + (np.log(nf / np.float32(max_exact)) / np.float32(math.log(MAX_DISTANCE / max_exact))
                         * np.float32(half_b - max_exact)).astype(np.int32)
    large = np.minimum(large, half_b - 1)
    return (ret + np.where(n < max_exact, n, large)).astype(np.int32)


def _toeplitz_tiles(vec, tile, reach, group=1):
    heads = vec.shape[0]
    r = (vec.shape[1] - 1) // 2
    n_off = 2 * reach + 1
    span = 2 * tile - 1
    lo = -reach * tile - (tile - 1)
    hi = reach * tile + (tile - 1)
    padded = jnp.pad(vec, ((0, 0), (-lo - r, hi - r)), constant_values=MASKED)
    rows = [padded[:, o * tile:o * tile + span] for o in range(n_off)]
    u = jnp.stack(rows, axis=1)[:, :, ::-1]
    w = jnp.pad(u, ((0, 0), (0, 0), (0, 1)))
    flat = jnp.broadcast_to(w[:, :, None, :], (heads, n_off, tile, span + 1))
    flat = flat.reshape(heads, n_off, tile * (span + 1))[:, :, :tile * span]
    skew = flat.reshape(heads, n_off, tile, span)
    tiles = skew[:, :, :, tile - 1:]
    if group > 1:
        tiles = tiles.reshape(heads // group, group, n_off, tile, tile)
        tiles = jnp.moveaxis(tiles, 1, 3).reshape(heads // group, n_off, tile, group * tile)
    masked = jnp.full((tiles.shape[0], 1) + tiles.shape[2:], MASKED, tiles.dtype)
    return jnp.concatenate([tiles, masked], axis=1)


def _bias_tiles(rel_bias):
    hi = lax.Precision.HIGHEST
    d = np.arange(-A_REACH, A_REACH + 1)
    count = np.zeros(d.shape, np.float64)
    for w, dil in A_PATTERNS:
        count += (d % dil == 0) & (np.abs(d) <= w // 2)
    log_count = np.where(count > 0, np.log(np.maximum(count, 1.0)), 0.0).astype(np.float32)
    table = rel_bias.astype(jnp.float32)
    onehot_a = jnp.asarray(np.eye(NUM_BUCKETS, dtype=np.float32)[_t5_bucket_np(d)])
    vec_a = (jnp.dot(onehot_a, table[:, :A_HEADS], precision=hi).T + log_count[None]) * LOG2E
    vec_a = jnp.where(jnp.asarray(count > 0)[None], vec_a, MASKED)
    tiles_a = _toeplitz_tiles(vec_a, A_TILE, -(-A_REACH // A_TILE))
    db = np.arange(-B_HALF_WINDOW, B_HALF_WINDOW + 1)
    onehot_b = jnp.asarray(np.eye(NUM_BUCKETS, dtype=np.float32)[_t5_bucket_np(db)])
    vec_b = jnp.dot(onehot_b, table[:, A_HEADS:], precision=hi).T * LOG2E
    tiles_b = _toeplitz_tiles(vec_b, B_TILE, -(-B_HALF_WINDOW // B_TILE), B_HEADS // B_KV_HEADS)
    return tiles_a, tiles_b


def kernel(x, c, w_mod, b_mod, pre_norm, post_norm, w_in, w_out, q_norm, k_norm, sink, rel_bias):
    bsz, s, d = x.shape
    assert bsz == 1 and c.shape[0] == 1
    depth = w_in.shape[0]
    assert w_in.shape[2] == IN_WIDTH and w_out.shape[1] == A_W + B_W + C_W
    bf = jnp.bfloat16
    x2 = x[0]
    w_in_b = w_in.astype(bf)
    w_out_b = w_out.astype(bf)
    cos, sin = _rope_tables(s)
    tiles_a, tiles_b = _bias_tiles(rel_bias)
    mod = _modulation(c, w_mod, b_mod)[:, 0]
    shift, scale, gate = mod[:, :d], mod[:, d:2 * d], mod[:, 2 * d:]
    vec = lambda a, i: a[i][None, :]

    h = _prenorm(x2, vec(pre_norm, 0), vec(scale, 0), vec(shift, 0))
    for i in range(depth):
        proj = _in_projection(h, w_in_b[i])
        vta, vtb, vtc, kc = _prep(proj, cos, sin, vec(k_norm, i))
        oa = _band_attention(proj, AK, vta, AQ, AG, A_HEADS, 1, A_TILE, -(-A_REACH // A_TILE), tiles_a)
        ob = _band_attention(proj, BK, vtb, BQ, BG, B_KV_HEADS, B_HEADS // B_KV_HEADS, B_TILE,
                             -(-B_HALF_WINDOW // B_TILE), tiles_b, sink[i])
        oc = _dense_attention(proj, kc, vtc, CQ, CG, C_KV_HEADS, C_HEADS // C_KV_HEADS, C_TQ, C_TK,
                              cos, sin, vec(q_norm, i))
        y = _out_projection(oa, ob, oc, w_out_b[i, :A_W], w_out_b[i, A_W:A_W + B_W],
                            w_out_b[i, A_W + B_W:])
        nxt = None if i == depth - 1 else (vec(pre_norm, i + 1), vec(scale, i + 1), vec(shift, i + 1))
        x2, h = _post(x2, y, vec(post_norm, i), vec(gate, i), nxt)
    return x2[None]
```

```python
import functools
import math

import numpy as np
import jax
import jax.numpy as jnp
from jax import lax
from jax.experimental import pallas as pl
from jax.experimental.pallas import tpu as pltpu

HEAD_DIM = 128
GRID_W = 64
A_HEADS = 12
A_PATTERNS = ((128, 1), (512, 4), (2048, 16))
B_HEADS = 8
B_KV_HEADS = 2
B_HALF_WINDOW = 128
C_HEADS = 12
C_KV_HEADS = 4
ROPE_THETA = 10000.0
NUM_BUCKETS = 32
MAX_DISTANCE = 1024
EPS = 1e-6
LOG2E = math.log2(math.e)
MASKED = -1e30
M_INIT = -1e29

A_W = A_HEADS * HEAD_DIM
B_W = B_HEADS * HEAD_DIM
C_W = C_HEADS * HEAD_DIM
AQ, AK, AV, AG = 0, 12, 24, 36
BQ, BK, BV, BG = 48, 56, 58, 60
CQ, CK, CV, CG = 68, 80, 84, 88
IN_WIDTH = 100 * HEAD_DIM

A_REACH = max(w // 2 for w, _ in A_PATTERNS)
A_TILE = 512
B_TILE = 256
C_TQ = 256
C_TK = 1024
DENSE_UNROLL = 3
VT_ROWS = HEAD_DIM + 16
V7X_VMEM_LIMIT = 56 * 1024 * 1024
SUB = 128


def _cparams(n_axes):
    return pltpu.CompilerParams(dimension_semantics=("arbitrary",) * n_axes,
                                vmem_limit_bytes=V7X_VMEM_LIMIT)


def _tile(n, pref):
    t = min(n, pref)
    while n % t:
        t //= 2
    return t


def _mod_kernel(c_ref, w_ref, b_ref, o_ref):
    c = c_ref[...]
    cs = jnp.broadcast_to(c * jax.nn.sigmoid(c), (8, c.shape[1]))
    r = jnp.dot(cs, w_ref[0], preferred_element_type=jnp.float32)
    o_ref[0] = r[0:1] + b_ref[0]


def _modulation(c, w_mod, b_mod):
    depth, d, n = w_mod.shape
    tn = _tile(n, 1024)
    out = pl.pallas_call(
        _mod_kernel,
        grid=(depth, n // tn),
        in_specs=[pl.BlockSpec((1, d), lambda i, j: (0, 0)),
                  pl.BlockSpec((1, d, tn), lambda i, j: (i, 0, j)),
                  pl.BlockSpec((1, 1, tn), lambda i, j: (i, 0, j))],
        out_specs=pl.BlockSpec((1, 1, tn), lambda i, j: (i, 0, j)),
        out_shape=jax.ShapeDtypeStruct((depth, 1, n), jnp.float32),
        compiler_params=_cparams(2),
        name="modulation",
    )(c, w_mod, b_mod.reshape(depth, 1, n))
    return out


def _rms(x):
    return x * lax.rsqrt(jnp.mean(x * x, axis=-1, keepdims=True) + EPS)


def _prenorm_kernel(x_ref, g_ref, sc_ref, sh_ref, h_ref):
    y = _rms(x_ref[...]) * g_ref[...]
    h_ref[...] = (y * (1.0 + sc_ref[...]) + sh_ref[...]).astype(h_ref.dtype)


def _prenorm(x, g, scale, shift):
    s, d = x.shape
    tm = _tile(s, 512)
    vec = pl.BlockSpec((1, d), lambda i: (0, 0))
    return pl.pallas_call(
        _prenorm_kernel,
        grid=(s // tm,),
        in_specs=[pl.BlockSpec((tm, d), lambda i: (i, 0)), vec, vec, vec],
        out_specs=pl.BlockSpec((tm, d), lambda i: (i, 0)),
        out_shape=jax.ShapeDtypeStruct((s, d), jnp.bfloat16),
        compiler_params=_cparams(1),
        name="prenorm",
    )(x, g, scale, shift)


def _post_kernel(x_ref, y_ref, pg_ref, gate_ref, xo_ref):
    yn = _rms(y_ref[...].astype(jnp.float32)) * pg_ref[...]
    xo_ref[...] = x_ref[...] + gate_ref[...] * yn


def _postpre_kernel(x_ref, y_ref, pg_ref, gate_ref, g_ref, sc_ref, sh_ref, xo_ref, h_ref):
    yn = _rms(y_ref[...].astype(jnp.float32)) * pg_ref[...]
    xn = x_ref[...] + gate_ref[...] * yn
    xo_ref[...] = xn
    h = _rms(xn) * g_ref[...]
    h_ref[...] = (h * (1.0 + sc_ref[...]) + sh_ref[...]).astype(h_ref.dtype)


def _post(x, y, post_g, gate, nxt=None):
    s, d = x.shape
    tm = _tile(s, 256)
    row = pl.BlockSpec((tm, d), lambda i: (i, 0))
    vec = pl.BlockSpec((1, d), lambda i: (0, 0))
    if nxt is None:
        return pl.pallas_call(
            _post_kernel, grid=(s // tm,),
            in_specs=[row, row, vec, vec], out_specs=row,
            out_shape=jax.ShapeDtypeStruct((s, d), jnp.float32),
            compiler_params=_cparams(1), name="postnorm",
        )(x, y, post_g, gate), None
    g, scale, shift = nxt
    return pl.pallas_call(
        _postpre_kernel, grid=(s // tm,),
        in_specs=[row, row, vec, vec, vec, vec, vec], out_specs=[row, row],
        out_shape=[jax.ShapeDtypeStruct((s, d), jnp.float32),
                   jax.ShapeDtypeStruct((s, d), jnp.bfloat16)],
        compiler_params=_cparams(1), name="postnorm_prenorm",
    )(x, y, post_g, gate, g, scale, shift)


def _mm_kernel(a_ref, b_ref, o_ref):
    o_ref[...] = jnp.dot(a_ref[...], b_ref[0].astype(a_ref.dtype),
                         preferred_element_type=jnp.float32).astype(o_ref.dtype)


def _in_projection(h, w, layer):
    m, k = h.shape
    n = w.shape[2]
    tm, tn = _tile(m, 1024), _tile(n, 512)
    return pl.pallas_call(
        _mm_kernel,
        grid=(m // tm, n // tn),
        in_specs=[pl.BlockSpec((tm, k), lambda i, j: (i, 0)),
                  pl.BlockSpec((1, k, tn), lambda i, j: (layer, 0, j))],
        out_specs=pl.BlockSpec((tm, tn), lambda i, j: (i, j)),
        out_shape=jax.ShapeDtypeStruct((m, n), jnp.bfloat16),
        compiler_params=_cparams(2),
        name="in_projection",
    )(h, w)


def _mm3_kernel(a1, a2, a3, w_ref, o_ref):
    acc, row = None, 0
    for a in (a1, a2, a3):
        w = w_ref[0, row:row + a.shape[1]].astype(a.dtype)
        part = jnp.dot(a[...], w, preferred_element_type=jnp.float32)
        acc = part if acc is None else acc + part
        row += a.shape[1]
    o_ref[...] = acc.astype(o_ref.dtype)


def _out_projection(oa, ob, oc, w, layer):
    m = oa.shape[0]
    k, n = w.shape[1:]
    tm, tn = _tile(m, 1024), _tile(n, 512)
    a_spec = lambda a: pl.BlockSpec((tm, a.shape[1]), lambda i, j: (i, 0))
    return pl.pallas_call(
        _mm3_kernel,
        grid=(m // tm, n // tn),
        in_specs=[a_spec(oa), a_spec(ob), a_spec(oc),
                  pl.BlockSpec((1, k, tn), lambda i, j: (layer, 0, j))],
        out_specs=pl.BlockSpec((tm, tn), lambda i, j: (i, j)),
        out_shape=jax.ShapeDtypeStruct((m, n), jnp.float32),
        compiler_params=_cparams(2),
        name="out_projection",
    )(oa, ob, oc, w)


def _rope_tables(s):
    rows = s // GRID_W
    row = jnp.repeat(jnp.arange(rows), GRID_W)
    col = jnp.tile(jnp.arange(GRID_W), rows)
    quarter = HEAD_DIM // 4
    freqs = ROPE_THETA ** (-jnp.arange(0, 2 * quarter, 2, dtype=jnp.float32) / (2 * quarter))
    ar = row.astype(jnp.float32)[:, None] * freqs[None, :]
    ac = col.astype(jnp.float32)[:, None] * freqs[None, :]
    cr, sr, cc, sc = jnp.cos(ar), jnp.sin(ar), jnp.cos(ac), jnp.sin(ac)
    cos = jnp.concatenate([cr, cr, cc, cc], axis=-1)
    sin = jnp.concatenate([-sr, sr, -sc, sc], axis=-1)
    return cos, sin


def _rope(x, cos, sin):
    quarter = HEAD_DIM // 4
    fwd = pltpu.roll(x, HEAD_DIM - quarter, 1)
    bwd = pltpu.roll(x, quarter, 1)
    lane = lax.broadcasted_iota(jnp.int32, x.shape, 1)
    partner = jnp.where((lane % (2 * quarter)) < quarter, fwd, bwd)
    return x * cos + partner * sin


def _prep_kernel(av_ref, bv_ref, cv_ref, ck_ref, cos_ref, sin_ref, kg_ref,
                 vta_ref, vtb_ref, vtc_ref, kc_ref, *, ts):
    def put_t(src_ref, n_heads, dst_ref, chunk):
        for hh in range(n_heads):
            vt = src_ref[:, hh * HEAD_DIM:(hh + 1) * HEAD_DIM].astype(jnp.float32).T
            for cidx in range(ts // chunk):
                dst_ref[hh, cidx, :HEAD_DIM] = vt[:, cidx * chunk:(cidx + 1) * chunk].astype(dst_ref.dtype)
                dst_ref[hh, cidx, HEAD_DIM:] = jnp.ones((VT_ROWS - HEAD_DIM, chunk), dst_ref.dtype)

    put_t(av_ref, A_HEADS, vta_ref, A_TILE)
    put_t(bv_ref, B_KV_HEADS, vtb_ref, B_TILE)
    put_t(cv_ref, C_KV_HEADS, vtc_ref, C_TK)
    cos, sin, kg = cos_ref[...], sin_ref[...], kg_ref[...]
    for hh in range(C_KV_HEADS):
        k = ck_ref[:, hh * HEAD_DIM:(hh + 1) * HEAD_DIM].astype(jnp.float32)
        k = _rope(_rms(k) * kg, cos, sin)
        kc_ref[:, hh * HEAD_DIM:(hh + 1) * HEAD_DIM] = k.astype(kc_ref.dtype)


def _prep(proj, cos, sin, k_gain):
    s = proj.shape[0]
    ts = max(A_TILE, B_TILE, C_TK)
    assert s % ts == 0
    bf = jnp.bfloat16
    n = s // ts
    vt_spec = lambda heads, t: pl.BlockSpec((heads, ts // t, VT_ROWS, t), lambda i: (0, i, 0, 0))
    vt_shape = lambda heads, t: jax.ShapeDtypeStruct((heads, s // t, VT_ROWS, t), bf)
    return pl.pallas_call(
        functools.partial(_prep_kernel, ts=ts),
        grid=(n,),
        in_specs=[pl.BlockSpec((ts, A_W), lambda i: (i, AV * HEAD_DIM // A_W)),
                  pl.BlockSpec((ts, B_KV_HEADS * HEAD_DIM), lambda i: (i, BV // B_KV_HEADS)),
                  pl.BlockSpec((ts, C_KV_HEADS * HEAD_DIM), lambda i: (i, CV // C_KV_HEADS)),
                  pl.BlockSpec((ts, C_KV_HEADS * HEAD_DIM), lambda i: (i, CK // C_KV_HEADS)),
                  pl.BlockSpec((ts, HEAD_DIM), lambda i: (i, 0)),
                  pl.BlockSpec((ts, HEAD_DIM), lambda i: (i, 0)),
                  pl.BlockSpec((1, HEAD_DIM), lambda i: (0, 0))],
        out_specs=[vt_spec(A_HEADS, A_TILE), vt_spec(B_KV_HEADS, B_TILE), vt_spec(C_KV_HEADS, C_TK),
                   pl.BlockSpec((ts, C_KV_HEADS * HEAD_DIM), lambda i: (i, 0))],
        out_shape=[vt_shape(A_HEADS, A_TILE), vt_shape(B_KV_HEADS, B_TILE), vt_shape(C_KV_HEADS, C_TK),
                   jax.ShapeDtypeStruct((s, C_KV_HEADS * HEAD_DIM), bf)],
        compiler_params=_cparams(1),
        name="vt_krope_prep",
    )(proj, proj, proj, proj, cos, sin, k_gain)


def _scores(k_ref, row0, tk, qt_ref, s_ref, mx_ref, slot, bias=None):
    kt = k_ref[pl.ds(pl.multiple_of(row0, tk), tk), :]
    s = jnp.dot(kt, qt_ref[...], preferred_element_type=jnp.float32)
    if bias is not None:
        s = s + bias
    s_ref[slot] = s
    mx_ref[slot] = jnp.max(s, axis=0, keepdims=True)


def _accumulate(vt, s_ref, mx_ref, m_ref, acc_ref, slot):
    m_old = m_ref[...]
    m_new = jnp.maximum(m_old, mx_ref[slot])
    alpha = jnp.exp2(m_old - m_new)
    p = jnp.exp2(s_ref[slot] - m_new).astype(jnp.bfloat16)
    acc_ref[...] = alpha * acc_ref[...] + jnp.dot(vt, p, preferred_element_type=jnp.float32)
    m_ref[...] = m_new


def _load_q(q_refs, qt_ref, tq, prep=lambda q: q):
    scale2 = HEAD_DIM ** -0.5 * LOG2E
    for g, q_ref in enumerate(q_refs):
        q = prep(q_ref[...].astype(jnp.float32))
        qt_ref[:, g * tq:(g + 1) * tq] = (q * scale2).T.astype(qt_ref.dtype)


def _store_gated(acc_ref, g_refs, o_ref, tq):
    for g, g_ref in enumerate(g_refs):
        sl = slice(g * tq, (g + 1) * tq)
        o = (acc_ref[:HEAD_DIM, sl] / acc_ref[HEAD_DIM:HEAD_DIM + 1, sl]).T
        gate = g_ref[...].astype(jnp.float32)
        o_ref[:, g * HEAD_DIM:(g + 1) * HEAD_DIM] = (o * (gate * jax.nn.sigmoid(gate))).astype(o_ref.dtype)


def _band_kernel(*refs, group, t, nkv, reach, has_sink):
    it = iter(refs)
    q_refs = [next(it) for _ in range(group)]
    g_refs = [next(it) for _ in range(group)]
    k_ref, vt_ref, bias_ref = next(it), next(it), next(it)
    sink_ref = next(it) if has_sink else None
    o_ref = next(it)
    qt_ref, s_ref, mx_ref, m_ref, acc_ref = (next(it) for _ in range(5))
    h = pl.program_id(0)
    qb = pl.program_id(1)
    n_off = 2 * reach + 1

    _load_q(q_refs, qt_ref, t)
    acc_ref[:HEAD_DIM] = jnp.zeros((HEAD_DIM, group * t), jnp.float32)
    if has_sink:
        for g in range(group):
            m_ref[:, g * t:(g + 1) * t] = jnp.full((1, t), sink_ref[h * group + g] * LOG2E, jnp.float32)
        acc_ref[HEAD_DIM:] = jnp.ones((VT_ROWS - HEAD_DIM, group * t), jnp.float32)
    else:
        m_ref[...] = jnp.full(m_ref.shape, M_INIT, jnp.float32)
        acc_ref[HEAD_DIM:] = jnp.zeros((VT_ROWS - HEAD_DIM, group * t), jnp.float32)

    nsb = t // SUB
    n_sub = bias_ref.shape[1] - 1

    def block(o):
        kb = qb + (o - reach)
        return jnp.clip(kb, 0, nkv - 1), jnp.logical_and(kb >= 0, kb < nkv)

    def bias_tile(o, valid):
        rows = []
        for a in range(nsb):
            cols = []
            for g in range(group):
                for b in range(nsb):
                    sub = (o - reach) * nsb + a - b + (n_sub - 1) // 2
                    cols.append(bias_ref[g, jnp.where(valid, sub, n_sub)])
            rows.append(jnp.concatenate(cols, axis=1))
        return jnp.concatenate(rows, axis=0)

    def scores(o):
        kb, valid = block(o)
        _scores(k_ref, kb * t, t, qt_ref, s_ref, mx_ref, o % 2, bias_tile(o, valid))

    scores(0)
    for o in range(n_off):
        if o + 1 < n_off:
            scores(o + 1)
        _accumulate(vt_ref[0, block(o)[0]], s_ref, mx_ref, m_ref, acc_ref, o % 2)
    _store_gated(acc_ref, g_refs, o_ref, t)


def _dense_kernel(*refs, group, tq, tk, nkv):
    it = iter(refs)
    q_refs = [next(it) for _ in range(group)]
    g_refs = [next(it) for _ in range(group)]
    k_ref, vt_ref, cos_ref, sin_ref, qg_ref, o_ref = (next(it) for _ in range(6))
    qt_ref, s_ref, mx_ref, m_ref, acc_ref = (next(it) for _ in range(5))

    _load_q(q_refs, qt_ref, tq, lambda q: _rope(_rms(q) * qg_ref[...], cos_ref[...], sin_ref[...]))
    m_ref[...] = jnp.full(m_ref.shape, M_INIT, jnp.float32)
    acc_ref[...] = jnp.zeros(acc_ref.shape, jnp.float32)

    def scores(jblk, slot):
        _scores(k_ref, jblk * tk, tk, qt_ref, s_ref, mx_ref, slot)

    def accumulate(jblk, slot):
        _accumulate(vt_ref[0, jblk], s_ref, mx_ref, m_ref, acc_ref, slot)

    scores(0, 0)

    def pair(j0):
        scores(j0 + 1, 1)
        accumulate(j0, 0)
        scores(j0 + 2, 0)
        accumulate(j0 + 1, 1)

    def pairs(it, carry):
        for u in range(DENSE_UNROLL):
            pair(2 * (it * DENSE_UNROLL + u))
        return carry

    n_pairs = nkv // 2 - 1
    lax.fori_loop(0, n_pairs // DENSE_UNROLL, pairs, 0)
    for jj in range(n_pairs - n_pairs % DENSE_UNROLL, n_pairs):
        pair(2 * jj)
    scores(nkv - 1, 1)
    accumulate(nkv - 2, 0)
    accumulate(nkv - 1, 1)
    _store_gated(acc_ref, g_refs, o_ref, tq)


def _attention(kern, proj, k_src, k_col, vt, q_col, g_col, n_kv_heads, group, tq, tk, extra_specs,
               extra_args, name):
    s = proj.shape[0]
    assert s % tq == 0 and s % tk == 0
    in_specs, args = [], []
    for col in (q_col, g_col):
        for g in range(group):
            in_specs.append(pl.BlockSpec((tq, HEAD_DIM),
                                         lambda h, qb, col=col, g=g: (qb, col + h * group + g)))
            args.append(proj)
    in_specs += [pl.BlockSpec((s, HEAD_DIM), lambda h, qb: (0, k_col + h)),
                 pl.BlockSpec((1, s // tk, VT_ROWS, tk), lambda h, qb: (h, 0, 0, 0))] + extra_specs
    args += [k_src, vt] + extra_args
    n = group * tq
    return pl.pallas_call(
        kern,
        grid=(n_kv_heads, s // tq),
        in_specs=in_specs,
        out_specs=pl.BlockSpec((tq, group * HEAD_DIM), lambda h, qb: (qb, h)),
        out_shape=jax.ShapeDtypeStruct((s, n_kv_heads * group * HEAD_DIM), jnp.bfloat16),
        scratch_shapes=[pltpu.VMEM((HEAD_DIM, n), jnp.bfloat16),
                        pltpu.VMEM((2, tk, n), jnp.float32),
                        pltpu.VMEM((2, 1, n), jnp.float32),
                        pltpu.VMEM((1, n), jnp.float32),
                        pltpu.VMEM((VT_ROWS, n), jnp.float32)],
        compiler_params=_cparams(2),
        name=name,
    )(*args)


def _band_attention(proj, k_col, vt, q_col, g_col, n_kv_heads, group, t, reach, bias, sink=None):
    nkv = proj.shape[0] // t
    extra_specs = [pl.BlockSpec((group,) + bias.shape[1:], lambda h, qb: (h, 0, 0, 0))]
    extra_args = [bias]
    if sink is not None:
        extra_specs.append(pl.BlockSpec(memory_space=pltpu.SMEM))
        extra_args.append(sink)
    kern = functools.partial(_band_kernel, group=group, t=t, nkv=nkv, reach=reach,
                             has_sink=sink is not None)
    return _attention(kern, proj, proj, k_col, vt, q_col, g_col, n_kv_heads, group, t, t,
                      extra_specs, extra_args, "band_attention_g%d" % group)


def _dense_attention(proj, kc, vt, q_col, g_col, n_kv_heads, group, tq, tk, cos, sin, q_gain):
    nkv = proj.shape[0] // tk
    assert nkv % 2 == 0
    extra_specs = [pl.BlockSpec((tq, HEAD_DIM), lambda h, qb: (qb, 0)),
                   pl.BlockSpec((tq, HEAD_DIM), lambda h, qb: (qb, 0)),
                   pl.BlockSpec((1, HEAD_DIM), lambda h, qb: (0, 0))]
    kern = functools.partial(_dense_kernel, group=group, tq=tq, tk=tk, nkv=nkv)
    return _attention(kern, proj, kc, 0, vt, q_col, g_col, n_kv_heads, group, tq, tk,
                      extra_specs, [cos, sin, q_gain], "dense_attention_g%d" % group)


def _t5_bucket_np(rel):
    half_b = NUM_BUCKETS // 2
    max_exact = half_b // 2
    ret = np.where(rel > 0, half_b, 0)
    n = np.abs(rel)
    nf = np.maximum(n, 1).astype(np.float32)
    large = max_exact + (np.log(nf / np.float32(max_exact)) / np.float32(math.log(MAX_DISTANCE / max_exact))
                         * np.float32(half_b - max_exact)).astype(np.int32)
    large = np.minimum(large, half_b - 1)
    return (ret + np.where(n < max_exact, n, large)).astype(np.int32)


def _toeplitz_tiles(vec, tile, reach):
    heads = vec.shape[0]
    r = (vec.shape[1] - 1) // 2
    n_off = 2 * reach + 1
    span = 2 * tile - 1
    lo = -reach * tile - (tile - 1)
    hi = reach * tile + (tile - 1)
    padded = jnp.pad(vec, ((0, 0), (-lo - r, hi - r)), constant_values=MASKED)
    rows = [padded[:, o * tile:o * tile + span] for o in range(n_off)]
    u = jnp.stack(rows, axis=1)[:, :, ::-1]
    w = jnp.pad(u, ((0, 0), (0, 0), (0, 1)))
    flat = jnp.broadcast_to(w[:, :, None, :], (heads, n_off, tile, span + 1))
    flat = flat.reshape(heads, n_off, tile * (span + 1))[:, :, :tile * span]
    skew = flat.reshape(heads, n_off, tile, span)
    tiles = skew[:, :, :, tile - 1:]
    masked = jnp.full((tiles.shape[0], 1) + tiles.shape[2:], MASKED, tiles.dtype)
    return jnp.concatenate([tiles, masked], axis=1)


def _bias_tiles(rel_bias):
    hi = lax.Precision.HIGHEST
    d = np.arange(-A_REACH, A_REACH + 1)
    count = np.zeros(d.shape, np.float64)
    for w, dil in A_PATTERNS:
        count += (d % dil == 0) & (np.abs(d) <= w // 2)
    log_count = np.where(count > 0, np.log(np.maximum(count, 1.0)), 0.0).astype(np.float32)
    table = rel_bias.astype(jnp.float32)
    onehot_a = jnp.asarray(np.eye(NUM_BUCKETS, dtype=np.float32)[_t5_bucket_np(d)])
    vec_a = (jnp.dot(onehot_a, table[:, :A_HEADS], precision=hi).T + log_count[None]) * LOG2E
    vec_a = jnp.where(jnp.asarray(count > 0)[None], vec_a, MASKED)
    tiles_a = _toeplitz_tiles(vec_a, SUB, (-(-A_REACH // A_TILE) + 1) * (A_TILE // SUB) - 1)
    db = np.arange(-B_HALF_WINDOW, B_HALF_WINDOW + 1)
    onehot_b = jnp.asarray(np.eye(NUM_BUCKETS, dtype=np.float32)[_t5_bucket_np(db)])
    vec_b = jnp.dot(onehot_b, table[:, A_HEADS:], precision=hi).T * LOG2E
    tiles_b = _toeplitz_tiles(vec_b, SUB, (-(-B_HALF_WINDOW // B_TILE) + 1) * (B_TILE // SUB) - 1)
    return tiles_a, tiles_b


def kernel(x, c, w_mod, b_mod, pre_norm, post_norm, w_in, w_out, q_norm, k_norm, sink, rel_bias):
    bsz, s, d = x.shape
    assert bsz == 1 and c.shape[0] == 1
    depth = w_in.shape[0]
    assert w_in.shape[2] == IN_WIDTH and w_out.shape[1] == A_W + B_W + C_W
    x2 = x[0]
    cos, sin = _rope_tables(s)
    tiles_a, tiles_b = _bias_tiles(rel_bias)
    mod = _modulation(c, w_mod, b_mod)[:, 0]
    shift, scale, gate = mod[:, :d], mod[:, d:2 * d], mod[:, 2 * d:]
    vec = lambda a, i: a[i][None, :]

    h = _prenorm(x2, vec(pre_norm, 0), vec(scale, 0), vec(shift, 0))
    for i in range(depth):
        proj = _in_projection(h, w_in, i)
        vta, vtb, vtc, kc = _prep(proj, cos, sin, vec(k_norm, i))
        oa = _band_attention(proj, AK, vta, AQ, AG, A_HEADS, 1, A_TILE, -(-A_REACH // A_TILE), tiles_a)
        ob = _band_attention(proj, BK, vtb, BQ, BG, B_KV_HEADS, B_HEADS // B_KV_HEADS, B_TILE,
                             -(-B_HALF_WINDOW // B_TILE), tiles_b, sink[i])
        oc = _dense_attention(proj, kc, vtc, CQ, CG, C_KV_HEADS, C_HEADS // C_KV_HEADS, C_TQ, C_TK,
                              cos, sin, vec(q_norm, i))
        y = _out_projection(oa, ob, oc, w_out, i)
        nxt = None if i == depth - 1 else (vec(pre_norm, i + 1), vec(scale, i + 1), vec(shift, i + 1))
        x2, h = _post(x2, y, vec(post_norm, i), vec(gate, i), nxt)
    return x2[None]
```

```python
import functools
import math

import numpy as np
import jax
import jax.numpy as jnp
from jax import lax
from jax.experimental import pallas as pl
from jax.experimental.pallas import tpu as pltpu

HEAD_DIM = 128
GRID_W = 64
A_HEADS = 12
A_PATTERNS = ((128, 1), (512, 4), (2048, 16))
B_HEADS = 8
B_KV_HEADS = 2
B_HALF_WINDOW = 128
C_HEADS = 12
C_KV_HEADS = 4
ROPE_THETA = 10000.0
NUM_BUCKETS = 32
MAX_DISTANCE = 1024
EPS = 1e-6
LOG2E = math.log2(math.e)
MASKED = -1e30
M_INIT = -1e29

A_W = A_HEADS * HEAD_DIM
B_W = B_HEADS * HEAD_DIM
C_W = C_HEADS * HEAD_DIM
AQ, AK, AV, AG = 0, 12, 24, 36
BQ, BK, BV, BG = 48, 56, 58, 60
CQ, CK, CV, CG = 68, 80, 84, 88
IN_WIDTH = 100 * HEAD_DIM

A_REACH = max(w // 2 for w, _ in A_PATTERNS)
A_TILE = 512
B_TILE = 256
C_TQ = 256
C_TK = 1024
BAND_HEADS_PER_STEP = 2
DENSE_UNROLL = 3
VT_ROWS = HEAD_DIM + 16
V7X_VMEM_LIMIT = 56 * 1024 * 1024
SUB = 128


def _cparams(n_axes):
    return pltpu.CompilerParams(dimension_semantics=("arbitrary",) * n_axes,
                                vmem_limit_bytes=V7X_VMEM_LIMIT)


def _tile(n, pref):
    t = min(n, pref)
    while n % t:
        t //= 2
    return t


def _mod_kernel(c_ref, w_ref, b_ref, o_ref):
    c = c_ref[...]
    cs = jnp.broadcast_to(c * jax.nn.sigmoid(c), (8, c.shape[1]))
    r = jnp.dot(cs, w_ref[0], preferred_element_type=jnp.float32)
    o_ref[0] = r[0:1] + b_ref[0]


def _modulation(c, w_mod, b_mod):
    depth, d, n = w_mod.shape
    tn = _tile(n, 1024)
    out = pl.pallas_call(
        _mod_kernel,
        grid=(depth, n // tn),
        in_specs=[pl.BlockSpec((1, d), lambda i, j: (0, 0)),
                  pl.BlockSpec((1, d, tn), lambda i, j: (i, 0, j)),
                  pl.BlockSpec((1, 1, tn), lambda i, j: (i, 0, j))],
        out_specs=pl.BlockSpec((1, 1, tn), lambda i, j: (i, 0, j)),
        out_shape=jax.ShapeDtypeStruct((depth, 1, n), jnp.float32),
        compiler_params=_cparams(2),
        name="modulation",
    )(c, w_mod, b_mod.reshape(depth, 1, n))
    return out


def _rms(x):
    return x * lax.rsqrt(jnp.mean(x * x, axis=-1, keepdims=True) + EPS)


def _prenorm_kernel(x_ref, g_ref, sc_ref, sh_ref, h_ref):
    y = _rms(x_ref[...]) * g_ref[...]
    h_ref[...] = (y * (1.0 + sc_ref[...]) + sh_ref[...]).astype(h_ref.dtype)


def _prenorm(x, g, scale, shift):
    s, d = x.shape
    tm = _tile(s, 512)
    vec = pl.BlockSpec((1, d), lambda i: (0, 0))
    return pl.pallas_call(
        _prenorm_kernel,
        grid=(s // tm,),
        in_specs=[pl.BlockSpec((tm, d), lambda i: (i, 0)), vec, vec, vec],
        out_specs=pl.BlockSpec((tm, d), lambda i: (i, 0)),
        out_shape=jax.ShapeDtypeStruct((s, d), jnp.bfloat16),
        compiler_params=_cparams(1),
        name="prenorm",
    )(x, g, scale, shift)


def _post_kernel(x_ref, y_ref, pg_ref, gate_ref, xo_ref):
    yn = _rms(y_ref[...].astype(jnp.float32)) * pg_ref[...]
    xo_ref[...] = x_ref[...] + gate_ref[...] * yn


def _postpre_kernel(x_ref, y_ref, pg_ref, gate_ref, g_ref, sc_ref, sh_ref, xo_ref, h_ref):
    yn = _rms(y_ref[...].astype(jnp.float32)) * pg_ref[...]
    xn = x_ref[...] + gate_ref[...] * yn
    xo_ref[...] = xn
    h = _rms(xn) * g_ref[...]
    h_ref[...] = (h * (1.0 + sc_ref[...]) + sh_ref[...]).astype(h_ref.dtype)


def _post(x, y, post_g, gate, nxt=None):
    s, d = x.shape
    tm = _tile(s, 256)
    row = pl.BlockSpec((tm, d), lambda i: (i, 0))
    vec = pl.BlockSpec((1, d), lambda i: (0, 0))
    if nxt is None:
        return pl.pallas_call(
            _post_kernel, grid=(s // tm,),
            in_specs=[row, row, vec, vec], out_specs=row,
            out_shape=jax.ShapeDtypeStruct((s, d), jnp.float32),
            compiler_params=_cparams(1), name="postnorm",
        )(x, y, post_g, gate), None
    g, scale, shift = nxt
    return pl.pallas_call(
        _postpre_kernel, grid=(s // tm,),
        in_specs=[row, row, vec, vec, vec, vec, vec], out_specs=[row, row],
        out_shape=[jax.ShapeDtypeStruct((s, d), jnp.float32),
                   jax.ShapeDtypeStruct((s, d), jnp.bfloat16)],
        compiler_params=_cparams(1), name="postnorm_prenorm",
    )(x, y, post_g, gate, g, scale, shift)


def _mm_kernel(a_ref, b_ref, o_ref):
    o_ref[...] = jnp.dot(a_ref[...], b_ref[0].astype(a_ref.dtype),
                         preferred_element_type=jnp.float32).astype(o_ref.dtype)


def _in_projection(h, w, layer):
    m, k = h.shape
    n = w.shape[2]
    tm, tn = _tile(m, 1024), _tile(n, 512)
    return pl.pallas_call(
        _mm_kernel,
        grid=(m // tm, n // tn),
        in_specs=[pl.BlockSpec((tm, k), lambda i, j: (i, 0)),
                  pl.BlockSpec((1, k, tn), lambda i, j: (layer, 0, j))],
        out_specs=pl.BlockSpec((tm, tn), lambda i, j: (i, j)),
        out_shape=jax.ShapeDtypeStruct((m, n), jnp.bfloat16),
        compiler_params=_cparams(2),
        name="in_projection",
    )(h, w)


def _mm3_kernel(a1, a2, a3, w_ref, o_ref):
    acc, row = None, 0
    for a in (a1, a2, a3):
        w = w_ref[0, row:row + a.shape[1]].astype(a.dtype)
        part = jnp.dot(a[...], w, preferred_element_type=jnp.float32)
        acc = part if acc is None else acc + part
        row += a.shape[1]
    o_ref[...] = acc.astype(o_ref.dtype)


def _out_projection(oa, ob, oc, w, layer):
    m = oa.shape[0]
    k, n = w.shape[1:]
    tm, tn = _tile(m, 1024), _tile(n, 512)
    a_spec = lambda a: pl.BlockSpec((tm, a.shape[1]), lambda i, j: (i, 0))
    return pl.pallas_call(
        _mm3_kernel,
        grid=(m // tm, n // tn),
        in_specs=[a_spec(oa), a_spec(ob), a_spec(oc),
                  pl.BlockSpec((1, k, tn), lambda i, j: (layer, 0, j))],
        out_specs=pl.BlockSpec((tm, tn), lambda i, j: (i, j)),
        out_shape=jax.ShapeDtypeStruct((m, n), jnp.bfloat16),
        compiler_params=_cparams(2),
        name="out_projection",
    )(oa, ob, oc, w)


def _rope_tables(s):
    rows = s // GRID_W
    row = jnp.repeat(jnp.arange(rows), GRID_W)
    col = jnp.tile(jnp.arange(GRID_W), rows)
    quarter = HEAD_DIM // 4
    freqs = ROPE_THETA ** (-jnp.arange(0, 2 * quarter, 2, dtype=jnp.float32) / (2 * quarter))
    ar = row.astype(jnp.float32)[:, None] * freqs[None, :]
    ac = col.astype(jnp.float32)[:, None] * freqs[None, :]
    cr, sr, cc, sc = jnp.cos(ar), jnp.sin(ar), jnp.cos(ac), jnp.sin(ac)
    cos = jnp.concatenate([cr, cr, cc, cc], axis=-1)
    sin = jnp.concatenate([-sr, sr, -sc, sc], axis=-1)
    return cos, sin


def _rope(x, cos, sin):
    quarter = HEAD_DIM // 4
    fwd = pltpu.roll(x, HEAD_DIM - quarter, 1)
    bwd = pltpu.roll(x, quarter, 1)
    lane = lax.broadcasted_iota(jnp.int32, x.shape, 1)
    partner = jnp.where((lane % (2 * quarter)) < quarter, fwd, bwd)
    return x * cos + partner * sin


def _prep_kernel(av_ref, bv_ref, cv_ref, ck_ref, cos_ref, sin_ref, kg_ref,
                 vta_ref, vtb_ref, vtc_ref, kc_ref, *, ts):
    def put_t(src_ref, n_heads, dst_ref, chunk):
        for hh in range(n_heads):
            vt = src_ref[:, hh * HEAD_DIM:(hh + 1) * HEAD_DIM].astype(jnp.float32).T
            for cidx in range(ts // chunk):
                dst_ref[hh, cidx, :HEAD_DIM] = vt[:, cidx * chunk:(cidx + 1) * chunk].astype(dst_ref.dtype)
                dst_ref[hh, cidx, HEAD_DIM:] = jnp.ones((VT_ROWS - HEAD_DIM, chunk), dst_ref.dtype)

    put_t(av_ref, A_HEADS, vta_ref, A_TILE)
    put_t(bv_ref, B_KV_HEADS, vtb_ref, B_TILE)
    put_t(cv_ref, C_KV_HEADS, vtc_ref, C_TK)
    cos, sin, kg = cos_ref[...], sin_ref[...], kg_ref[...]
    for hh in range(C_KV_HEADS):
        k = ck_ref[:, hh * HEAD_DIM:(hh + 1) * HEAD_DIM].astype(jnp.float32)
        k = _rope(_rms(k) * kg, cos, sin)
        kc_ref[:, hh * HEAD_DIM:(hh + 1) * HEAD_DIM] = k.astype(kc_ref.dtype)


def _prep(proj, cos, sin, k_gain):
    s = proj.shape[0]
    ts = max(A_TILE, B_TILE, C_TK)
    assert s % ts == 0
    bf = jnp.bfloat16
    n = s // ts
    vt_spec = lambda heads, t: pl.BlockSpec((heads, ts // t, VT_ROWS, t), lambda i: (0, i, 0, 0))
    vt_shape = lambda heads, t: jax.ShapeDtypeStruct((heads, s // t, VT_ROWS, t), bf)
    return pl.pallas_call(
        functools.partial(_prep_kernel, ts=ts),
        grid=(n,),
        in_specs=[pl.BlockSpec((ts, A_W), lambda i: (i, AV * HEAD_DIM // A_W)),
                  pl.BlockSpec((ts, B_KV_HEADS * HEAD_DIM), lambda i: (i, BV // B_KV_HEADS)),
                  pl.BlockSpec((ts, C_KV_HEADS * HEAD_DIM), lambda i: (i, CV // C_KV_HEADS)),
                  pl.BlockSpec((ts, C_KV_HEADS * HEAD_DIM), lambda i: (i, CK // C_KV_HEADS)),
                  pl.BlockSpec((ts, HEAD_DIM), lambda i: (i, 0)),
                  pl.BlockSpec((ts, HEAD_DIM), lambda i: (i, 0)),
                  pl.BlockSpec((1, HEAD_DIM), lambda i: (0, 0))],
        out_specs=[vt_spec(A_HEADS, A_TILE), vt_spec(B_KV_HEADS, B_TILE), vt_spec(C_KV_HEADS, C_TK),
                   pl.BlockSpec((ts, C_KV_HEADS * HEAD_DIM), lambda i: (i, 0))],
        out_shape=[vt_shape(A_HEADS, A_TILE), vt_shape(B_KV_HEADS, B_TILE), vt_shape(C_KV_HEADS, C_TK),
                   jax.ShapeDtypeStruct((s, C_KV_HEADS * HEAD_DIM), bf)],
        compiler_params=_cparams(1),
        name="vt_krope_prep",
    )(proj, proj, proj, proj, cos, sin, k_gain)


def _scores(k_ref, head, row0, tk, qt_ref, s_ref, mx_ref, slot, bias=None):
    kt = k_ref[pl.ds(pl.multiple_of(row0, tk), tk), head * HEAD_DIM:(head + 1) * HEAD_DIM]
    s = jnp.dot(kt, qt_ref[...], preferred_element_type=jnp.float32)
    if bias is not None:
        s = s + bias
    s_ref[slot] = s
    mx_ref[slot] = jnp.max(s, axis=0, keepdims=True)


def _accumulate(vt, s_ref, mx_ref, m_ref, acc_ref, slot):
    m_old = m_ref[...]
    m_new = jnp.maximum(m_old, mx_ref[slot])
    alpha = jnp.exp2(m_old - m_new)
    p = jnp.exp2(s_ref[slot] - m_new).astype(jnp.bfloat16)
    acc_ref[...] = alpha * acc_ref[...] + jnp.dot(vt, p, preferred_element_type=jnp.float32)
    m_ref[...] = m_new


def _load_q(q_refs, qt_ref, tq, prep=lambda q: q):
    scale2 = HEAD_DIM ** -0.5 * LOG2E
    for g, q_ref in enumerate(q_refs):
        q = prep(q_ref[...].astype(jnp.float32))
        qt_ref[:, g * tq:(g + 1) * tq] = (q * scale2).T.astype(qt_ref.dtype)


def _store_gated(acc_ref, g_refs, o_ref, tq, first=0):
    for g, g_ref in enumerate(g_refs):
        sl = slice(g * tq, (g + 1) * tq)
        o = (acc_ref[:HEAD_DIM, sl] / acc_ref[HEAD_DIM:HEAD_DIM + 1, sl]).T
        gate = g_ref[...].astype(jnp.float32)
        cols = slice((first + g) * HEAD_DIM, (first + g + 1) * HEAD_DIM)
        o_ref[:, cols] = (o * (gate * jax.nn.sigmoid(gate))).astype(o_ref.dtype)


def _band_kernel(*refs, heads, group, t, nkv, reach, has_sink):
    it = iter(refs)
    q_refs = [[next(it) for _ in range(group)] for _ in range(heads)]
    g_refs = [[next(it) for _ in range(group)] for _ in range(heads)]
    k_ref, vt_ref, bias_ref = next(it), next(it), next(it)
    sink_ref = next(it) if has_sink else None
    o_ref = next(it)
    qt_ref, s_ref, mx_ref, m_ref, acc_ref = (next(it) for _ in range(5))
    h = pl.program_id(0)
    qb = pl.program_id(1)
    n_off = 2 * reach + 1

    for hh in range(heads):
        _load_q(q_refs[hh], qt_ref.at[hh], t)
        acc_ref[hh, :HEAD_DIM] = jnp.zeros((HEAD_DIM, group * t), jnp.float32)
        if has_sink:
            for g in range(group):
                sink = sink_ref[(h * heads + hh) * group + g] * LOG2E
                m_ref[hh, :, g * t:(g + 1) * t] = jnp.full((1, t), sink, jnp.float32)
            acc_ref[hh, HEAD_DIM:] = jnp.ones((VT_ROWS - HEAD_DIM, group * t), jnp.float32)
        else:
            m_ref[hh] = jnp.full((1, group * t), M_INIT, jnp.float32)
            acc_ref[hh, HEAD_DIM:] = jnp.zeros((VT_ROWS - HEAD_DIM, group * t), jnp.float32)

    nsb = t // SUB
    n_sub = bias_ref.shape[1] - 1

    def block(o):
        kb = qb + (o - reach)
        return jnp.clip(kb, 0, nkv - 1), jnp.logical_and(kb >= 0, kb < nkv)

    def bias_tile(hh, o, valid):
        rows = []
        for a in range(nsb):
            cols = []
            for g in range(group):
                for b in range(nsb):
                    sub = (o - reach) * nsb + a - b + (n_sub - 1) // 2
                    cols.append(bias_ref[hh * group + g, jnp.where(valid, sub, n_sub)])
            rows.append(jnp.concatenate(cols, axis=1))
        return jnp.concatenate(rows, axis=0)

    units = [(hh, o) for hh in range(heads) for o in range(n_off)]

    def scores(u):
        hh, o = units[u]
        kb, valid = block(o)
        _scores(k_ref, hh, kb * t, t, qt_ref.at[hh], s_ref, mx_ref, u % 2, bias_tile(hh, o, valid))

    scores(0)
    for u, (hh, o) in enumerate(units):
        if u + 1 < len(units):
            scores(u + 1)
        _accumulate(vt_ref[hh, block(o)[0]], s_ref, mx_ref, m_ref.at[hh], acc_ref.at[hh], u % 2)
        if o == n_off - 1:
            _store_gated(acc_ref.at[hh], g_refs[hh], o_ref, t, hh * group)


def _dense_kernel(*refs, group, tq, tk, nkv):
    it = iter(refs)
    q_refs = [next(it) for _ in range(group)]
    g_refs = [next(it) for _ in range(group)]
    k_ref, vt_ref, cos_ref, sin_ref, qg_ref, o_ref = (next(it) for _ in range(6))
    qt_ref, s_ref, mx_ref, m_ref, acc_ref = (next(it) for _ in range(5))
    qt_ref, m_ref, acc_ref = qt_ref.at[0], m_ref.at[0], acc_ref.at[0]

    _load_q(q_refs, qt_ref, tq, lambda q: _rope(_rms(q) * qg_ref[...], cos_ref[...], sin_ref[...]))
    m_ref[...] = jnp.full(m_ref.shape, M_INIT, jnp.float32)
    acc_ref[...] = jnp.zeros(acc_ref.shape, jnp.float32)

    def scores(jblk, slot):
        _scores(k_ref, 0, jblk * tk, tk, qt_ref, s_ref, mx_ref, slot)

    def accumulate(jblk, slot):
        _accumulate(vt_ref[0, jblk], s_ref, mx_ref, m_ref, acc_ref, slot)

    scores(0, 0)

    def pair(j0):
        scores(j0 + 1, 1)
        accumulate(j0, 0)
        scores(j0 + 2, 0)
        accumulate(j0 + 1, 1)

    def pairs(it, carry):
        for u in range(DENSE_UNROLL):
            pair(2 * (it * DENSE_UNROLL + u))
        return carry

    n_pairs = nkv // 2 - 1
    lax.fori_loop(0, n_pairs // DENSE_UNROLL, pairs, 0)
    for jj in range(n_pairs - n_pairs % DENSE_UNROLL, n_pairs):
        pair(2 * jj)
    scores(nkv - 1, 1)
    accumulate(nkv - 2, 0)
    accumulate(nkv - 1, 1)
    _store_gated(acc_ref, g_refs, o_ref, tq)


def _attention(kern, proj, k_src, k_col, vt, q_col, g_col, n_kv_heads, group, tq, tk, extra_specs,
               extra_args, name, heads=1):
    s = proj.shape[0]
    assert s % tq == 0 and s % tk == 0 and n_kv_heads % heads == 0 and k_col % heads == 0
    in_specs, args = [], []
    for col in (q_col, g_col):
        for hh in range(heads):
            for g in range(group):
                in_specs.append(pl.BlockSpec(
                    (tq, HEAD_DIM), lambda h, qb, col=col, hh=hh, g=g: (qb, col + (h * heads + hh) * group + g)))
                args.append(proj)
    in_specs += [pl.BlockSpec((s, heads * HEAD_DIM), lambda h, qb: (0, k_col // heads + h)),
                 pl.BlockSpec((heads, s // tk, VT_ROWS, tk), lambda h, qb: (h, 0, 0, 0))] + extra_specs
    args += [k_src, vt] + extra_args
    n = group * tq
    return pl.pallas_call(
        kern,
        grid=(n_kv_heads // heads, s // tq),
        in_specs=in_specs,
        out_specs=pl.BlockSpec((tq, heads * group * HEAD_DIM), lambda h, qb: (qb, h)),
        out_shape=jax.ShapeDtypeStruct((s, n_kv_heads * group * HEAD_DIM), jnp.bfloat16),
        scratch_shapes=[pltpu.VMEM((heads, HEAD_DIM, n), jnp.bfloat16),
                        pltpu.VMEM((2, tk, n), jnp.float32),
                        pltpu.VMEM((2, 1, n), jnp.float32),
                        pltpu.VMEM((heads, 1, n), jnp.float32),
                        pltpu.VMEM((heads, VT_ROWS, n), jnp.float32)],
        compiler_params=_cparams(2),
        name=name,
    )(*args)


def _band_attention(proj, k_col, vt, q_col, g_col, n_kv_heads, group, t, reach, bias, sink=None,
                    heads=BAND_HEADS_PER_STEP):
    nkv = proj.shape[0] // t
    extra_specs = [pl.BlockSpec((heads * group,) + bias.shape[1:], lambda h, qb: (h, 0, 0, 0))]
    extra_args = [bias]
    if sink is not None:
        extra_specs.append(pl.BlockSpec(memory_space=pltpu.SMEM))
        extra_args.append(sink)
    kern = functools.partial(_band_kernel, heads=heads, group=group, t=t, nkv=nkv, reach=reach,
                             has_sink=sink is not None)
    return _attention(kern, proj, proj, k_col, vt, q_col, g_col, n_kv_heads, group, t, t,
                      extra_specs, extra_args, "band_attention_g%d" % group, heads)


def _dense_attention(proj, kc, vt, q_col, g_col, n_kv_heads, group, tq, tk, cos, sin, q_gain):
    nkv = proj.shape[0] // tk
    assert nkv % 2 == 0
    extra_specs = [pl.BlockSpec((tq, HEAD_DIM), lambda h, qb: (qb, 0)),
                   pl.BlockSpec((tq, HEAD_DIM), lambda h, qb: (qb, 0)),
                   pl.BlockSpec((1, HEAD_DIM), lambda h, qb: (0, 0))]
    kern = functools.partial(_dense_kernel, group=group, tq=tq, tk=tk, nkv=nkv)
    return _attention(kern, proj, kc, 0, vt, q_col, g_col, n_kv_heads, group, tq, tk,
                      extra_specs, [cos, sin, q_gain], "dense_attention_g%d" % group)


def _t5_bucket_np(rel):
    half_b = NUM_BUCKETS // 2
    max_exact = half_b // 2
    ret = np.where(rel > 0, half_b, 0)
    n = np.abs(rel)
    nf = np.maximum(n, 1).astype(np.float32)
    large = max_exact + (np.log(nf / np.float32(max_exact)) / np.float32(math.log(MAX_DISTANCE / max_exact))
                         * np.float32(half_b - max_exact)).astype(np.int32)
    large = np.minimum(large, half_b - 1)
    return (ret + np.where(n < max_exact, n, large)).astype(np.int32)


def _toeplitz_tiles(vec, tile, reach):
    heads = vec.shape[0]
    r = (vec.shape[1] - 1) // 2
    n_off = 2 * reach + 1
    span = 2 * tile - 1
    lo = -reach * tile - (tile - 1)
    hi = reach * tile + (tile - 1)
    padded = jnp.pad(vec, ((0, 0), (-lo - r, hi - r)), constant_values=MASKED)
    rows = [padded[:, o * tile:o * tile + span] for o in range(n_off)]
    u = jnp.stack(rows, axis=1)[:, :, ::-1]
    w = jnp.pad(u, ((0, 0), (0, 0), (0, 1)))
    flat = jnp.broadcast_to(w[:, :, None, :], (heads, n_off, tile, span + 1))
    flat = flat.reshape(heads, n_off, tile * (span + 1))[:, :, :tile * span]
    skew = flat.reshape(heads, n_off, tile, span)
    tiles = skew[:, :, :, tile - 1:]
    masked = jnp.full((tiles.shape[0], 1) + tiles.shape[2:], MASKED, tiles.dtype)
    return jnp.concatenate([tiles, masked], axis=1)


def _bias_tiles(rel_bias):
    hi = lax.Precision.HIGHEST
    d = np.arange(-A_REACH, A_REACH + 1)
    count = np.zeros(d.shape, np.float64)
    for w, dil in A_PATTERNS:
        count += (d % dil == 0) & (np.abs(d) <= w // 2)
    log_count = np.where(count > 0, np.log(np.maximum(count, 1.0)), 0.0).astype(np.float32)
    table = rel_bias.astype(jnp.float32)
    onehot_a = jnp.asarray(np.eye(NUM_BUCKETS, dtype=np.float32)[_t5_bucket_np(d)])
    vec_a = (jnp.dot(onehot_a, table[:, :A_HEADS], precision=hi).T + log_count[None]) * LOG2E
    vec_a = jnp.where(jnp.asarray(count > 0)[None], vec_a, MASKED)
    tiles_a = _toeplitz_tiles(vec_a, SUB, (-(-A_REACH // A_TILE) + 1) * (A_TILE // SUB) - 1)
    db = np.arange(-B_HALF_WINDOW, B_HALF_WINDOW + 1)
    onehot_b = jnp.asarray(np.eye(NUM_BUCKETS, dtype=np.float32)[_t5_bucket_np(db)])
    vec_b = jnp.dot(onehot_b, table[:, A_HEADS:], precision=hi).T * LOG2E
    tiles_b = _toeplitz_tiles(vec_b, SUB, (-(-B_HALF_WINDOW // B_TILE) + 1) * (B_TILE // SUB) - 1)
    return tiles_a, tiles_b


def kernel(x, c, w_mod, b_mod, pre_norm, post_norm, w_in, w_out, q_norm, k_norm, sink, rel_bias):
    bsz, s, d = x.shape
    assert bsz == 1 and c.shape[0] == 1
    depth = w_in.shape[0]
    assert w_in.shape[2] == IN_WIDTH and w_out.shape[1] == A_W + B_W + C_W
    x2 = x[0]
    cos, sin = _rope_tables(s)
    tiles_a, tiles_b = _bias_tiles(rel_bias)
    mod = _modulation(c, w_mod, b_mod)[:, 0]
    shift, scale, gate = mod[:, :d], mod[:, d:2 * d], mod[:, 2 * d:]
    vec = lambda a, i: a[i][None, :]

    h = _prenorm(x2, vec(pre_norm, 0), vec(scale, 0), vec(shift, 0))
    for i in range(depth):
        proj = _in_projection(h, w_in, i)
        vta, vtb, vtc, kc = _prep(proj, cos, sin, vec(k_norm, i))
        oa = _band_attention(proj, AK, vta, AQ, AG, A_HEADS, 1, A_TILE, -(-A_REACH // A_TILE), tiles_a)
        ob = _band_attention(proj, BK, vtb, BQ, BG, B_KV_HEADS, B_HEADS // B_KV_HEADS, B_TILE,
                             -(-B_HALF_WINDOW // B_TILE), tiles_b, sink[i])
        oc = _dense_attention(proj, kc, vtc, CQ, CG, C_KV_HEADS, C_HEADS // C_KV_HEADS, C_TQ, C_TK,
                              cos, sin, vec(q_norm, i))
        y = _out_projection(oa, ob, oc, w_out, i)
        nxt = None if i == depth - 1 else (vec(pre_norm, i + 1), vec(scale, i + 1), vec(shift, i + 1))
        x2, h = _post(x2, y, vec(post_norm, i), vec(gate, i), nxt)
    return x2[None]
```

```python
import functools
import math

import numpy as np
import jax
import jax.numpy as jnp
from jax import lax
from jax.experimental import pallas as pl
from jax.experimental.pallas import tpu as pltpu

HEAD_DIM = 128
GRID_W = 64
A_HEADS = 12
A_PATTERNS = ((128, 1), (512, 4), (2048, 16))
B_HEADS = 8
B_KV_HEADS = 2
B_HALF_WINDOW = 128
C_HEADS = 12
C_KV_HEADS = 4
ROPE_THETA = 10000.0
NUM_BUCKETS = 32
MAX_DISTANCE = 1024
EPS = 1e-6
LOG2E = math.log2(math.e)
MASKED = -1e30
M_INIT = -1e29

A_W = A_HEADS * HEAD_DIM
B_W = B_HEADS * HEAD_DIM
C_W = C_HEADS * HEAD_DIM
AQ, AK, AV, AG = 0, 12, 24, 36
BQ, BK, BV, BG = 48, 56, 58, 60
CQ, CK, CV, CG = 68, 80, 84, 88
IN_WIDTH = 100 * HEAD_DIM

A_REACH = max(w // 2 for w, _ in A_PATTERNS)
A_TILE = 512
B_TILE = 256
C_TQ = 256
C_TK = 1024
BAND_HEADS_PER_STEP = 2
BOUNDED_GROUP = 4
NORM_SLACK = 1.02
LOGIT_BOUND = 64.0
DENSE_UNROLL = 3
VT_ROWS = HEAD_DIM + 16
V7X_VMEM_LIMIT = 56 * 1024 * 1024
SUB = 128


def _cparams(n_axes):
    return pltpu.CompilerParams(dimension_semantics=("arbitrary",) * n_axes,
                                vmem_limit_bytes=V7X_VMEM_LIMIT)


def _tile(n, pref):
    t = min(n, pref)
    while n % t:
        t //= 2
    return t


def _mod_kernel(c_ref, w_ref, b_ref, o_ref):
    c = c_ref[...]
    cs = jnp.broadcast_to(c * jax.nn.sigmoid(c), (8, c.shape[1]))
    r = jnp.dot(cs, w_ref[0], preferred_element_type=jnp.float32)
    o_ref[0] = r[0:1] + b_ref[0]


def _modulation(c, w_mod, b_mod):
    depth, d, n = w_mod.shape
    tn = _tile(n, 1024)
    out = pl.pallas_call(
        _mod_kernel,
        grid=(depth, n // tn),
        in_specs=[pl.BlockSpec((1, d), lambda i, j: (0, 0)),
                  pl.BlockSpec((1, d, tn), lambda i, j: (i, 0, j)),
                  pl.BlockSpec((1, 1, tn), lambda i, j: (i, 0, j))],
        out_specs=pl.BlockSpec((1, 1, tn), lambda i, j: (i, 0, j)),
        out_shape=jax.ShapeDtypeStruct((depth, 1, n), jnp.float32),
        compiler_params=_cparams(2),
        name="modulation",
    )(c, w_mod, b_mod.reshape(depth, 1, n))
    return out


def _rms(x):
    return x * lax.rsqrt(jnp.mean(x * x, axis=-1, keepdims=True) + EPS)


def _prenorm_kernel(x_ref, g_ref, sc_ref, sh_ref, h_ref):
    y = _rms(x_ref[...]) * g_ref[...]
    h_ref[...] = (y * (1.0 + sc_ref[...]) + sh_ref[...]).astype(h_ref.dtype)


def _prenorm(x, g, scale, shift):
    s, d = x.shape
    tm = _tile(s, 512)
    vec = pl.BlockSpec((1, d), lambda i: (0, 0))
    return pl.pallas_call(
        _prenorm_kernel,
        grid=(s // tm,),
        in_specs=[pl.BlockSpec((tm, d), lambda i: (i, 0)), vec, vec, vec],
        out_specs=pl.BlockSpec((tm, d), lambda i: (i, 0)),
        out_shape=jax.ShapeDtypeStruct((s, d), jnp.bfloat16),
        compiler_params=_cparams(1),
        name="prenorm",
    )(x, g, scale, shift)


def _post_kernel(x_ref, y_ref, pg_ref, gate_ref, xo_ref):
    yn = _rms(y_ref[...].astype(jnp.float32)) * pg_ref[...]
    xo_ref[...] = x_ref[...] + gate_ref[...] * yn


def _postpre_kernel(x_ref, y_ref, pg_ref, gate_ref, g_ref, sc_ref, sh_ref, xo_ref, h_ref):
    yn = _rms(y_ref[...].astype(jnp.float32)) * pg_ref[...]
    xn = x_ref[...] + gate_ref[...] * yn
    xo_ref[...] = xn
    h = _rms(xn) * g_ref[...]
    h_ref[...] = (h * (1.0 + sc_ref[...]) + sh_ref[...]).astype(h_ref.dtype)


def _post(x, y, post_g, gate, nxt=None):
    s, d = x.shape
    tm = _tile(s, 256)
    row = pl.BlockSpec((tm, d), lambda i: (i, 0))
    vec = pl.BlockSpec((1, d), lambda i: (0, 0))
    if nxt is None:
        return pl.pallas_call(
            _post_kernel, grid=(s // tm,),
            in_specs=[row, row, vec, vec], out_specs=row,
            out_shape=jax.ShapeDtypeStruct((s, d), jnp.float32),
            compiler_params=_cparams(1), name="postnorm",
        )(x, y, post_g, gate), None
    g, scale, shift = nxt
    return pl.pallas_call(
        _postpre_kernel, grid=(s // tm,),
        in_specs=[row, row, vec, vec, vec, vec, vec], out_specs=[row, row],
        out_shape=[jax.ShapeDtypeStruct((s, d), jnp.float32),
                   jax.ShapeDtypeStruct((s, d), jnp.bfloat16)],
        compiler_params=_cparams(1), name="postnorm_prenorm",
    )(x, y, post_g, gate, g, scale, shift)


def _mm_kernel(a_ref, b_ref, o_ref):
    o_ref[...] = jnp.dot(a_ref[...], b_ref[0].astype(a_ref.dtype),
                         preferred_element_type=jnp.float32).astype(o_ref.dtype)


def _in_projection(h, w, layer):
    m, k = h.shape
    n = w.shape[2]
    tm, tn = _tile(m, 1024), _tile(n, 512)
    return pl.pallas_call(
        _mm_kernel,
        grid=(m // tm, n // tn),
        in_specs=[pl.BlockSpec((tm, k), lambda i, j: (i, 0)),
                  pl.BlockSpec((1, k, tn), lambda i, j: (layer, 0, j))],
        out_specs=pl.BlockSpec((tm, tn), lambda i, j: (i, j)),
        out_shape=jax.ShapeDtypeStruct((m, n), jnp.bfloat16),
        compiler_params=_cparams(2),
        name="in_projection",
    )(h, w)


def _mm3_kernel(a1, a2, a3, w_ref, o_ref):
    acc, row = None, 0
    for a in (a1, a2, a3):
        w = w_ref[0, row:row + a.shape[1]].astype(a.dtype)
        part = jnp.dot(a[...], w, preferred_element_type=jnp.float32)
        acc = part if acc is None else acc + part
        row += a.shape[1]
    o_ref[...] = acc.astype(o_ref.dtype)


def _out_projection(oa, ob, oc, w, layer):
    m = oa.shape[0]
    k, n = w.shape[1:]
    tm, tn = _tile(m, 1024), _tile(n, 512)
    a_spec = lambda a: pl.BlockSpec((tm, a.shape[1]), lambda i, j: (i, 0))
    return pl.pallas_call(
        _mm3_kernel,
        grid=(m // tm, n // tn),
        in_specs=[a_spec(oa), a_spec(ob), a_spec(oc),
                  pl.BlockSpec((1, k, tn), lambda i, j: (layer, 0, j))],
        out_specs=pl.BlockSpec((tm, tn), lambda i, j: (i, j)),
        out_shape=jax.ShapeDtypeStruct((m, n), jnp.bfloat16),
        compiler_params=_cparams(2),
        name="out_projection",
    )(oa, ob, oc, w)


def _rope_tables(s):
    rows = s // GRID_W
    row = jnp.repeat(jnp.arange(rows), GRID_W)
    col = jnp.tile(jnp.arange(GRID_W), rows)
    quarter = HEAD_DIM // 4
    freqs = ROPE_THETA ** (-jnp.arange(0, 2 * quarter, 2, dtype=jnp.float32) / (2 * quarter))
    ar = row.astype(jnp.float32)[:, None] * freqs[None, :]
    ac = col.astype(jnp.float32)[:, None] * freqs[None, :]
    cr, sr, cc, sc = jnp.cos(ar), jnp.sin(ar), jnp.cos(ac), jnp.sin(ac)
    cos = jnp.concatenate([cr, cr, cc, cc], axis=-1)
    sin = jnp.concatenate([-sr, sr, -sc, sc], axis=-1)
    return cos, sin


def _rope(x, cos, sin):
    quarter = HEAD_DIM // 4
    fwd = pltpu.roll(x, HEAD_DIM - quarter, 1)
    bwd = pltpu.roll(x, quarter, 1)
    lane = lax.broadcasted_iota(jnp.int32, x.shape, 1)
    partner = jnp.where((lane % (2 * quarter)) < quarter, fwd, bwd)
    return x * cos + partner * sin


def _prep_kernel(av_ref, bv_ref, cv_ref, ck_ref, aq_ref, ak_ref, bq_ref, bk_ref, cos_ref, sin_ref, kg_ref,
                 vta_ref, vtb_ref, vtc_ref, kc_ref, nrm_ref, *, ts):
    def put_t(src_ref, n_heads, dst_ref, chunk):
        for hh in range(n_heads):
            vt = src_ref[:, hh * HEAD_DIM:(hh + 1) * HEAD_DIM].astype(jnp.float32).T
            for cidx in range(ts // chunk):
                dst_ref[hh, cidx, :HEAD_DIM] = vt[:, cidx * chunk:(cidx + 1) * chunk].astype(dst_ref.dtype)
                dst_ref[hh, cidx, HEAD_DIM:] = jnp.ones((VT_ROWS - HEAD_DIM, chunk), dst_ref.dtype)

    put_t(av_ref, A_HEADS, vta_ref, A_TILE)
    put_t(bv_ref, B_KV_HEADS, vtb_ref, B_TILE)
    put_t(cv_ref, C_KV_HEADS, vtc_ref, C_TK)
    cos, sin, kg = cos_ref[...], sin_ref[...], kg_ref[...]
    for hh in range(C_KV_HEADS):
        k = ck_ref[:, hh * HEAD_DIM:(hh + 1) * HEAD_DIM].astype(jnp.float32)
        k = _rope(_rms(k) * kg, cos, sin)
        kc_ref[:, hh * HEAD_DIM:(hh + 1) * HEAD_DIM] = k.astype(kc_ref.dtype)

    def max_sq_norm(src_ref, n_heads):
        ones = jnp.ones((HEAD_DIM, HEAD_DIM), src_ref.dtype)
        best = None
        for hh in range(n_heads):
            x = src_ref[:, hh * HEAD_DIM:(hh + 1) * HEAD_DIM]
            row_sums = jnp.dot(x * x, ones, preferred_element_type=jnp.float32)
            top = jnp.max(row_sums, axis=0, keepdims=True)
            best = top if best is None else jnp.maximum(best, top)
        return best

    nrm_ref[0] = jnp.concatenate(
        [max_sq_norm(aq_ref, A_HEADS), max_sq_norm(ak_ref, A_HEADS), max_sq_norm(bq_ref, B_HEADS),
         max_sq_norm(bk_ref, B_KV_HEADS), jnp.zeros((4, HEAD_DIM), jnp.float32)], axis=0)


def _prep(proj, cos, sin, k_gain):
    s = proj.shape[0]
    ts = max(A_TILE, B_TILE, C_TK)
    assert s % ts == 0
    bf = jnp.bfloat16
    n = s // ts
    vt_spec = lambda heads, t: pl.BlockSpec((heads, ts // t, VT_ROWS, t), lambda i: (0, i, 0, 0))
    vt_shape = lambda heads, t: jax.ShapeDtypeStruct((heads, s // t, VT_ROWS, t), bf)
    return pl.pallas_call(
        functools.partial(_prep_kernel, ts=ts),
        grid=(n,),
        in_specs=[pl.BlockSpec((ts, A_W), lambda i: (i, AV * HEAD_DIM // A_W)),
                  pl.BlockSpec((ts, B_KV_HEADS * HEAD_DIM), lambda i: (i, BV // B_KV_HEADS)),
                  pl.BlockSpec((ts, C_KV_HEADS * HEAD_DIM), lambda i: (i, CV // C_KV_HEADS)),
                  pl.BlockSpec((ts, C_KV_HEADS * HEAD_DIM), lambda i: (i, CK // C_KV_HEADS)),
                  pl.BlockSpec((ts, A_W), lambda i: (i, AQ * HEAD_DIM // A_W)),
                  pl.BlockSpec((ts, A_W), lambda i: (i, AK * HEAD_DIM // A_W)),
                  pl.BlockSpec((ts, B_W), lambda i: (i, BQ * HEAD_DIM // B_W)),
                  pl.BlockSpec((ts, B_KV_HEADS * HEAD_DIM), lambda i: (i, BK // B_KV_HEADS)),
                  pl.BlockSpec((ts, HEAD_DIM), lambda i: (i, 0)),
                  pl.BlockSpec((ts, HEAD_DIM), lambda i: (i, 0)),
                  pl.BlockSpec((1, HEAD_DIM), lambda i: (0, 0))],
        out_specs=[vt_spec(A_HEADS, A_TILE), vt_spec(B_KV_HEADS, B_TILE), vt_spec(C_KV_HEADS, C_TK),
                   pl.BlockSpec((ts, C_KV_HEADS * HEAD_DIM), lambda i: (i, 0)),
                   pl.BlockSpec((1, 8, HEAD_DIM), lambda i: (i, 0, 0))],
        out_shape=[vt_shape(A_HEADS, A_TILE), vt_shape(B_KV_HEADS, B_TILE), vt_shape(C_KV_HEADS, C_TK),
                   jax.ShapeDtypeStruct((s, C_KV_HEADS * HEAD_DIM), bf),
                   jax.ShapeDtypeStruct((n, 8, HEAD_DIM), jnp.float32)],
        compiler_params=_cparams(1),
        name="vt_krope_prep",
    )(proj, proj, proj, proj, proj, proj, proj, proj, cos, sin, k_gain)


def _scores(k_ref, head, row0, tk, qt_ref, s_ref, mx_ref, slot, bias=None):
    kt = k_ref[pl.ds(pl.multiple_of(row0, tk), tk), head * HEAD_DIM:(head + 1) * HEAD_DIM]
    s = jnp.dot(kt, qt_ref[...], preferred_element_type=jnp.float32)
    if bias is not None:
        s = s + bias
    s_ref[slot] = s
    mx_ref[slot] = jnp.max(s, axis=0, keepdims=True)


def _accumulate(vt, s_ref, mx_ref, m_ref, acc_ref, slot):
    m_old = m_ref[...]
    m_new = jnp.maximum(m_old, mx_ref[slot])
    alpha = jnp.exp2(m_old - m_new)
    p = jnp.exp2(s_ref[slot] - m_new).astype(jnp.bfloat16)
    acc_ref[...] = alpha * acc_ref[...] + jnp.dot(vt, p, preferred_element_type=jnp.float32)
    m_ref[...] = m_new


def _load_q(q_refs, qt_ref, tq, prep=lambda q: q):
    scale2 = HEAD_DIM ** -0.5 * LOG2E
    for g, q_ref in enumerate(q_refs):
        q = prep(q_ref[...].astype(jnp.float32))
        qt_ref[:, g * tq:(g + 1) * tq] = (q * scale2).T.astype(qt_ref.dtype)


def _store_gated(acc, g_refs, o_ref, tq, first=0, extra_denominator=None):
    for g, g_ref in enumerate(g_refs):
        sl = slice(g * tq, (g + 1) * tq)
        den = acc[HEAD_DIM:HEAD_DIM + 1, sl]
        if extra_denominator is not None:
            den = den + extra_denominator[g]
        o = (acc[:HEAD_DIM, sl] / den).T
        gate = g_ref[...].astype(jnp.float32)
        cols = slice((first + g) * HEAD_DIM, (first + g + 1) * HEAD_DIM)
        o_ref[:, cols] = (o * (gate * jax.nn.sigmoid(gate))).astype(o_ref.dtype)


def _band_kernel(*refs, heads, group, t, nkv, reach, has_sink, bounded):
    it = iter(refs)
    q_refs = [[next(it) for _ in range(group)] for _ in range(heads)]
    g_refs = [[next(it) for _ in range(group)] for _ in range(heads)]
    k_ref, vt_ref, bias_ref = next(it), next(it), next(it)
    sink_ref = next(it) if has_sink else None
    o_ref = next(it)
    qt_ref, s_ref, mx_ref, m_ref, acc_ref = (next(it) for _ in range(5))
    h = pl.program_id(0)
    qb = pl.program_id(1)
    n_off = 2 * reach + 1

    nsb = t // SUB
    n_sub = bias_ref.shape[1] - 1

    def block(o):
        kb = qb + (o - reach)
        return jnp.clip(kb, 0, nkv - 1), jnp.logical_and(kb >= 0, kb < nkv)

    def bias_tile(hh, o, valid):
        rows = []
        for a in range(nsb):
            cols = []
            for g in range(group):
                for b in range(nsb):
                    sub = (o - reach) * nsb + a - b + (n_sub - 1) // 2
                    cols.append(bias_ref[hh * group + g, jnp.where(valid, sub, n_sub)])
            rows.append(jnp.concatenate(cols, axis=1))
        return jnp.concatenate(rows, axis=0)

    if bounded:
        for hh in range(heads):
            _load_q(q_refs[hh], qt_ref.at[hh], t)
        for hh in range(heads):
            total = None
            for o in range(n_off):
                kb, valid = block(o)
                kt = k_ref[pl.ds(pl.multiple_of(kb * t, t), t), hh * HEAD_DIM:(hh + 1) * HEAD_DIM]
                s = jnp.dot(kt, qt_ref[hh], preferred_element_type=jnp.float32) + bias_tile(hh, o, valid)
                part = jnp.dot(vt_ref[hh, kb], jnp.exp2(s).astype(jnp.bfloat16),
                               preferred_element_type=jnp.float32)
                total = part if total is None else total + part
            sinks = None
            if has_sink:
                sinks = [jnp.exp2(jnp.full((1, t), sink_ref[(h * heads + hh) * group + g] * LOG2E, jnp.float32))
                         for g in range(group)]
            _store_gated(total, g_refs[hh], o_ref, t, hh * group, sinks)
        return

    for hh in range(heads):
        _load_q(q_refs[hh], qt_ref.at[hh], t)
        acc_ref[hh, :HEAD_DIM] = jnp.zeros((HEAD_DIM, group * t), jnp.float32)
        if has_sink:
            for g in range(group):
                sink = sink_ref[(h * heads + hh) * group + g] * LOG2E
                m_ref[hh, :, g * t:(g + 1) * t] = jnp.full((1, t), sink, jnp.float32)
            acc_ref[hh, HEAD_DIM:] = jnp.ones((VT_ROWS - HEAD_DIM, group * t), jnp.float32)
        else:
            m_ref[hh] = jnp.full((1, group * t), M_INIT, jnp.float32)
            acc_ref[hh, HEAD_DIM:] = jnp.zeros((VT_ROWS - HEAD_DIM, group * t), jnp.float32)

    units = [(hh, o) for hh in range(heads) for o in range(n_off)]

    def scores(u):
        hh, o = units[u]
        kb, valid = block(o)
        _scores(k_ref, hh, kb * t, t, qt_ref.at[hh], s_ref, mx_ref, u % 2, bias_tile(hh, o, valid))

    scores(0)
    for u, (hh, o) in enumerate(units):
        if u + 1 < len(units):
            scores(u + 1)
        _accumulate(vt_ref[hh, block(o)[0]], s_ref, mx_ref, m_ref.at[hh], acc_ref.at[hh], u % 2)
        if o == n_off - 1:
            _store_gated(acc_ref.at[hh], g_refs[hh], o_ref, t, hh * group)


def _dense_kernel(*refs, group, tq, tk, nkv, bounded):
    it = iter(refs)
    q_refs = [next(it) for _ in range(group)]
    g_refs = [next(it) for _ in range(group)]
    k_ref, vt_ref, cos_ref, sin_ref, qg_ref, o_ref = (next(it) for _ in range(6))
    qt_ref, s_ref, mx_ref, m_ref, acc_ref = (next(it) for _ in range(5))
    qt_ref, m_ref, acc_ref = qt_ref.at[0], m_ref.at[0], acc_ref.at[0]

    _load_q(q_refs, qt_ref, tq, lambda q: _rope(_rms(q) * qg_ref[...], cos_ref[...], sin_ref[...]))
    acc_ref[...] = jnp.zeros(acc_ref.shape, jnp.float32)

    if bounded:
        def block(jblk):
            kt = k_ref[pl.ds(pl.multiple_of(jblk * tk, tk), tk), :]
            s = jnp.dot(kt, qt_ref[...], preferred_element_type=jnp.float32)
            return jnp.dot(vt_ref[0, jblk], jnp.exp2(s).astype(jnp.bfloat16),
                           preferred_element_type=jnp.float32)

        per_iter = math.gcd(nkv, BOUNDED_GROUP)

        def blocks(it, carry):
            part = block(it * per_iter)
            for u in range(1, per_iter):
                part = part + block(it * per_iter + u)
            acc_ref[...] += part
            return carry

        lax.fori_loop(0, nkv // per_iter, blocks, 0)
        _store_gated(acc_ref, g_refs, o_ref, tq)
        return

    m_ref[...] = jnp.full(m_ref.shape, M_INIT, jnp.float32)

    def scores(jblk, slot):
        _scores(k_ref, 0, jblk * tk, tk, qt_ref, s_ref, mx_ref, slot)

    def accumulate(jblk, slot):
        _accumulate(vt_ref[0, jblk], s_ref, mx_ref, m_ref, acc_ref, slot)

    scores(0, 0)

    def pair(j0):
        scores(j0 + 1, 1)
        accumulate(j0, 0)
        scores(j0 + 2, 0)
        accumulate(j0 + 1, 1)

    def pairs(it, carry):
        for u in range(DENSE_UNROLL):
            pair(2 * (it * DENSE_UNROLL + u))
        return carry

    n_pairs = nkv // 2 - 1
    lax.fori_loop(0, n_pairs // DENSE_UNROLL, pairs, 0)
    for jj in range(n_pairs - n_pairs % DENSE_UNROLL, n_pairs):
        pair(2 * jj)
    scores(nkv - 1, 1)
    accumulate(nkv - 2, 0)
    accumulate(nkv - 1, 1)
    _store_gated(acc_ref, g_refs, o_ref, tq)


def _attention(kern, proj, k_src, k_col, vt, q_col, g_col, n_kv_heads, group, tq, tk, extra_specs,
               extra_args, name, heads=1):
    s = proj.shape[0]
    assert s % tq == 0 and s % tk == 0 and n_kv_heads % heads == 0 and k_col % heads == 0
    in_specs, args = [], []
    for col in (q_col, g_col):
        for hh in range(heads):
            for g in range(group):
                in_specs.append(pl.BlockSpec(
                    (tq, HEAD_DIM), lambda h, qb, col=col, hh=hh, g=g: (qb, col + (h * heads + hh) * group + g)))
                args.append(proj)
    in_specs += [pl.BlockSpec((s, heads * HEAD_DIM), lambda h, qb: (0, k_col // heads + h)),
                 pl.BlockSpec((heads, s // tk, VT_ROWS, tk), lambda h, qb: (h, 0, 0, 0))] + extra_specs
    args += [k_src, vt] + extra_args
    n = group * tq
    return pl.pallas_call(
        kern,
        grid=(n_kv_heads // heads, s // tq),
        in_specs=in_specs,
        out_specs=pl.BlockSpec((tq, heads * group * HEAD_DIM), lambda h, qb: (qb, h)),
        out_shape=jax.ShapeDtypeStruct((s, n_kv_heads * group * HEAD_DIM), jnp.bfloat16),
        scratch_shapes=[pltpu.VMEM((heads, HEAD_DIM, n), jnp.bfloat16),
                        pltpu.VMEM((2, tk, n), jnp.float32),
                        pltpu.VMEM((2, 1, n), jnp.float32),
                        pltpu.VMEM((heads, 1, n), jnp.float32),
                        pltpu.VMEM((heads, VT_ROWS, n), jnp.float32)],
        compiler_params=_cparams(2),
        name=name,
    )(*args)


def _band_attention(proj, k_col, vt, q_col, g_col, n_kv_heads, group, t, reach, bias, sink, bounded,
                    heads=BAND_HEADS_PER_STEP):
    nkv = proj.shape[0] // t
    extra_specs = [pl.BlockSpec((heads * group,) + bias.shape[1:], lambda h, qb: (h, 0, 0, 0))]
    extra_args = [bias]
    if sink is not None:
        extra_specs.append(pl.BlockSpec(memory_space=pltpu.SMEM))
        extra_args.append(sink)
    kern = functools.partial(_band_kernel, heads=heads, group=group, t=t, nkv=nkv, reach=reach,
                             has_sink=sink is not None, bounded=bounded)
    name = "band_attention_g%d%s" % (group, "_bounded" if bounded else "")
    return _attention(kern, proj, proj, k_col, vt, q_col, g_col, n_kv_heads, group, t, t,
                      extra_specs, extra_args, name, heads)


def _dense_attention(proj, kc, vt, q_col, g_col, n_kv_heads, group, tq, tk, cos, sin, q_gain, bounded):
    nkv = proj.shape[0] // tk
    assert nkv % 2 == 0
    extra_specs = [pl.BlockSpec((tq, HEAD_DIM), lambda h, qb: (qb, 0)),
                   pl.BlockSpec((tq, HEAD_DIM), lambda h, qb: (qb, 0)),
                   pl.BlockSpec((1, HEAD_DIM), lambda h, qb: (0, 0))]
    kern = functools.partial(_dense_kernel, group=group, tq=tq, tk=tk, nkv=nkv, bounded=bounded)
    name = "dense_attention_g%d%s" % (group, "_bounded" if bounded else "")
    return _attention(kern, proj, kc, 0, vt, q_col, g_col, n_kv_heads, group, tq, tk,
                      extra_specs, [cos, sin, q_gain], name)


def _t5_bucket_np(rel):
    half_b = NUM_BUCKETS // 2
    max_exact = half_b // 2
    ret = np.where(rel > 0, half_b, 0)
    n = np.abs(rel)
    nf = np.maximum(n, 1).astype(np.float32)
    large = max_exact + (np.log(nf / np.float32(max_exact)) / np.float32(math.log(MAX_DISTANCE / max_exact))
                         * np.float32(half_b - max_exact)).astype(np.int32)
    large = np.minimum(large, half_b - 1)
    return (ret + np.where(n < max_exact, n, large)).astype(np.int32)


def _toeplitz_tiles(vec, tile, reach):
    heads = vec.shape[0]
    r = (vec.shape[1] - 1) // 2
    n_off = 2 * reach + 1
    span = 2 * tile - 1
    lo = -reach * tile - (tile - 1)
    hi = reach * tile + (tile - 1)
    padded = jnp.pad(vec, ((0, 0), (-lo - r, hi - r)), constant_values=MASKED)
    rows = [padded[:, o * tile:o * tile + span] for o in range(n_off)]
    u = jnp.stack(rows, axis=1)[:, :, ::-1]
    w = jnp.pad(u, ((0, 0), (0, 0), (0, 1)))
    flat = jnp.broadcast_to(w[:, :, None, :], (heads, n_off, tile, span + 1))
    flat = flat.reshape(heads, n_off, tile * (span + 1))[:, :, :tile * span]
    skew = flat.reshape(heads, n_off, tile, span)
    tiles = skew[:, :, :, tile - 1:]
    masked = jnp.full((tiles.shape[0], 1) + tiles.shape[2:], MASKED, tiles.dtype)
    return jnp.concatenate([tiles, masked], axis=1)


def _bias_tiles(rel_bias):
    hi = lax.Precision.HIGHEST
    d = np.arange(-A_REACH, A_REACH + 1)
    count = np.zeros(d.shape, np.float64)
    for w, dil in A_PATTERNS:
        count += (d % dil == 0) & (np.abs(d) <= w // 2)
    log_count = np.where(count > 0, np.log(np.maximum(count, 1.0)), 0.0).astype(np.float32)
    table = rel_bias.astype(jnp.float32)
    onehot_a = jnp.asarray(np.eye(NUM_BUCKETS, dtype=np.float32)[_t5_bucket_np(d)])
    vec_a = (jnp.dot(onehot_a, table[:, :A_HEADS], precision=hi).T + log_count[None]) * LOG2E
    vec_a = jnp.where(jnp.asarray(count > 0)[None], vec_a, MASKED)
    tiles_a = _toeplitz_tiles(vec_a, SUB, (-(-A_REACH // A_TILE) + 1) * (A_TILE // SUB) - 1)
    db = np.arange(-B_HALF_WINDOW, B_HALF_WINDOW + 1)
    onehot_b = jnp.asarray(np.eye(NUM_BUCKETS, dtype=np.float32)[_t5_bucket_np(db)])
    vec_b = jnp.dot(onehot_b, table[:, A_HEADS:], precision=hi).T * LOG2E
    tiles_b = _toeplitz_tiles(vec_b, SUB, (-(-B_HALF_WINDOW // B_TILE) + 1) * (B_TILE // SUB) - 1)
    top_a = jnp.max(jnp.abs(jnp.where(jnp.asarray(count > 0)[None], vec_a, 0.0)))
    return tiles_a, tiles_b, top_a, jnp.max(jnp.abs(vec_b))


def kernel(x, c, w_mod, b_mod, pre_norm, post_norm, w_in, w_out, q_norm, k_norm, sink, rel_bias):
    bsz, s, d = x.shape
    assert bsz == 1 and c.shape[0] == 1
    depth = w_in.shape[0]
    assert w_in.shape[2] == IN_WIDTH and w_out.shape[1] == A_W + B_W + C_W
    x2 = x[0]
    cos, sin = _rope_tables(s)
    tiles_a, tiles_b, top_bias_a, top_bias_b = _bias_tiles(rel_bias)
    mod = _modulation(c, w_mod, b_mod)[:, 0]
    shift, scale, gate = mod[:, :d], mod[:, d:2 * d], mod[:, 2 * d:]
    vec = lambda a, i: a[i][None, :]

    h = _prenorm(x2, vec(pre_norm, 0), vec(scale, 0), vec(shift, 0))
    for i in range(depth):
        proj = _in_projection(h, w_in, i)
        vta, vtb, vtc, kc, sq_norms = _prep(proj, cos, sin, vec(k_norm, i))
        sq_norms = jnp.max(sq_norms, axis=(0, 2))
        qk_scale = NORM_SLACK * HEAD_DIM ** -0.5 * LOG2E
        bound_a = qk_scale * jnp.sqrt(sq_norms[0] * sq_norms[1]) + top_bias_a
        bound_b = jnp.maximum(qk_scale * jnp.sqrt(sq_norms[2] * sq_norms[3]) + top_bias_b,
                              jnp.max(jnp.abs(sink[i])) * LOG2E)
        band_a = functools.partial(_band_attention, proj, AK, vta, AQ, AG, A_HEADS, 1, A_TILE,
                                   -(-A_REACH // A_TILE), tiles_a, None)
        band_b = functools.partial(_band_attention, proj, BK, vtb, BQ, BG, B_KV_HEADS, B_HEADS // B_KV_HEADS,
                                   B_TILE, -(-B_HALF_WINDOW // B_TILE), tiles_b, sink[i])
        oa = lax.cond(bound_a <= LOGIT_BOUND, lambda: band_a(True), lambda: band_a(False))
        ob = lax.cond(bound_b <= LOGIT_BOUND, lambda: band_b(True), lambda: band_b(False))
        dense = functools.partial(_dense_attention, proj, kc, vtc, CQ, CG, C_KV_HEADS,
                                  C_HEADS // C_KV_HEADS, C_TQ, C_TK, cos, sin, vec(q_norm, i))
        logit_bound = (HEAD_DIM ** 0.5 * LOG2E) * jnp.max(jnp.abs(q_norm[i])) * jnp.max(jnp.abs(k_norm[i]))
        oc = lax.cond(logit_bound <= LOGIT_BOUND, lambda: dense(True), lambda: dense(False))
        y = _out_projection(oa, ob, oc, w_out, i)
        nxt = None if i == depth - 1 else (vec(pre_norm, i + 1), vec(scale, i + 1), vec(shift, i + 1))
        x2, h = _post(x2, y, vec(post_norm, i), vec(gate, i), nxt)
    return x2[None]
```

```python
import functools
import math

import numpy as np
import jax
import jax.numpy as jnp
from jax import lax
from jax.experimental import pallas as pl
from jax.experimental.pallas import tpu as pltpu

HEAD_DIM = 128
GRID_W = 64
A_HEADS = 12
A_PATTERNS = ((128, 1), (512, 4), (2048, 16))
B_HEADS = 8
B_KV_HEADS = 2
B_HALF_WINDOW = 128
C_HEADS = 12
C_KV_HEADS = 4
ROPE_THETA = 10000.0
NUM_BUCKETS = 32
MAX_DISTANCE = 1024
EPS = 1e-6
LOG2E = math.log2(math.e)
MASKED = -1e30
M_INIT = -1e29

A_W = A_HEADS * HEAD_DIM
B_W = B_HEADS * HEAD_DIM
C_W = C_HEADS * HEAD_DIM
AQ, AK, AV, AG = 0, 12, 24, 36
BQ, BK, BV, BG = 48, 56, 58, 60
CQ, CK, CV, CG = 68, 80, 84, 88
IN_WIDTH = 100 * HEAD_DIM

A_REACH = max(w // 2 for w, _ in A_PATTERNS)
A_TILE = 512
B_TILE = 256
C_TQ = {True: 512, False: 256}
C_TK = 1024
BAND_HEADS_PER_STEP = 2
BOUNDED_GROUP = 4
NORM_SLACK = 1.02
LOGIT_BOUND = 64.0
DENSE_UNROLL = 3
VT_ROWS = HEAD_DIM + 16
V7X_VMEM_LIMIT = 56 * 1024 * 1024
SUB = 128


def _cparams(n_axes):
    return pltpu.CompilerParams(dimension_semantics=("arbitrary",) * n_axes,
                                vmem_limit_bytes=V7X_VMEM_LIMIT)


def _tile(n, pref):
    t = min(n, pref)
    while n % t:
        t //= 2
    return t


def _mod_kernel(c_ref, w_ref, b_ref, o_ref):
    c = c_ref[...]
    cs = jnp.broadcast_to(c * jax.nn.sigmoid(c), (8, c.shape[1]))
    r = jnp.dot(cs, w_ref[0], preferred_element_type=jnp.float32)
    o_ref[0] = r[0:1] + b_ref[0]


def _modulation(c, w_mod, b_mod):
    depth, d, n = w_mod.shape
    tn = _tile(n, 1024)
    out = pl.pallas_call(
        _mod_kernel,
        grid=(depth, n // tn),
        in_specs=[pl.BlockSpec((1, d), lambda i, j: (0, 0)),
                  pl.BlockSpec((1, d, tn), lambda i, j: (i, 0, j)),
                  pl.BlockSpec((1, 1, tn), lambda i, j: (i, 0, j))],
        out_specs=pl.BlockSpec((1, 1, tn), lambda i, j: (i, 0, j)),
        out_shape=jax.ShapeDtypeStruct((depth, 1, n), jnp.float32),
        compiler_params=_cparams(2),
        name="modulation",
    )(c, w_mod, b_mod.reshape(depth, 1, n))
    return out


def _rms(x):
    return x * lax.rsqrt(jnp.mean(x * x, axis=-1, keepdims=True) + EPS)


def _prenorm_kernel(x_ref, g_ref, sc_ref, sh_ref, h_ref):
    y = _rms(x_ref[...]) * g_ref[...]
    h_ref[...] = (y * (1.0 + sc_ref[...]) + sh_ref[...]).astype(h_ref.dtype)


def _prenorm(x, g, scale, shift):
    s, d = x.shape
    tm = _tile(s, 512)
    vec = pl.BlockSpec((1, d), lambda i: (0, 0))
    return pl.pallas_call(
        _prenorm_kernel,
        grid=(s // tm,),
        in_specs=[pl.BlockSpec((tm, d), lambda i: (i, 0)), vec, vec, vec],
        out_specs=pl.BlockSpec((tm, d), lambda i: (i, 0)),
        out_shape=jax.ShapeDtypeStruct((s, d), jnp.bfloat16),
        compiler_params=_cparams(1),
        name="prenorm",
    )(x, g, scale, shift)


def _post_kernel(x_ref, y_ref, pg_ref, gate_ref, xo_ref):
    yn = _rms(y_ref[...].astype(jnp.float32)) * pg_ref[...]
    xo_ref[...] = x_ref[...] + gate_ref[...] * yn


def _postpre_kernel(x_ref, y_ref, pg_ref, gate_ref, g_ref, sc_ref, sh_ref, xo_ref, h_ref):
    yn = _rms(y_ref[...].astype(jnp.float32)) * pg_ref[...]
    xn = x_ref[...] + gate_ref[...] * yn
    xo_ref[...] = xn
    h = _rms(xn) * g_ref[...]
    h_ref[...] = (h * (1.0 + sc_ref[...]) + sh_ref[...]).astype(h_ref.dtype)


def _post(x, y, post_g, gate, nxt=None):
    s, d = x.shape
    tm = _tile(s, 256)
    row = pl.BlockSpec((tm, d), lambda i: (i, 0))
    vec = pl.BlockSpec((1, d), lambda i: (0, 0))
    if nxt is None:
        return pl.pallas_call(
            _post_kernel, grid=(s // tm,),
            in_specs=[row, row, vec, vec], out_specs=row,
            out_shape=jax.ShapeDtypeStruct((s, d), jnp.float32),
            compiler_params=_cparams(1), name="postnorm",
        )(x, y, post_g, gate), None
    g, scale, shift = nxt
    return pl.pallas_call(
        _postpre_kernel, grid=(s // tm,),
        in_specs=[row, row, vec, vec, vec, vec, vec], out_specs=[row, row],
        out_shape=[jax.ShapeDtypeStruct((s, d), jnp.float32),
                   jax.ShapeDtypeStruct((s, d), jnp.bfloat16)],
        compiler_params=_cparams(1), name="postnorm_prenorm",
    )(x, y, post_g, gate, g, scale, shift)


def _mm_kernel(a_ref, b_ref, o_ref):
    o_ref[...] = jnp.dot(a_ref[...], b_ref[0].astype(a_ref.dtype),
                         preferred_element_type=jnp.float32).astype(o_ref.dtype)


def _in_projection(h, w, layer):
    m, k = h.shape
    n = w.shape[2]
    tm, tn = _tile(m, 1024), _tile(n, 512)
    return pl.pallas_call(
        _mm_kernel,
        grid=(m // tm, n // tn),
        in_specs=[pl.BlockSpec((tm, k), lambda i, j: (i, 0)),
                  pl.BlockSpec((1, k, tn), lambda i, j: (layer, 0, j))],
        out_specs=pl.BlockSpec((tm, tn), lambda i, j: (i, j)),
        out_shape=jax.ShapeDtypeStruct((m, n), jnp.bfloat16),
        compiler_params=_cparams(2),
        name="in_projection",
    )(h, w)


def _mm3_kernel(a1, a2, a3, w_ref, o_ref):
    acc, row = None, 0
    for a in (a1, a2, a3):
        w = w_ref[0, row:row + a.shape[1]].astype(a.dtype)
        part = jnp.dot(a[...], w, preferred_element_type=jnp.float32)
        acc = part if acc is None else acc + part
        row += a.shape[1]
    o_ref[...] = acc.astype(o_ref.dtype)


def _out_projection(oa, ob, oc, w, layer):
    m = oa.shape[0]
    k, n = w.shape[1:]
    tm, tn = _tile(m, 1024), _tile(n, 512)
    a_spec = lambda a: pl.BlockSpec((tm, a.shape[1]), lambda i, j: (i, 0))
    return pl.pallas_call(
        _mm3_kernel,
        grid=(m // tm, n // tn),
        in_specs=[a_spec(oa), a_spec(ob), a_spec(oc),
                  pl.BlockSpec((1, k, tn), lambda i, j: (layer, 0, j))],
        out_specs=pl.BlockSpec((tm, tn), lambda i, j: (i, j)),
        out_shape=jax.ShapeDtypeStruct((m, n), jnp.bfloat16),
        compiler_params=_cparams(2),
        name="out_projection",
    )(oa, ob, oc, w)


def _rope_tables(s):
    rows = s // GRID_W
    row = jnp.repeat(jnp.arange(rows), GRID_W)
    col = jnp.tile(jnp.arange(GRID_W), rows)
    quarter = HEAD_DIM // 4
    freqs = ROPE_THETA ** (-jnp.arange(0, 2 * quarter, 2, dtype=jnp.float32) / (2 * quarter))
    ar = row.astype(jnp.float32)[:, None] * freqs[None, :]
    ac = col.astype(jnp.float32)[:, None] * freqs[None, :]
    cr, sr, cc, sc = jnp.cos(ar), jnp.sin(ar), jnp.cos(ac), jnp.sin(ac)
    cos = jnp.concatenate([cr, cr, cc, cc], axis=-1)
    sin = jnp.concatenate([-sr, sr, -sc, sc], axis=-1)
    return cos, sin


def _rope(x, cos, sin):
    quarter = HEAD_DIM // 4
    fwd = pltpu.roll(x, HEAD_DIM - quarter, 1)
    bwd = pltpu.roll(x, quarter, 1)
    lane = lax.broadcasted_iota(jnp.int32, x.shape, 1)
    partner = jnp.where((lane % (2 * quarter)) < quarter, fwd, bwd)
    return x * cos + partner * sin


def _prep_kernel(av_ref, bv_ref, cv_ref, ck_ref, aq_ref, ak_ref, bq_ref, bk_ref, cos_ref, sin_ref, kg_ref,
                 vta_ref, vtb_ref, vtc_ref, kc_ref, nrm_ref, *, ts):
    def put_t(src_ref, n_heads, dst_ref, chunk):
        for hh in range(n_heads):
            vt = src_ref[:, hh * HEAD_DIM:(hh + 1) * HEAD_DIM].astype(jnp.float32).T
            for cidx in range(ts // chunk):
                dst_ref[hh, cidx, :HEAD_DIM] = vt[:, cidx * chunk:(cidx + 1) * chunk].astype(dst_ref.dtype)
                dst_ref[hh, cidx, HEAD_DIM:] = jnp.ones((VT_ROWS - HEAD_DIM, chunk), dst_ref.dtype)

    put_t(av_ref, A_HEADS, vta_ref, A_TILE)
    put_t(bv_ref, B_KV_HEADS, vtb_ref, B_TILE)
    put_t(cv_ref, C_KV_HEADS, vtc_ref, C_TK)
    cos, sin, kg = cos_ref[...], sin_ref[...], kg_ref[...]
    for hh in range(C_KV_HEADS):
        k = ck_ref[:, hh * HEAD_DIM:(hh + 1) * HEAD_DIM].astype(jnp.float32)
        k = _rope(_rms(k) * kg, cos, sin)
        kc_ref[:, hh * HEAD_DIM:(hh + 1) * HEAD_DIM] = k.astype(kc_ref.dtype)

    def max_sq_norm(src_ref, n_heads):
        ones = jnp.ones((HEAD_DIM, HEAD_DIM), src_ref.dtype)
        best = None
        for hh in range(n_heads):
            x = src_ref[:, hh * HEAD_DIM:(hh + 1) * HEAD_DIM]
            row_sums = jnp.dot(x * x, ones, preferred_element_type=jnp.float32)
            top = jnp.max(row_sums, axis=0, keepdims=True)
            best = top if best is None else jnp.maximum(best, top)
        return best

    nrm_ref[0] = jnp.concatenate(
        [max_sq_norm(aq_ref, A_HEADS), max_sq_norm(ak_ref, A_HEADS), max_sq_norm(bq_ref, B_HEADS),
         max_sq_norm(bk_ref, B_KV_HEADS), jnp.zeros((4, HEAD_DIM), jnp.float32)], axis=0)


def _prep(proj, cos, sin, k_gain):
    s = proj.shape[0]
    ts = max(A_TILE, B_TILE, C_TK)
    assert s % ts == 0
    bf = jnp.bfloat16
    n = s // ts
    vt_spec = lambda heads, t: pl.BlockSpec((heads, ts // t, VT_ROWS, t), lambda i: (0, i, 0, 0))
    vt_shape = lambda heads, t: jax.ShapeDtypeStruct((heads, s // t, VT_ROWS, t), bf)
    return pl.pallas_call(
        functools.partial(_prep_kernel, ts=ts),
        grid=(n,),
        in_specs=[pl.BlockSpec((ts, A_W), lambda i: (i, AV * HEAD_DIM // A_W)),
                  pl.BlockSpec((ts, B_KV_HEADS * HEAD_DIM), lambda i: (i, BV // B_KV_HEADS)),
                  pl.BlockSpec((ts, C_KV_HEADS * HEAD_DIM), lambda i: (i, CV // C_KV_HEADS)),
                  pl.BlockSpec((ts, C_KV_HEADS * HEAD_DIM), lambda i: (i, CK // C_KV_HEADS)),
                  pl.BlockSpec((ts, A_W), lambda i: (i, AQ * HEAD_DIM // A_W)),
                  pl.BlockSpec((ts, A_W), lambda i: (i, AK * HEAD_DIM // A_W)),
                  pl.BlockSpec((ts, B_W), lambda i: (i, BQ * HEAD_DIM // B_W)),
                  pl.BlockSpec((ts, B_KV_HEADS * HEAD_DIM), lambda i: (i, BK // B_KV_HEADS)),
                  pl.BlockSpec((ts, HEAD_DIM), lambda i: (i, 0)),
                  pl.BlockSpec((ts, HEAD_DIM), lambda i: (i, 0)),
                  pl.BlockSpec((1, HEAD_DIM), lambda i: (0, 0))],
        out_specs=[vt_spec(A_HEADS, A_TILE), vt_spec(B_KV_HEADS, B_TILE), vt_spec(C_KV_HEADS, C_TK),
                   pl.BlockSpec((ts, C_KV_HEADS * HEAD_DIM), lambda i: (i, 0)),
                   pl.BlockSpec((1, 8, HEAD_DIM), lambda i: (i, 0, 0))],
        out_shape=[vt_shape(A_HEADS, A_TILE), vt_shape(B_KV_HEADS, B_TILE), vt_shape(C_KV_HEADS, C_TK),
                   jax.ShapeDtypeStruct((s, C_KV_HEADS * HEAD_DIM), bf),
                   jax.ShapeDtypeStruct((n, 8, HEAD_DIM), jnp.float32)],
        compiler_params=_cparams(1),
        name="vt_krope_prep",
    )(proj, proj, proj, proj, proj, proj, proj, proj, cos, sin, k_gain)


def _scores(k_ref, head, row0, tk, qt_ref, s_ref, mx_ref, slot, bias=None):
    kt = k_ref[pl.ds(pl.multiple_of(row0, tk), tk), head * HEAD_DIM:(head + 1) * HEAD_DIM]
    s = jnp.dot(kt, qt_ref[...], preferred_element_type=jnp.float32)
    if bias is not None:
        s = s + bias
    s_ref[slot] = s
    mx_ref[slot] = jnp.max(s, axis=0, keepdims=True)


def _accumulate(vt, s_ref, mx_ref, m_ref, acc_ref, slot):
    m_old = m_ref[...]
    m_new = jnp.maximum(m_old, mx_ref[slot])
    alpha = jnp.exp2(m_old - m_new)
    p = jnp.exp2(s_ref[slot] - m_new).astype(jnp.bfloat16)
    acc_ref[...] = alpha * acc_ref[...] + jnp.dot(vt, p, preferred_element_type=jnp.float32)
    m_ref[...] = m_new


def _load_q(q_refs, qt_ref, tq, prep=lambda q: q):
    scale2 = HEAD_DIM ** -0.5 * LOG2E
    for g, q_ref in enumerate(q_refs):
        q = prep(q_ref[...].astype(jnp.float32))
        qt_ref[:, g * tq:(g + 1) * tq] = (q * scale2).T.astype(qt_ref.dtype)


def _store_gated(acc, g_refs, o_ref, tq, first=0, extra_denominator=None):
    for g, g_ref in enumerate(g_refs):
        sl = slice(g * tq, (g + 1) * tq)
        den = acc[HEAD_DIM:HEAD_DIM + 1, sl]
        if extra_denominator is not None:
            den = den + extra_denominator[g]
        o = (acc[:HEAD_DIM, sl] / den).T
        gate = g_ref[...].astype(jnp.float32)
        cols = slice((first + g) * HEAD_DIM, (first + g + 1) * HEAD_DIM)
        o_ref[:, cols] = (o * (gate * jax.nn.sigmoid(gate))).astype(o_ref.dtype)


def _band_kernel(*refs, heads, group, t, nkv, reach, has_sink, bounded):
    it = iter(refs)
    q_refs = [[next(it) for _ in range(group)] for _ in range(heads)]
    g_refs = [[next(it) for _ in range(group)] for _ in range(heads)]
    k_ref, vt_ref, bias_ref = next(it), next(it), next(it)
    sink_ref = next(it) if has_sink else None
    o_ref = next(it)
    qt_ref, s_ref, mx_ref, m_ref, acc_ref = (next(it) for _ in range(5))
    h = pl.program_id(0)
    qb = pl.program_id(1)
    n_off = 2 * reach + 1

    nsb = t // SUB
    n_sub = bias_ref.shape[1] - 1

    def block(o):
        kb = qb + (o - reach)
        return jnp.clip(kb, 0, nkv - 1), jnp.logical_and(kb >= 0, kb < nkv)

    def bias_tile(hh, o, valid):
        rows = []
        for a in range(nsb):
            cols = []
            for g in range(group):
                for b in range(nsb):
                    sub = (o - reach) * nsb + a - b + (n_sub - 1) // 2
                    cols.append(bias_ref[hh * group + g, jnp.where(valid, sub, n_sub)])
            rows.append(jnp.concatenate(cols, axis=1))
        return jnp.concatenate(rows, axis=0)

    if bounded:
        for hh in range(heads):
            _load_q(q_refs[hh], qt_ref.at[hh], t)
        width = min(n_off, nkv)
        start = jnp.clip(qb - reach, 0, nkv - width)

        def bias_window(hh):
            rows = []
            for a in range(width * nsb):
                cols = []
                for g in range(group):
                    for b in range(nsb):
                        sub = (start - qb) * nsb + a - b + (n_sub - 1) // 2
                        inside = jnp.logical_and(sub >= 0, sub < n_sub)
                        cols.append(bias_ref[hh * group + g, jnp.where(inside, sub, n_sub)])
                rows.append(jnp.concatenate(cols, axis=1))
            return jnp.concatenate(rows, axis=0)

        for hh in range(heads):
            keys = k_ref[pl.ds(pl.multiple_of(start * t, t), width * t), hh * HEAD_DIM:(hh + 1) * HEAD_DIM]
            s = jnp.dot(keys, qt_ref[hh], preferred_element_type=jnp.float32) + bias_window(hh)
            p = jnp.exp2(s).astype(jnp.bfloat16)
            total = None
            for o in range(width):
                part = jnp.dot(vt_ref[hh, start + o], p[o * t:(o + 1) * t],
                               preferred_element_type=jnp.float32)
                total = part if total is None else total + part
            sinks = None
            if has_sink:
                sinks = [jnp.exp2(jnp.full((1, t), sink_ref[(h * heads + hh) * group + g] * LOG2E, jnp.float32))
                         for g in range(group)]
            _store_gated(total, g_refs[hh], o_ref, t, hh * group, sinks)
        return

    for hh in range(heads):
        _load_q(q_refs[hh], qt_ref.at[hh], t)
        acc_ref[hh, :HEAD_DIM] = jnp.zeros((HEAD_DIM, group * t), jnp.float32)
        if has_sink:
            for g in range(group):
                sink = sink_ref[(h * heads + hh) * group + g] * LOG2E
                m_ref[hh, :, g * t:(g + 1) * t] = jnp.full((1, t), sink, jnp.float32)
            acc_ref[hh, HEAD_DIM:] = jnp.ones((VT_ROWS - HEAD_DIM, group * t), jnp.float32)
        else:
            m_ref[hh] = jnp.full((1, group * t), M_INIT, jnp.float32)
            acc_ref[hh, HEAD_DIM:] = jnp.zeros((VT_ROWS - HEAD_DIM, group * t), jnp.float32)

    units = [(hh, o) for hh in range(heads) for o in range(n_off)]

    def scores(u):
        hh, o = units[u]
        kb, valid = block(o)
        _scores(k_ref, hh, kb * t, t, qt_ref.at[hh], s_ref, mx_ref, u % 2, bias_tile(hh, o, valid))

    scores(0)
    for u, (hh, o) in enumerate(units):
        if u + 1 < len(units):
            scores(u + 1)
        _accumulate(vt_ref[hh, block(o)[0]], s_ref, mx_ref, m_ref.at[hh], acc_ref.at[hh], u % 2)
        if o == n_off - 1:
            _store_gated(acc_ref.at[hh], g_refs[hh], o_ref, t, hh * group)


def _dense_kernel(*refs, group, tq, tk, nkv, bounded):
    it = iter(refs)
    q_refs = [next(it) for _ in range(group)]
    g_refs = [next(it) for _ in range(group)]
    k_ref, vt_ref, cos_ref, sin_ref, qg_ref, o_ref = (next(it) for _ in range(6))
    qt_ref, s_ref, mx_ref, m_ref, acc_ref = (next(it) for _ in range(5))
    qt_ref, m_ref, acc_ref = qt_ref.at[0], m_ref.at[0], acc_ref.at[0]

    _load_q(q_refs, qt_ref, tq, lambda q: _rope(_rms(q) * qg_ref[...], cos_ref[...], sin_ref[...]))
    acc_ref[...] = jnp.zeros(acc_ref.shape, jnp.float32)

    if bounded:
        def block(jblk):
            kt = k_ref[pl.ds(pl.multiple_of(jblk * tk, tk), tk), :]
            s = jnp.dot(kt, qt_ref[...], preferred_element_type=jnp.float32)
            return jnp.dot(vt_ref[0, jblk], jnp.exp2(s).astype(jnp.bfloat16),
                           preferred_element_type=jnp.float32)

        per_iter = math.gcd(nkv, BOUNDED_GROUP)

        def blocks(it, carry):
            part = block(it * per_iter)
            for u in range(1, per_iter):
                part = part + block(it * per_iter + u)
            acc_ref[...] += part
            return carry

        lax.fori_loop(0, nkv // per_iter, blocks, 0)
        _store_gated(acc_ref, g_refs, o_ref, tq)
        return

    m_ref[...] = jnp.full(m_ref.shape, M_INIT, jnp.float32)

    def scores(jblk, slot):
        _scores(k_ref, 0, jblk * tk, tk, qt_ref, s_ref, mx_ref, slot)

    def accumulate(jblk, slot):
        _accumulate(vt_ref[0, jblk], s_ref, mx_ref, m_ref, acc_ref, slot)

    scores(0, 0)

    def pair(j0):
        scores(j0 + 1, 1)
        accumulate(j0, 0)
        scores(j0 + 2, 0)
        accumulate(j0 + 1, 1)

    def pairs(it, carry):
        for u in range(DENSE_UNROLL):
            pair(2 * (it * DENSE_UNROLL + u))
        return carry

    n_pairs = nkv // 2 - 1
    lax.fori_loop(0, n_pairs // DENSE_UNROLL, pairs, 0)
    for jj in range(n_pairs - n_pairs % DENSE_UNROLL, n_pairs):
        pair(2 * jj)
    scores(nkv - 1, 1)
    accumulate(nkv - 2, 0)
    accumulate(nkv - 1, 1)
    _store_gated(acc_ref, g_refs, o_ref, tq)


def _attention(kern, proj, k_src, k_col, vt, q_col, g_col, n_kv_heads, group, tq, tk, extra_specs,
               extra_args, name, heads=1, bounded=False):
    s = proj.shape[0]
    parked = 8 if bounded else tk
    assert s % tq == 0 and s % tk == 0 and n_kv_heads % heads == 0 and k_col % heads == 0
    in_specs, args = [], []
    for col in (q_col, g_col):
        for hh in range(heads):
            for g in range(group):
                in_specs.append(pl.BlockSpec(
                    (tq, HEAD_DIM), lambda h, qb, col=col, hh=hh, g=g: (qb, col + (h * heads + hh) * group + g)))
                args.append(proj)
    in_specs += [pl.BlockSpec((s, heads * HEAD_DIM), lambda h, qb: (0, k_col // heads + h)),
                 pl.BlockSpec((heads, s // tk, VT_ROWS, tk), lambda h, qb: (h, 0, 0, 0))] + extra_specs
    args += [k_src, vt] + extra_args
    n = group * tq
    return pl.pallas_call(
        kern,
        grid=(n_kv_heads // heads, s // tq),
        in_specs=in_specs,
        out_specs=pl.BlockSpec((tq, heads * group * HEAD_DIM), lambda h, qb: (qb, h)),
        out_shape=jax.ShapeDtypeStruct((s, n_kv_heads * group * HEAD_DIM), jnp.bfloat16),
        scratch_shapes=[pltpu.VMEM((heads, HEAD_DIM, n), jnp.bfloat16),
                        pltpu.VMEM((2, parked, n), jnp.float32),
                        pltpu.VMEM((2, 1, n), jnp.float32),
                        pltpu.VMEM((heads, 1, n), jnp.float32),
                        pltpu.VMEM((heads, VT_ROWS, n), jnp.float32)],
        compiler_params=_cparams(2),
        name=name,
    )(*args)


def _band_attention(proj, k_col, vt, q_col, g_col, n_kv_heads, group, t, reach, bias, sink, bounded,
                    heads=BAND_HEADS_PER_STEP):
    nkv = proj.shape[0] // t
    extra_specs = [pl.BlockSpec((heads * group,) + bias.shape[1:], lambda h, qb: (h, 0, 0, 0))]
    extra_args = [bias]
    if sink is not None:
        extra_specs.append(pl.BlockSpec(memory_space=pltpu.SMEM))
        extra_args.append(sink)
    kern = functools.partial(_band_kernel, heads=heads, group=group, t=t, nkv=nkv, reach=reach,
                             has_sink=sink is not None, bounded=bounded)
    name = "band_attention_g%d%s" % (group, "_bounded" if bounded else "")
    return _attention(kern, proj, proj, k_col, vt, q_col, g_col, n_kv_heads, group, t, t,
                      extra_specs, extra_args, name, heads, bounded)


def _dense_attention(proj, kc, vt, q_col, g_col, n_kv_heads, group, tk, cos, sin, q_gain, bounded):
    nkv = proj.shape[0] // tk
    tq = C_TQ[bounded]
    assert nkv % 2 == 0
    extra_specs = [pl.BlockSpec((tq, HEAD_DIM), lambda h, qb: (qb, 0)),
                   pl.BlockSpec((tq, HEAD_DIM), lambda h, qb: (qb, 0)),
                   pl.BlockSpec((1, HEAD_DIM), lambda h, qb: (0, 0))]
    kern = functools.partial(_dense_kernel, group=group, tq=tq, tk=tk, nkv=nkv, bounded=bounded)
    name = "dense_attention_g%d%s" % (group, "_bounded" if bounded else "")
    return _attention(kern, proj, kc, 0, vt, q_col, g_col, n_kv_heads, group, tq, tk,
                      extra_specs, [cos, sin, q_gain], name, 1, bounded)


def _t5_bucket_np(rel):
    half_b = NUM_BUCKETS // 2
    max_exact = half_b // 2
    ret = np.where(rel > 0, half_b, 0)
    n = np.abs(rel)
    nf = np.maximum(n, 1).astype(np.float32)
    large = max_exact + (np.log(nf / np.float32(max_exact)) / np.float32(math.log(MAX_DISTANCE / max_exact))
                         * np.float32(half_b - max_exact)).astype(np.int32)
    large = np.minimum(large, half_b - 1)
    return (ret + np.where(n < max_exact, n, large)).astype(np.int32)


def _toeplitz_tiles(vec, tile, reach):
    heads = vec.shape[0]
    r = (vec.shape[1] - 1) // 2
    n_off = 2 * reach + 1
    span = 2 * tile - 1
    lo = -reach * tile - (tile - 1)
    hi = reach * tile + (tile - 1)
    padded = jnp.pad(vec, ((0, 0), (-lo - r, hi - r)), constant_values=MASKED)
    rows = [padded[:, o * tile:o * tile + span] for o in range(n_off)]
    u = jnp.stack(rows, axis=1)[:, :, ::-1]
    w = jnp.pad(u, ((0, 0), (0, 0), (0, 1)))
    flat = jnp.broadcast_to(w[:, :, None, :], (heads, n_off, tile, span + 1))
    flat = flat.reshape(heads, n_off, tile * (span + 1))[:, :, :tile * span]
    skew = flat.reshape(heads, n_off, tile, span)
    tiles = skew[:, :, :, tile - 1:]
    masked = jnp.full((tiles.shape[0], 1) + tiles.shape[2:], MASKED, tiles.dtype)
    return jnp.concatenate([tiles, masked], axis=1)


def _bias_tiles(rel_bias):
    hi = lax.Precision.HIGHEST
    d = np.arange(-A_REACH, A_REACH + 1)
    count = np.zeros(d.shape, np.float64)
    for w, dil in A_PATTERNS:
        count += (d % dil == 0) & (np.abs(d) <= w // 2)
    log_count = np.where(count > 0, np.log(np.maximum(count, 1.0)), 0.0).astype(np.float32)
    table = rel_bias.astype(jnp.float32)
    onehot_a = jnp.asarray(np.eye(NUM_BUCKETS, dtype=np.float32)[_t5_bucket_np(d)])
    vec_a = (jnp.dot(onehot_a, table[:, :A_HEADS], precision=hi).T + log_count[None]) * LOG2E
    vec_a = jnp.where(jnp.asarray(count > 0)[None], vec_a, MASKED)
    tiles_a = _toeplitz_tiles(vec_a, SUB, (-(-A_REACH // A_TILE) + 1) * (A_TILE // SUB) - 1)
    db = np.arange(-B_HALF_WINDOW, B_HALF_WINDOW + 1)
    onehot_b = jnp.asarray(np.eye(NUM_BUCKETS, dtype=np.float32)[_t5_bucket_np(db)])
    vec_b = jnp.dot(onehot_b, table[:, A_HEADS:], precision=hi).T * LOG2E
    tiles_b = _toeplitz_tiles(vec_b, SUB, (-(-B_HALF_WINDOW // B_TILE) + 1) * (B_TILE // SUB) - 1)
    top_a = jnp.max(jnp.abs(jnp.where(jnp.asarray(count > 0)[None], vec_a, 0.0)))
    return tiles_a, tiles_b, top_a, jnp.max(jnp.abs(vec_b))


def kernel(x, c, w_mod, b_mod, pre_norm, post_norm, w_in, w_out, q_norm, k_norm, sink, rel_bias):
    bsz, s, d = x.shape
    assert bsz == 1 and c.shape[0] == 1
    depth = w_in.shape[0]
    assert w_in.shape[2] == IN_WIDTH and w_out.shape[1] == A_W + B_W + C_W
    x2 = x[0]
    cos, sin = _rope_tables(s)
    tiles_a, tiles_b, top_bias_a, top_bias_b = _bias_tiles(rel_bias)
    mod = _modulation(c, w_mod, b_mod)[:, 0]
    shift, scale, gate = mod[:, :d], mod[:, d:2 * d], mod[:, 2 * d:]
    vec = lambda a, i: a[i][None, :]

    h = _prenorm(x2, vec(pre_norm, 0), vec(scale, 0), vec(shift, 0))
    for i in range(depth):
        proj = _in_projection(h, w_in, i)
        vta, vtb, vtc, kc, sq_norms = _prep(proj, cos, sin, vec(k_norm, i))
        sq_norms = jnp.max(sq_norms, axis=(0, 2))
        qk_scale = NORM_SLACK * HEAD_DIM ** -0.5 * LOG2E
        bound_a = qk_scale * jnp.sqrt(sq_norms[0] * sq_norms[1]) + top_bias_a
        bound_b = jnp.maximum(qk_scale * jnp.sqrt(sq_norms[2] * sq_norms[3]) + top_bias_b,
                              jnp.max(jnp.abs(sink[i])) * LOG2E)
        band_a = functools.partial(_band_attention, proj, AK, vta, AQ, AG, A_HEADS, 1, A_TILE,
                                   -(-A_REACH // A_TILE), tiles_a, None)
        band_b = functools.partial(_band_attention, proj, BK, vtb, BQ, BG, B_KV_HEADS, B_HEADS // B_KV_HEADS,
                                   B_TILE, -(-B_HALF_WINDOW // B_TILE), tiles_b, sink[i])
        oa = lax.cond(bound_a <= LOGIT_BOUND, lambda: band_a(True), lambda: band_a(False))
        ob = lax.cond(bound_b <= LOGIT_BOUND, lambda: band_b(True), lambda: band_b(False))
        dense = functools.partial(_dense_attention, proj, kc, vtc, CQ, CG, C_KV_HEADS,
                                  C_HEADS // C_KV_HEADS, C_TK, cos, sin, vec(q_norm, i))
        logit_bound = (HEAD_DIM ** 0.5 * LOG2E) * jnp.max(jnp.abs(q_norm[i])) * jnp.max(jnp.abs(k_norm[i]))
        oc = lax.cond(logit_bound <= LOGIT_BOUND, lambda: dense(True), lambda: dense(False))
        y = _out_projection(oa, ob, oc, w_out, i)
        nxt = None if i == depth - 1 else (vec(pre_norm, i + 1), vec(scale, i + 1), vec(shift, i + 1))
        x2, h = _post(x2, y, vec(post_norm, i), vec(gate, i), nxt)
    return x2[None]
```

```python
import functools
import math

import numpy as np
import jax
import jax.numpy as jnp
from jax import lax
from jax.experimental import pallas as pl
from jax.experimental.pallas import tpu as pltpu

HEAD_DIM = 128
GRID_W = 64
A_HEADS = 12
A_PATTERNS = ((128, 1), (512, 4), (2048, 16))
B_HEADS = 8
B_KV_HEADS = 2
B_HALF_WINDOW = 128
C_HEADS = 12
C_KV_HEADS = 4
ROPE_THETA = 10000.0
NUM_BUCKETS = 32
MAX_DISTANCE = 1024
EPS = 1e-6
LOG2E = math.log2(math.e)
MASKED = -1e30
M_INIT = -1e29

A_W = A_HEADS * HEAD_DIM
B_W = B_HEADS * HEAD_DIM
C_W = C_HEADS * HEAD_DIM
AQ, AK, AV, AG = 0, 12, 24, 36
BQ, BK, BV, BG = 48, 56, 58, 60
CQ, CK, CV, CG = 68, 80, 84, 88
IN_WIDTH = 100 * HEAD_DIM

A_REACH = max(w // 2 for w, _ in A_PATTERNS)
A_TILE = 512
B_TILE = 256
C_TQ = {True: 512, False: 256}
C_TK = 1024
BAND_HEADS_PER_STEP = 2
BOUNDED_GROUP = 4
NORM_SLACK = 1.02
LOGIT_BOUND = 64.0
DENSE_UNROLL = 3
VT_ROWS = HEAD_DIM + 16
V7X_VMEM_LIMIT = 56 * 1024 * 1024
SUB = 128


def _cparams(n_axes):
    return pltpu.CompilerParams(dimension_semantics=("arbitrary",) * n_axes,
                                vmem_limit_bytes=V7X_VMEM_LIMIT)


def _tile(n, pref):
    t = min(n, pref)
    while n % t:
        t //= 2
    return t


def _mod_kernel(c_ref, w_ref, b_ref, o_ref):
    c = c_ref[...]
    cs = jnp.broadcast_to(c * jax.nn.sigmoid(c), (8, c.shape[1]))
    r = jnp.dot(cs, w_ref[0], preferred_element_type=jnp.float32)
    o_ref[0] = r[0:1] + b_ref[0]


def _modulation(c, w_mod, b_mod):
    depth, d, n = w_mod.shape
    tn = _tile(n, 1024)
    out = pl.pallas_call(
        _mod_kernel,
        grid=(depth, n // tn),
        in_specs=[pl.BlockSpec((1, d), lambda i, j: (0, 0)),
                  pl.BlockSpec((1, d, tn), lambda i, j: (i, 0, j)),
                  pl.BlockSpec((1, 1, tn), lambda i, j: (i, 0, j))],
        out_specs=pl.BlockSpec((1, 1, tn), lambda i, j: (i, 0, j)),
        out_shape=jax.ShapeDtypeStruct((depth, 1, n), jnp.float32),
        compiler_params=_cparams(2),
        name="modulation",
    )(c, w_mod, b_mod.reshape(depth, 1, n))
    return out


def _rms(x):
    return x * lax.rsqrt(jnp.mean(x * x, axis=-1, keepdims=True) + EPS)


def _prenorm_kernel(x_ref, g_ref, sc_ref, sh_ref, h_ref):
    y = _rms(x_ref[...]) * g_ref[...]
    h_ref[...] = (y * (1.0 + sc_ref[...]) + sh_ref[...]).astype(h_ref.dtype)


def _prenorm(x, g, scale, shift):
    s, d = x.shape
    tm = _tile(s, 512)
    vec = pl.BlockSpec((1, d), lambda i: (0, 0))
    return pl.pallas_call(
        _prenorm_kernel,
        grid=(s // tm,),
        in_specs=[pl.BlockSpec((tm, d), lambda i: (i, 0)), vec, vec, vec],
        out_specs=pl.BlockSpec((tm, d), lambda i: (i, 0)),
        out_shape=jax.ShapeDtypeStruct((s, d), jnp.bfloat16),
        compiler_params=_cparams(1),
        name="prenorm",
    )(x, g, scale, shift)


def _post_kernel(x_ref, y_ref, pg_ref, gate_ref, xo_ref):
    yn = _rms(y_ref[...].astype(jnp.float32)) * pg_ref[...]
    xo_ref[...] = x_ref[...] + gate_ref[...] * yn


def _postpre_kernel(x_ref, y_ref, pg_ref, gate_ref, g_ref, sc_ref, sh_ref, xo_ref, h_ref):
    yn = _rms(y_ref[...].astype(jnp.float32)) * pg_ref[...]
    xn = x_ref[...] + gate_ref[...] * yn
    xo_ref[...] = xn
    h = _rms(xn) * g_ref[...]
    h_ref[...] = (h * (1.0 + sc_ref[...]) + sh_ref[...]).astype(h_ref.dtype)


def _post(x, y, post_g, gate, nxt=None):
    s, d = x.shape
    tm = _tile(s, 256)
    row = pl.BlockSpec((tm, d), lambda i: (i, 0))
    vec = pl.BlockSpec((1, d), lambda i: (0, 0))
    if nxt is None:
        return pl.pallas_call(
            _post_kernel, grid=(s // tm,),
            in_specs=[row, row, vec, vec], out_specs=row,
            out_shape=jax.ShapeDtypeStruct((s, d), jnp.float32),
            compiler_params=_cparams(1), name="postnorm",
        )(x, y, post_g, gate), None
    g, scale, shift = nxt
    return pl.pallas_call(
        _postpre_kernel, grid=(s // tm,),
        in_specs=[row, row, vec, vec, vec, vec, vec], out_specs=[row, row],
        out_shape=[jax.ShapeDtypeStruct((s, d), jnp.float32),
                   jax.ShapeDtypeStruct((s, d), jnp.bfloat16)],
        compiler_params=_cparams(1), name="postnorm_prenorm",
    )(x, y, post_g, gate, g, scale, shift)


def _mm_kernel(a_ref, b_ref, o_ref):
    o_ref[...] = jnp.dot(a_ref[...], b_ref[0].astype(a_ref.dtype),
                         preferred_element_type=jnp.float32).astype(o_ref.dtype)


def _in_projection(h, w, layer):
    m, k = h.shape
    n = w.shape[2]
    tm, tn = _tile(m, 1024), _tile(n, 512)
    return pl.pallas_call(
        _mm_kernel,
        grid=(m // tm, n // tn),
        in_specs=[pl.BlockSpec((tm, k), lambda i, j: (i, 0)),
                  pl.BlockSpec((1, k, tn), lambda i, j: (layer, 0, j))],
        out_specs=pl.BlockSpec((tm, tn), lambda i, j: (i, j)),
        out_shape=jax.ShapeDtypeStruct((m, n), jnp.bfloat16),
        compiler_params=_cparams(2),
        name="in_projection",
    )(h, w)


def _mm3_kernel(a1, a2, a3, w_ref, o_ref):
    acc, row = None, 0
    for a in (a1, a2, a3):
        w = w_ref[0, row:row + a.shape[1]].astype(a.dtype)
        part = jnp.dot(a[...], w, preferred_element_type=jnp.float32)
        acc = part if acc is None else acc + part
        row += a.shape[1]
    o_ref[...] = acc.astype(o_ref.dtype)


def _out_projection(oa, ob, oc, w, layer):
    m = oa.shape[0]
    k, n = w.shape[1:]
    tm, tn = _tile(m, 1024), _tile(n, 512)
    a_spec = lambda a: pl.BlockSpec((tm, a.shape[1]), lambda i, j: (i, 0))
    return pl.pallas_call(
        _mm3_kernel,
        grid=(m // tm, n // tn),
        in_specs=[a_spec(oa), a_spec(ob), a_spec(oc),
                  pl.BlockSpec((1, k, tn), lambda i, j: (layer, 0, j))],
        out_specs=pl.BlockSpec((tm, tn), lambda i, j: (i, j)),
        out_shape=jax.ShapeDtypeStruct((m, n), jnp.bfloat16),
        compiler_params=_cparams(2),
        name="out_projection",
    )(oa, ob, oc, w)


def _rope_tables(s):
    rows = s // GRID_W
    row = jnp.repeat(jnp.arange(rows), GRID_W)
    col = jnp.tile(jnp.arange(GRID_W), rows)
    quarter = HEAD_DIM // 4
    freqs = ROPE_THETA ** (-jnp.arange(0, 2 * quarter, 2, dtype=jnp.float32) / (2 * quarter))
    ar = row.astype(jnp.float32)[:, None] * freqs[None, :]
    ac = col.astype(jnp.float32)[:, None] * freqs[None, :]
    cr, sr, cc, sc = jnp.cos(ar), jnp.sin(ar), jnp.cos(ac), jnp.sin(ac)
    cos = jnp.concatenate([cr, cr, cc, cc], axis=-1)
    sin = jnp.concatenate([-sr, sr, -sc, sc], axis=-1)
    return cos, sin


def _rope(x, cos, sin):
    quarter = HEAD_DIM // 4
    fwd = pltpu.roll(x, HEAD_DIM - quarter, 1)
    bwd = pltpu.roll(x, quarter, 1)
    lane = lax.broadcasted_iota(jnp.int32, x.shape, 1)
    partner = jnp.where((lane % (2 * quarter)) < quarter, fwd, bwd)
    return x * cos + partner * sin


def _prep_kernel(av_ref, bv_ref, cv_ref, ck_ref, aq_ref, ak_ref, bq_ref, bk_ref, cos_ref, sin_ref, kg_ref,
                 vta_ref, vtb_ref, vtc_ref, kc_ref, nrm_ref, *, ts):
    def put_t(src_ref, n_heads, dst_ref, chunk):
        for hh in range(n_heads):
            vt = src_ref[:, hh * HEAD_DIM:(hh + 1) * HEAD_DIM].astype(jnp.float32).T
            for cidx in range(ts // chunk):
                dst_ref[hh, cidx, :HEAD_DIM] = vt[:, cidx * chunk:(cidx + 1) * chunk].astype(dst_ref.dtype)
                dst_ref[hh, cidx, HEAD_DIM:] = jnp.ones((VT_ROWS - HEAD_DIM, chunk), dst_ref.dtype)

    put_t(av_ref, A_HEADS, vta_ref, A_TILE)
    put_t(bv_ref, B_KV_HEADS, vtb_ref, B_TILE)
    put_t(cv_ref, C_KV_HEADS, vtc_ref, C_TK)
    cos, sin, kg = cos_ref[...], sin_ref[...], kg_ref[...]
    for hh in range(C_KV_HEADS):
        k = ck_ref[:, hh * HEAD_DIM:(hh + 1) * HEAD_DIM].astype(jnp.float32)
        k = _rope(_rms(k) * kg, cos, sin)
        kc_ref[:, hh * HEAD_DIM:(hh + 1) * HEAD_DIM] = k.astype(kc_ref.dtype)

    def max_sq_norm(src_ref, n_heads):
        ones = jnp.ones((HEAD_DIM, HEAD_DIM), src_ref.dtype)
        best = None
        for hh in range(n_heads):
            x = src_ref[:, hh * HEAD_DIM:(hh + 1) * HEAD_DIM]
            row_sums = jnp.dot(x * x, ones, preferred_element_type=jnp.float32)
            top = jnp.max(row_sums, axis=0, keepdims=True)
            best = top if best is None else jnp.maximum(best, top)
        return best

    nrm_ref[0] = jnp.concatenate(
        [max_sq_norm(aq_ref, A_HEADS), max_sq_norm(ak_ref, A_HEADS), max_sq_norm(bq_ref, B_HEADS),
         max_sq_norm(bk_ref, B_KV_HEADS), jnp.zeros((4, HEAD_DIM), jnp.float32)], axis=0)


def _prep(proj, cos, sin, k_gain):
    s = proj.shape[0]
    ts = max(A_TILE, B_TILE, C_TK)
    assert s % ts == 0
    bf = jnp.bfloat16
    n = s // ts
    vt_spec = lambda heads, t: pl.BlockSpec((heads, ts // t, VT_ROWS, t), lambda i: (0, i, 0, 0))
    vt_shape = lambda heads, t: jax.ShapeDtypeStruct((heads, s // t, VT_ROWS, t), bf)
    return pl.pallas_call(
        functools.partial(_prep_kernel, ts=ts),
        grid=(n,),
        in_specs=[pl.BlockSpec((ts, A_W), lambda i: (i, AV * HEAD_DIM // A_W)),
                  pl.BlockSpec((ts, B_KV_HEADS * HEAD_DIM), lambda i: (i, BV // B_KV_HEADS)),
                  pl.BlockSpec((ts, C_KV_HEADS * HEAD_DIM), lambda i: (i, CV // C_KV_HEADS)),
                  pl.BlockSpec((ts, C_KV_HEADS * HEAD_DIM), lambda i: (i, CK // C_KV_HEADS)),
                  pl.BlockSpec((ts, A_W), lambda i: (i, AQ * HEAD_DIM // A_W)),
                  pl.BlockSpec((ts, A_W), lambda i: (i, AK * HEAD_DIM // A_W)),
                  pl.BlockSpec((ts, B_W), lambda i: (i, BQ * HEAD_DIM // B_W)),
                  pl.BlockSpec((ts, B_KV_HEADS * HEAD_DIM), lambda i: (i, BK // B_KV_HEADS)),
                  pl.BlockSpec((ts, HEAD_DIM), lambda i: (i, 0)),
                  pl.BlockSpec((ts, HEAD_DIM), lambda i: (i, 0)),
                  pl.BlockSpec((1, HEAD_DIM), lambda i: (0, 0))],
        out_specs=[vt_spec(A_HEADS, A_TILE), vt_spec(B_KV_HEADS, B_TILE), vt_spec(C_KV_HEADS, C_TK),
                   pl.BlockSpec((ts, C_KV_HEADS * HEAD_DIM), lambda i: (i, 0)),
                   pl.BlockSpec((1, 8, HEAD_DIM), lambda i: (i, 0, 0))],
        out_shape=[vt_shape(A_HEADS, A_TILE), vt_shape(B_KV_HEADS, B_TILE), vt_shape(C_KV_HEADS, C_TK),
                   jax.ShapeDtypeStruct((s, C_KV_HEADS * HEAD_DIM), bf),
                   jax.ShapeDtypeStruct((n, 8, HEAD_DIM), jnp.float32)],
        compiler_params=_cparams(1),
        name="vt_krope_prep",
    )(proj, proj, proj, proj, proj, proj, proj, proj, cos, sin, k_gain)


def _scores(k_ref, head, row0, tk, qt_ref, s_ref, mx_ref, slot, bias=None):
    kt = k_ref[pl.ds(pl.multiple_of(row0, tk), tk), head * HEAD_DIM:(head + 1) * HEAD_DIM]
    s = jnp.dot(kt, qt_ref[...], preferred_element_type=jnp.float32)
    if bias is not None:
        s = s + bias
    s_ref[slot] = s
    mx_ref[slot] = jnp.max(s, axis=0, keepdims=True)


def _accumulate(vt, s_ref, mx_ref, m_ref, acc_ref, slot):
    m_old = m_ref[...]
    m_new = jnp.maximum(m_old, mx_ref[slot])
    alpha = jnp.exp2(m_old - m_new)
    p = jnp.exp2(s_ref[slot] - m_new).astype(jnp.bfloat16)
    acc_ref[...] = alpha * acc_ref[...] + jnp.dot(vt, p, preferred_element_type=jnp.float32)
    m_ref[...] = m_new


def _load_q(q_refs, qt_ref, tq, prep_t=lambda qt: qt):
    scale2 = HEAD_DIM ** -0.5 * LOG2E
    for g, q_ref in enumerate(q_refs):
        qt = prep_t(q_ref[...].astype(jnp.float32).T)
        qt_ref[:, g * tq:(g + 1) * tq] = (qt * scale2).astype(qt_ref.dtype)


def _norm_rope_t(xt, gain_col, cos_t, sin_t):
    quarter = HEAD_DIM // 4
    y = xt * lax.rsqrt(jnp.mean(xt * xt, axis=0, keepdims=True) + EPS) * gain_col
    partner = jnp.concatenate([y[quarter:2 * quarter], y[:quarter], y[3 * quarter:], y[2 * quarter:3 * quarter]],
                              axis=0)
    return y * cos_t + partner * sin_t


def _store_gated(acc, g_refs, o_ref, tq, first=0, extra_denominator=None):
    for g, g_ref in enumerate(g_refs):
        sl = slice(g * tq, (g + 1) * tq)
        den = acc[HEAD_DIM:HEAD_DIM + 1, sl]
        if extra_denominator is not None:
            den = den + extra_denominator[g]
        o = (acc[:HEAD_DIM, sl] / den).T
        gate = g_ref[...].astype(jnp.float32)
        cols = slice((first + g) * HEAD_DIM, (first + g + 1) * HEAD_DIM)
        o_ref[:, cols] = (o * (gate * jax.nn.sigmoid(gate))).astype(o_ref.dtype)


def _band_kernel(*refs, heads, group, t, nkv, reach, has_sink, bounded):
    it = iter(refs)
    q_refs = [[next(it) for _ in range(group)] for _ in range(heads)]
    g_refs = [[next(it) for _ in range(group)] for _ in range(heads)]
    k_ref, vt_ref, bias_ref = next(it), next(it), next(it)
    sink_ref = next(it) if has_sink else None
    o_ref = next(it)
    qt_ref, s_ref, mx_ref, m_ref, acc_ref = (next(it) for _ in range(5))
    h = pl.program_id(0)
    qb = pl.program_id(1)
    n_off = 2 * reach + 1

    nsb = t // SUB
    n_sub = bias_ref.shape[1] - 1

    def block(o):
        kb = qb + (o - reach)
        return jnp.clip(kb, 0, nkv - 1), jnp.logical_and(kb >= 0, kb < nkv)

    def bias_tile(hh, o, valid):
        rows = []
        for a in range(nsb):
            cols = []
            for g in range(group):
                for b in range(nsb):
                    sub = (o - reach) * nsb + a - b + (n_sub - 1) // 2
                    cols.append(bias_ref[hh * group + g, jnp.where(valid, sub, n_sub)])
            rows.append(jnp.concatenate(cols, axis=1))
        return jnp.concatenate(rows, axis=0)

    if bounded:
        for hh in range(heads):
            _load_q(q_refs[hh], qt_ref.at[hh], t)
        width = min(n_off, nkv)
        start = jnp.clip(qb - reach, 0, nkv - width)

        def bias_window(hh):
            rows = []
            for a in range(width * nsb):
                cols = []
                for g in range(group):
                    for b in range(nsb):
                        sub = (start - qb) * nsb + a - b + (n_sub - 1) // 2
                        inside = jnp.logical_and(sub >= 0, sub < n_sub)
                        cols.append(bias_ref[hh * group + g, jnp.where(inside, sub, n_sub)])
                rows.append(jnp.concatenate(cols, axis=1))
            return jnp.concatenate(rows, axis=0)

        for hh in range(heads):
            keys = k_ref[pl.ds(pl.multiple_of(start * t, t), width * t), hh * HEAD_DIM:(hh + 1) * HEAD_DIM]
            s = jnp.dot(keys, qt_ref[hh], preferred_element_type=jnp.float32) + bias_window(hh)
            p = jnp.exp2(s).astype(jnp.bfloat16)
            total = None
            for o in range(width):
                part = jnp.dot(vt_ref[hh, start + o], p[o * t:(o + 1) * t],
                               preferred_element_type=jnp.float32)
                total = part if total is None else total + part
            sinks = None
            if has_sink:
                sinks = [jnp.exp2(jnp.full((1, t), sink_ref[(h * heads + hh) * group + g] * LOG2E, jnp.float32))
                         for g in range(group)]
            _store_gated(total, g_refs[hh], o_ref, t, hh * group, sinks)
        return

    for hh in range(heads):
        _load_q(q_refs[hh], qt_ref.at[hh], t)
        acc_ref[hh, :HEAD_DIM] = jnp.zeros((HEAD_DIM, group * t), jnp.float32)
        if has_sink:
            for g in range(group):
                sink = sink_ref[(h * heads + hh) * group + g] * LOG2E
                m_ref[hh, :, g * t:(g + 1) * t] = jnp.full((1, t), sink, jnp.float32)
            acc_ref[hh, HEAD_DIM:] = jnp.ones((VT_ROWS - HEAD_DIM, group * t), jnp.float32)
        else:
            m_ref[hh] = jnp.full((1, group * t), M_INIT, jnp.float32)
            acc_ref[hh, HEAD_DIM:] = jnp.zeros((VT_ROWS - HEAD_DIM, group * t), jnp.float32)

    units = [(hh, o) for hh in range(heads) for o in range(n_off)]

    def scores(u):
        hh, o = units[u]
        kb, valid = block(o)
        _scores(k_ref, hh, kb * t, t, qt_ref.at[hh], s_ref, mx_ref, u % 2, bias_tile(hh, o, valid))

    scores(0)
    for u, (hh, o) in enumerate(units):
        if u + 1 < len(units):
            scores(u + 1)
        _accumulate(vt_ref[hh, block(o)[0]], s_ref, mx_ref, m_ref.at[hh], acc_ref.at[hh], u % 2)
        if o == n_off - 1:
            _store_gated(acc_ref.at[hh], g_refs[hh], o_ref, t, hh * group)


def _dense_kernel(*refs, group, tq, tk, nkv, bounded):
    it = iter(refs)
    q_refs = [next(it) for _ in range(group)]
    g_refs = [next(it) for _ in range(group)]
    k_ref, vt_ref, cos_ref, sin_ref, qg_ref, o_ref = (next(it) for _ in range(6))
    qt_ref, s_ref, mx_ref, m_ref, acc_ref = (next(it) for _ in range(5))
    qt_ref, m_ref, acc_ref = qt_ref.at[0], m_ref.at[0], acc_ref.at[0]

    _load_q(q_refs, qt_ref, tq, lambda xt: _norm_rope_t(xt, qg_ref[...], cos_ref[...], sin_ref[...]))
    acc_ref[...] = jnp.zeros(acc_ref.shape, jnp.float32)

    if bounded:
        def block(jblk):
            kt = k_ref[pl.ds(pl.multiple_of(jblk * tk, tk), tk), :]
            s = jnp.dot(kt, qt_ref[...], preferred_element_type=jnp.float32)
            return jnp.dot(vt_ref[0, jblk], jnp.exp2(s).astype(jnp.bfloat16),
                           preferred_element_type=jnp.float32)

        per_iter = math.gcd(nkv, BOUNDED_GROUP)

        def blocks(it, carry):
            part = block(it * per_iter)
            for u in range(1, per_iter):
                part = part + block(it * per_iter + u)
            acc_ref[...] += part
            return carry

        lax.fori_loop(0, nkv // per_iter, blocks, 0)
        _store_gated(acc_ref, g_refs, o_ref, tq)
        return

    m_ref[...] = jnp.full(m_ref.shape, M_INIT, jnp.float32)

    def scores(jblk, slot):
        _scores(k_ref, 0, jblk * tk, tk, qt_ref, s_ref, mx_ref, slot)

    def accumulate(jblk, slot):
        _accumulate(vt_ref[0, jblk], s_ref, mx_ref, m_ref, acc_ref, slot)

    scores(0, 0)

    def pair(j0):
        scores(j0 + 1, 1)
        accumulate(j0, 0)
        scores(j0 + 2, 0)
        accumulate(j0 + 1, 1)

    def pairs(it, carry):
        for u in range(DENSE_UNROLL):
            pair(2 * (it * DENSE_UNROLL + u))
        return carry

    n_pairs = nkv // 2 - 1
    lax.fori_loop(0, n_pairs // DENSE_UNROLL, pairs, 0)
    for jj in range(n_pairs - n_pairs % DENSE_UNROLL, n_pairs):
        pair(2 * jj)
    scores(nkv - 1, 1)
    accumulate(nkv - 2, 0)
    accumulate(nkv - 1, 1)
    _store_gated(acc_ref, g_refs, o_ref, tq)


def _attention(kern, proj, k_src, k_col, vt, q_col, g_col, n_kv_heads, group, tq, tk, extra_specs,
               extra_args, name, heads=1, bounded=False):
    s = proj.shape[0]
    parked = 8 if bounded else tk
    assert s % tq == 0 and s % tk == 0 and n_kv_heads % heads == 0 and k_col % heads == 0
    in_specs, args = [], []
    for col in (q_col, g_col):
        for hh in range(heads):
            for g in range(group):
                in_specs.append(pl.BlockSpec(
                    (tq, HEAD_DIM), lambda h, qb, col=col, hh=hh, g=g: (qb, col + (h * heads + hh) * group + g)))
                args.append(proj)
    in_specs += [pl.BlockSpec((s, heads * HEAD_DIM), lambda h, qb: (0, k_col // heads + h)),
                 pl.BlockSpec((heads, s // tk, VT_ROWS, tk), lambda h, qb: (h, 0, 0, 0))] + extra_specs
    args += [k_src, vt] + extra_args
    n = group * tq
    return pl.pallas_call(
        kern,
        grid=(n_kv_heads // heads, s // tq),
        in_specs=in_specs,
        out_specs=pl.BlockSpec((tq, heads * group * HEAD_DIM), lambda h, qb: (qb, h)),
        out_shape=jax.ShapeDtypeStruct((s, n_kv_heads * group * HEAD_DIM), jnp.bfloat16),
        scratch_shapes=[pltpu.VMEM((heads, HEAD_DIM, n), jnp.bfloat16),
                        pltpu.VMEM((2, parked, n), jnp.float32),
                        pltpu.VMEM((2, 1, n), jnp.float32),
                        pltpu.VMEM((heads, 1, n), jnp.float32),
                        pltpu.VMEM((heads, VT_ROWS, n), jnp.float32)],
        compiler_params=_cparams(2),
        name=name,
    )(*args)


def _band_attention(proj, k_col, vt, q_col, g_col, n_kv_heads, group, t, reach, bias, sink, bounded,
                    heads=BAND_HEADS_PER_STEP):
    nkv = proj.shape[0] // t
    extra_specs = [pl.BlockSpec((heads * group,) + bias.shape[1:], lambda h, qb: (h, 0, 0, 0))]
    extra_args = [bias]
    if sink is not None:
        extra_specs.append(pl.BlockSpec(memory_space=pltpu.SMEM))
        extra_args.append(sink)
    kern = functools.partial(_band_kernel, heads=heads, group=group, t=t, nkv=nkv, reach=reach,
                             has_sink=sink is not None, bounded=bounded)
    name = "band_attention_g%d%s" % (group, "_bounded" if bounded else "")
    return _attention(kern, proj, proj, k_col, vt, q_col, g_col, n_kv_heads, group, t, t,
                      extra_specs, extra_args, name, heads, bounded)


def _dense_attention(proj, kc, vt, q_col, g_col, n_kv_heads, group, tk, cos_t, sin_t, q_gain_col, bounded):
    nkv = proj.shape[0] // tk
    tq = C_TQ[bounded]
    assert nkv % 2 == 0
    extra_specs = [pl.BlockSpec((HEAD_DIM, tq), lambda h, qb: (0, qb)),
                   pl.BlockSpec((HEAD_DIM, tq), lambda h, qb: (0, qb)),
                   pl.BlockSpec((HEAD_DIM, 1), lambda h, qb: (0, 0))]
    kern = functools.partial(_dense_kernel, group=group, tq=tq, tk=tk, nkv=nkv, bounded=bounded)
    name = "dense_attention_g%d%s" % (group, "_bounded" if bounded else "")
    return _attention(kern, proj, kc, 0, vt, q_col, g_col, n_kv_heads, group, tq, tk,
                      extra_specs, [cos_t, sin_t, q_gain_col], name, 1, bounded)


def _t5_bucket_np(rel):
    half_b = NUM_BUCKETS // 2
    max_exact = half_b // 2
    ret = np.where(rel > 0, half_b, 0)
    n = np.abs(rel)
    nf = np.maximum(n, 1).astype(np.float32)
    large = max_exact + (np.log(nf / np.float32(max_exact)) / np.float32(math.log(MAX_DISTANCE / max_exact))
                         * np.float32(half_b - max_exact)).astype(np.int32)
    large = np.minimum(large, half_b - 1)
    return (ret + np.where(n < max_exact, n, large)).astype(np.int32)


def _toeplitz_tiles(vec, tile, reach):
    heads = vec.shape[0]
    r = (vec.shape[1] - 1) // 2
    n_off = 2 * reach + 1
    span = 2 * tile - 1
    lo = -reach * tile - (tile - 1)
    hi = reach * tile + (tile - 1)
    padded = jnp.pad(vec, ((0, 0), (-lo - r, hi - r)), constant_values=MASKED)
    rows = [padded[:, o * tile:o * tile + span] for o in range(n_off)]
    u = jnp.stack(rows, axis=1)[:, :, ::-1]
    w = jnp.pad(u, ((0, 0), (0, 0), (0, 1)))
    flat = jnp.broadcast_to(w[:, :, None, :], (heads, n_off, tile, span + 1))
    flat = flat.reshape(heads, n_off, tile * (span + 1))[:, :, :tile * span]
    skew = flat.reshape(heads, n_off, tile, span)
    tiles = skew[:, :, :, tile - 1:]
    masked = jnp.full((tiles.shape[0], 1) + tiles.shape[2:], MASKED, tiles.dtype)
    return jnp.concatenate([tiles, masked], axis=1)


def _bias_tiles(rel_bias):
    hi = lax.Precision.HIGHEST
    d = np.arange(-A_REACH, A_REACH + 1)
    count = np.zeros(d.shape, np.float64)
    for w, dil in A_PATTERNS:
        count += (d % dil == 0) & (np.abs(d) <= w // 2)
    log_count = np.where(count > 0, np.log(np.maximum(count, 1.0)), 0.0).astype(np.float32)
    table = rel_bias.astype(jnp.float32)
    onehot_a = jnp.asarray(np.eye(NUM_BUCKETS, dtype=np.float32)[_t5_bucket_np(d)])
    vec_a = (jnp.dot(onehot_a, table[:, :A_HEADS], precision=hi).T + log_count[None]) * LOG2E
    vec_a = jnp.where(jnp.asarray(count > 0)[None], vec_a, MASKED)
    tiles_a = _toeplitz_tiles(vec_a, SUB, (-(-A_REACH // A_TILE) + 1) * (A_TILE // SUB) - 1)
    db = np.arange(-B_HALF_WINDOW, B_HALF_WINDOW + 1)
    onehot_b = jnp.asarray(np.eye(NUM_BUCKETS, dtype=np.float32)[_t5_bucket_np(db)])
    vec_b = jnp.dot(onehot_b, table[:, A_HEADS:], precision=hi).T * LOG2E
    tiles_b = _toeplitz_tiles(vec_b, SUB, (-(-B_HALF_WINDOW // B_TILE) + 1) * (B_TILE // SUB) - 1)
    top_a = jnp.max(jnp.abs(jnp.where(jnp.asarray(count > 0)[None], vec_a, 0.0)))
    return tiles_a, tiles_b, top_a, jnp.max(jnp.abs(vec_b))


def kernel(x, c, w_mod, b_mod, pre_norm, post_norm, w_in, w_out, q_norm, k_norm, sink, rel_bias):
    bsz, s, d = x.shape
    assert bsz == 1 and c.shape[0] == 1
    depth = w_in.shape[0]
    assert w_in.shape[2] == IN_WIDTH and w_out.shape[1] == A_W + B_W + C_W
    x2 = x[0]
    cos, sin = _rope_tables(s)
    cos_t, sin_t = cos.T, sin.T
    tiles_a, tiles_b, top_bias_a, top_bias_b = _bias_tiles(rel_bias)
    mod = _modulation(c, w_mod, b_mod)[:, 0]
    shift, scale, gate = mod[:, :d], mod[:, d:2 * d], mod[:, 2 * d:]
    vec = lambda a, i: a[i][None, :]

    h = _prenorm(x2, vec(pre_norm, 0), vec(scale, 0), vec(shift, 0))
    for i in range(depth):
        proj = _in_projection(h, w_in, i)
        vta, vtb, vtc, kc, sq_norms = _prep(proj, cos, sin, vec(k_norm, i))
        sq_norms = jnp.max(sq_norms, axis=(0, 2))
        qk_scale = NORM_SLACK * HEAD_DIM ** -0.5 * LOG2E
        bound_a = qk_scale * jnp.sqrt(sq_norms[0] * sq_norms[1]) + top_bias_a
        bound_b = jnp.maximum(qk_scale * jnp.sqrt(sq_norms[2] * sq_norms[3]) + top_bias_b,
                              jnp.max(jnp.abs(sink[i])) * LOG2E)
        band_a = functools.partial(_band_attention, proj, AK, vta, AQ, AG, A_HEADS, 1, A_TILE,
                                   -(-A_REACH // A_TILE), tiles_a, None)
        band_b = functools.partial(_band_attention, proj, BK, vtb, BQ, BG, B_KV_HEADS, B_HEADS // B_KV_HEADS,
                                   B_TILE, -(-B_HALF_WINDOW // B_TILE), tiles_b, sink[i])
        oa = lax.cond(bound_a <= LOGIT_BOUND, lambda: band_a(True), lambda: band_a(False))
        ob = lax.cond(bound_b <= LOGIT_BOUND, lambda: band_b(True), lambda: band_b(False))
        dense = functools.partial(_dense_attention, proj, kc, vtc, CQ, CG, C_KV_HEADS,
                                  C_HEADS // C_KV_HEADS, C_TK, cos_t, sin_t, q_norm[i][:, None])
        logit_bound = (HEAD_DIM ** 0.5 * LOG2E) * jnp.max(jnp.abs(q_norm[i])) * jnp.max(jnp.abs(k_norm[i]))
        oc = lax.cond(logit_bound <= LOGIT_BOUND, lambda: dense(True), lambda: dense(False))
        y = _out_projection(oa, ob, oc, w_out, i)
        nxt = None if i == depth - 1 else (vec(pre_norm, i + 1), vec(scale, i + 1), vec(shift, i + 1))
        x2, h = _post(x2, y, vec(post_norm, i), vec(gate, i), nxt)
    return x2[None]
```

```python
import functools
import math

import numpy as np
import jax
import jax.numpy as jnp
from jax import lax
from jax.experimental import pallas as pl
from jax.experimental.pallas import tpu as pltpu

HEAD_DIM = 128
GRID_W = 64
A_HEADS = 12
A_PATTERNS = ((128, 1), (512, 4), (2048, 16))
B_HEADS = 8
B_KV_HEADS = 2
B_HALF_WINDOW = 128
C_HEADS = 12
C_KV_HEADS = 4
ROPE_THETA = 10000.0
NUM_BUCKETS = 32
MAX_DISTANCE = 1024
EPS = 1e-6
LOG2E = math.log2(math.e)
MASKED = -1e30
M_INIT = -1e29

A_W = A_HEADS * HEAD_DIM
B_W = B_HEADS * HEAD_DIM
C_W = C_HEADS * HEAD_DIM
AQ, AK, AV, AG = 0, 12, 24, 36
BQ, BK, BV, BG = 48, 56, 58, 60
CQ, CK, CV, CG = 68, 80, 84, 88
IN_WIDTH = 100 * HEAD_DIM

A_REACH = max(w // 2 for w, _ in A_PATTERNS)
A_TILE = 512
B_TILE = 256
C_TQ = {True: 512, False: 256}
C_TK = 1024
BAND_HEADS_PER_STEP = 2
BOUNDED_GROUP = 4
NORM_SLACK = 1.02
LOGIT_BOUND = 64.0
DENSE_UNROLL = 3
VT_ROWS = HEAD_DIM + 16
V7X_VMEM_LIMIT = 56 * 1024 * 1024
SUBLANES = 8
SUB = 128


def _cparams(n_axes):
    return pltpu.CompilerParams(dimension_semantics=("arbitrary",) * n_axes,
                                vmem_limit_bytes=V7X_VMEM_LIMIT)


def _tile(n, pref):
    t = min(n, pref)
    while n % t:
        t //= 2
    return t


def _mod_kernel(c_ref, w_ref, b_ref, o_ref):
    c = c_ref[...]
    cs = jnp.broadcast_to(c * jax.nn.sigmoid(c), (SUBLANES, c.shape[1]))
    r = jnp.dot(cs, w_ref[0], preferred_element_type=jnp.float32)
    o_ref[0] = r[0:1] + b_ref[0]


def _modulation(c, w_mod, b_mod):
    depth, d, n = w_mod.shape
    tn = _tile(n, 1024)
    out = pl.pallas_call(
        _mod_kernel,
        grid=(depth, n // tn),
        in_specs=[pl.BlockSpec((1, d), lambda i, j: (0, 0)),
                  pl.BlockSpec((1, d, tn), lambda i, j: (i, 0, j)),
                  pl.BlockSpec((1, 1, tn), lambda i, j: (i, 0, j))],
        out_specs=pl.BlockSpec((1, 1, tn), lambda i, j: (i, 0, j)),
        out_shape=jax.ShapeDtypeStruct((depth, 1, n), jnp.float32),
        compiler_params=_cparams(2),
        name="modulation",
    )(c, w_mod, b_mod.reshape(depth, 1, n))
    return out


def _rms(x):
    return x * lax.rsqrt(jnp.mean(x * x, axis=-1, keepdims=True) + EPS)


def _prenorm_kernel(x_ref, g_ref, sc_ref, sh_ref, h_ref):
    y = _rms(x_ref[...]) * g_ref[...]
    h_ref[...] = (y * (1.0 + sc_ref[...]) + sh_ref[...]).astype(h_ref.dtype)


def _prenorm(x, g, scale, shift):
    s, d = x.shape
    tm = _tile(s, 512)
    vec = pl.BlockSpec((1, d), lambda i: (0, 0))
    return pl.pallas_call(
        _prenorm_kernel,
        grid=(s // tm,),
        in_specs=[pl.BlockSpec((tm, d), lambda i: (i, 0)), vec, vec, vec],
        out_specs=pl.BlockSpec((tm, d), lambda i: (i, 0)),
        out_shape=jax.ShapeDtypeStruct((s, d), jnp.bfloat16),
        compiler_params=_cparams(1),
        name="prenorm",
    )(x, g, scale, shift)


def _post_kernel(x_ref, y_ref, pg_ref, gate_ref, xo_ref):
    yn = _rms(y_ref[...].astype(jnp.float32)) * pg_ref[...]
    xo_ref[...] = x_ref[...] + gate_ref[...] * yn


def _postpre_kernel(x_ref, y_ref, pg_ref, gate_ref, g_ref, sc_ref, sh_ref, xo_ref, h_ref):
    yn = _rms(y_ref[...].astype(jnp.float32)) * pg_ref[...]
    xn = x_ref[...] + gate_ref[...] * yn
    xo_ref[...] = xn
    h = _rms(xn) * g_ref[...]
    h_ref[...] = (h * (1.0 + sc_ref[...]) + sh_ref[...]).astype(h_ref.dtype)


def _post(x, y, post_g, gate, nxt=None):
    s, d = x.shape
    tm = _tile(s, 256)
    row = pl.BlockSpec((tm, d), lambda i: (i, 0))
    vec = pl.BlockSpec((1, d), lambda i: (0, 0))
    if nxt is None:
        return pl.pallas_call(
            _post_kernel, grid=(s // tm,),
            in_specs=[row, row, vec, vec], out_specs=row,
            out_shape=jax.ShapeDtypeStruct((s, d), jnp.float32),
            compiler_params=_cparams(1), name="postnorm",
        )(x, y, post_g, gate), None
    g, scale, shift = nxt
    return pl.pallas_call(
        _postpre_kernel, grid=(s // tm,),
        in_specs=[row, row, vec, vec, vec, vec, vec], out_specs=[row, row],
        out_shape=[jax.ShapeDtypeStruct((s, d), jnp.float32),
                   jax.ShapeDtypeStruct((s, d), jnp.bfloat16)],
        compiler_params=_cparams(1), name="postnorm_prenorm",
    )(x, y, post_g, gate, g, scale, shift)


def _mm_kernel(a_ref, b_ref, o_ref):
    o_ref[...] = jnp.dot(a_ref[...], b_ref[0].astype(a_ref.dtype),
                         preferred_element_type=jnp.float32).astype(o_ref.dtype)


def _in_projection(h, w, layer):
    m, k = h.shape
    n = w.shape[2]
    tm, tn = _tile(m, 1024), _tile(n, 512)
    return pl.pallas_call(
        _mm_kernel,
        grid=(m // tm, n // tn),
        in_specs=[pl.BlockSpec((tm, k), lambda i, j: (i, 0)),
                  pl.BlockSpec((1, k, tn), lambda i, j: (layer, 0, j))],
        out_specs=pl.BlockSpec((tm, tn), lambda i, j: (i, j)),
        out_shape=jax.ShapeDtypeStruct((m, n), jnp.bfloat16),
        compiler_params=_cparams(2),
        name="in_projection",
    )(h, w)


def _mm3_kernel(a1, a2, a3, w_ref, o_ref):
    acc, row = None, 0
    for a in (a1, a2, a3):
        w = w_ref[0, row:row + a.shape[1]].astype(a.dtype)
        part = jnp.dot(a[...], w, preferred_element_type=jnp.float32)
        acc = part if acc is None else acc + part
        row += a.shape[1]
    o_ref[...] = acc.astype(o_ref.dtype)


def _out_projection(oa, ob, oc, w, layer):
    m = oa.shape[0]
    k, n = w.shape[1:]
    tm, tn = _tile(m, 1024), _tile(n, 512)
    a_spec = lambda a: pl.BlockSpec((tm, a.shape[1]), lambda i, j: (i, 0))
    return pl.pallas_call(
        _mm3_kernel,
        grid=(m // tm, n // tn),
        in_specs=[a_spec(oa), a_spec(ob), a_spec(oc),
                  pl.BlockSpec((1, k, tn), lambda i, j: (layer, 0, j))],
        out_specs=pl.BlockSpec((tm, tn), lambda i, j: (i, j)),
        out_shape=jax.ShapeDtypeStruct((m, n), jnp.bfloat16),
        compiler_params=_cparams(2),
        name="out_projection",
    )(oa, ob, oc, w)


def _rope_tables(s):
    rows = s // GRID_W
    row = jnp.repeat(jnp.arange(rows), GRID_W)
    col = jnp.tile(jnp.arange(GRID_W), rows)
    quarter = HEAD_DIM // 4
    freqs = ROPE_THETA ** (-jnp.arange(0, 2 * quarter, 2, dtype=jnp.float32) / (2 * quarter))
    ar = row.astype(jnp.float32)[:, None] * freqs[None, :]
    ac = col.astype(jnp.float32)[:, None] * freqs[None, :]
    cr, sr, cc, sc = jnp.cos(ar), jnp.sin(ar), jnp.cos(ac), jnp.sin(ac)
    cos = jnp.concatenate([cr, cr, cc, cc], axis=-1)
    sin = jnp.concatenate([-sr, sr, -sc, sc], axis=-1)
    return cos.T, sin.T


def _norm_rope_t(xt, gain_col, cos_t, sin_t):
    quarter = HEAD_DIM // 4
    y = xt * lax.rsqrt(jnp.mean(xt * xt, axis=0, keepdims=True) + EPS) * gain_col
    partner = jnp.concatenate([y[quarter:2 * quarter], y[:quarter], y[3 * quarter:], y[2 * quarter:3 * quarter]],
                              axis=0)
    return y * cos_t + partner * sin_t


def _prep_kernel(av_ref, bv_ref, cv_ref, ck_ref, aq_ref, ak_ref, bq_ref, bk_ref, cos_ref, sin_ref, kg_ref,
                 vta_ref, vtb_ref, vtc_ref, kc_ref, nrm_ref, *, ts):
    def put_t(src_ref, n_heads, dst_ref, chunk):
        for hh in range(n_heads):
            vt = src_ref[:, hh * HEAD_DIM:(hh + 1) * HEAD_DIM].T
            for cidx in range(ts // chunk):
                dst_ref[hh, cidx, :HEAD_DIM] = vt[:, cidx * chunk:(cidx + 1) * chunk]
                dst_ref[hh, cidx, HEAD_DIM:] = jnp.ones((VT_ROWS - HEAD_DIM, chunk), dst_ref.dtype)

    put_t(av_ref, A_HEADS, vta_ref, A_TILE)
    put_t(bv_ref, B_KV_HEADS, vtb_ref, B_TILE)
    put_t(cv_ref, C_KV_HEADS, vtc_ref, C_TK)
    cos_t, sin_t, kg = cos_ref[...], sin_ref[...], kg_ref[...]
    for hh in range(C_KV_HEADS):
        kt = ck_ref[:, hh * HEAD_DIM:(hh + 1) * HEAD_DIM].astype(jnp.float32).T
        k = _norm_rope_t(kt, kg, cos_t, sin_t).T
        kc_ref[:, hh * HEAD_DIM:(hh + 1) * HEAD_DIM] = k.astype(kc_ref.dtype)

    def max_sq_norm(src_ref, n_heads):
        ones = jnp.ones((HEAD_DIM, HEAD_DIM), src_ref.dtype)
        best = None
        for hh in range(n_heads):
            x = src_ref[:, hh * HEAD_DIM:(hh + 1) * HEAD_DIM]
            row_sums = jnp.dot(x * x, ones, preferred_element_type=jnp.float32)
            top = jnp.max(row_sums, axis=0, keepdims=True)
            best = top if best is None else jnp.maximum(best, top)
        return best

    nrm_ref[0] = jnp.concatenate(
        [max_sq_norm(aq_ref, A_HEADS), max_sq_norm(ak_ref, A_HEADS), max_sq_norm(bq_ref, B_HEADS),
         max_sq_norm(bk_ref, B_KV_HEADS), jnp.zeros((SUBLANES - 4, HEAD_DIM), jnp.float32)], axis=0)


def _prep(proj, cos_t, sin_t, k_gain_col):
    s = proj.shape[0]
    ts = max(A_TILE, B_TILE, C_TK)
    assert s % ts == 0
    bf = jnp.bfloat16
    n = s // ts
    vt_spec = lambda heads, t: pl.BlockSpec((heads, ts // t, VT_ROWS, t), lambda i: (0, i, 0, 0))
    vt_shape = lambda heads, t: jax.ShapeDtypeStruct((heads, s // t, VT_ROWS, t), bf)
    return pl.pallas_call(
        functools.partial(_prep_kernel, ts=ts),
        grid=(n,),
        in_specs=[pl.BlockSpec((ts, A_W), lambda i: (i, AV * HEAD_DIM // A_W)),
                  pl.BlockSpec((ts, B_KV_HEADS * HEAD_DIM), lambda i: (i, BV // B_KV_HEADS)),
                  pl.BlockSpec((ts, C_KV_HEADS * HEAD_DIM), lambda i: (i, CV // C_KV_HEADS)),
                  pl.BlockSpec((ts, C_KV_HEADS * HEAD_DIM), lambda i: (i, CK // C_KV_HEADS)),
                  pl.BlockSpec((ts, A_W), lambda i: (i, AQ * HEAD_DIM // A_W)),
                  pl.BlockSpec((ts, A_W), lambda i: (i, AK * HEAD_DIM // A_W)),
                  pl.BlockSpec((ts, B_W), lambda i: (i, BQ * HEAD_DIM // B_W)),
                  pl.BlockSpec((ts, B_KV_HEADS * HEAD_DIM), lambda i: (i, BK // B_KV_HEADS)),
                  pl.BlockSpec((HEAD_DIM, ts), lambda i: (0, i)),
                  pl.BlockSpec((HEAD_DIM, ts), lambda i: (0, i)),
                  pl.BlockSpec((HEAD_DIM, 1), lambda i: (0, 0))],
        out_specs=[vt_spec(A_HEADS, A_TILE), vt_spec(B_KV_HEADS, B_TILE), vt_spec(C_KV_HEADS, C_TK),
                   pl.BlockSpec((ts, C_KV_HEADS * HEAD_DIM), lambda i: (i, 0)),
                   pl.BlockSpec((1, SUBLANES, HEAD_DIM), lambda i: (i, 0, 0))],
        out_shape=[vt_shape(A_HEADS, A_TILE), vt_shape(B_KV_HEADS, B_TILE), vt_shape(C_KV_HEADS, C_TK),
                   jax.ShapeDtypeStruct((s, C_KV_HEADS * HEAD_DIM), bf),
                   jax.ShapeDtypeStruct((n, SUBLANES, HEAD_DIM), jnp.float32)],
        compiler_params=_cparams(1),
        name="vt_krope_prep",
    )(proj, proj, proj, proj, proj, proj, proj, proj, cos_t, sin_t, k_gain_col)


def _scores(k_ref, head, row0, tk, qt_ref, s_ref, mx_ref, slot, bias=None):
    kt = k_ref[pl.ds(pl.multiple_of(row0, tk), tk), head * HEAD_DIM:(head + 1) * HEAD_DIM]
    s = jnp.dot(kt, qt_ref[...], preferred_element_type=jnp.float32)
    if bias is not None:
        s = s + bias
    s_ref[slot] = s
    mx_ref[slot] = jnp.max(s, axis=0, keepdims=True)


def _accumulate(vt, s_ref, mx_ref, m_ref, acc_ref, slot):
    m_old = m_ref[...]
    m_new = jnp.maximum(m_old, mx_ref[slot])
    alpha = jnp.exp2(m_old - m_new)
    p = jnp.exp2(s_ref[slot] - m_new).astype(jnp.bfloat16)
    acc_ref[...] = alpha * acc_ref[...] + jnp.dot(vt, p, preferred_element_type=jnp.float32)
    m_ref[...] = m_new


def _load_q(q_refs, qt_ref, tq, prep_t=lambda qt: qt):
    scale2 = HEAD_DIM ** -0.5 * LOG2E
    for g, q_ref in enumerate(q_refs):
        qt = prep_t(q_ref[...].astype(jnp.float32).T)
        qt_ref[:, g * tq:(g + 1) * tq] = (qt * scale2).astype(qt_ref.dtype)


def _store_gated(acc, g_refs, o_ref, tq, first=0, extra_denominator=None):
    for g, g_ref in enumerate(g_refs):
        sl = slice(g * tq, (g + 1) * tq)
        den = acc[HEAD_DIM:HEAD_DIM + 1, sl]
        if extra_denominator is not None:
            den = den + extra_denominator[g]
        o = (acc[:HEAD_DIM, sl] / den).T
        gate = g_ref[...].astype(jnp.float32)
        cols = slice((first + g) * HEAD_DIM, (first + g + 1) * HEAD_DIM)
        o_ref[:, cols] = (o * (gate * jax.nn.sigmoid(gate))).astype(o_ref.dtype)


def _band_kernel(*refs, heads, group, t, nkv, reach, has_sink, bounded):
    it = iter(refs)
    q_refs = [[next(it) for _ in range(group)] for _ in range(heads)]
    g_refs = [[next(it) for _ in range(group)] for _ in range(heads)]
    k_ref, vt_ref, bias_ref = next(it), next(it), next(it)
    sink_ref = next(it) if has_sink else None
    o_ref = next(it)
    qt_ref, s_ref, mx_ref, m_ref, acc_ref = (next(it) for _ in range(5))
    h = pl.program_id(0)
    qb = pl.program_id(1)
    n_off = 2 * reach + 1

    nsb = t // SUB
    n_sub = bias_ref.shape[1] - 1

    def block(o):
        kb = qb + (o - reach)
        return jnp.clip(kb, 0, nkv - 1), jnp.logical_and(kb >= 0, kb < nkv)

    def bias_tile(hh, o, valid):
        rows = []
        for a in range(nsb):
            cols = []
            for g in range(group):
                for b in range(nsb):
                    sub = (o - reach) * nsb + a - b + (n_sub - 1) // 2
                    cols.append(bias_ref[hh * group + g, jnp.where(valid, sub, n_sub)])
            rows.append(jnp.concatenate(cols, axis=1))
        return jnp.concatenate(rows, axis=0)

    if bounded:
        for hh in range(heads):
            _load_q(q_refs[hh], qt_ref.at[hh], t)
        width = min(n_off, nkv)
        start = jnp.clip(qb - reach, 0, nkv - width)

        def bias_window(hh):
            rows = []
            for a in range(width * nsb):
                cols = []
                for g in range(group):
                    for b in range(nsb):
                        sub = (start - qb) * nsb + a - b + (n_sub - 1) // 2
                        inside = jnp.logical_and(sub >= 0, sub < n_sub)
                        cols.append(bias_ref[hh * group + g, jnp.where(inside, sub, n_sub)])
                rows.append(jnp.concatenate(cols, axis=1))
            return jnp.concatenate(rows, axis=0)

        for hh in range(heads):
            keys = k_ref[pl.ds(pl.multiple_of(start * t, t), width * t), hh * HEAD_DIM:(hh + 1) * HEAD_DIM]
            s = jnp.dot(keys, qt_ref[hh], preferred_element_type=jnp.float32) + bias_window(hh)
            p = jnp.exp2(s).astype(jnp.bfloat16)
            total = None
            for o in range(width):
                part = jnp.dot(vt_ref[hh, start + o], p[o * t:(o + 1) * t],
                               preferred_element_type=jnp.float32)
                total = part if total is None else total + part
            sinks = None
            if has_sink:
                sinks = [jnp.exp2(jnp.full((1, t), sink_ref[(h * heads + hh) * group + g] * LOG2E, jnp.float32))
                         for g in range(group)]
            _store_gated(total, g_refs[hh], o_ref, t, hh * group, sinks)
        return

    for hh in range(heads):
        _load_q(q_refs[hh], qt_ref.at[hh], t)
        acc_ref[hh, :HEAD_DIM] = jnp.zeros((HEAD_DIM, group * t), jnp.float32)
        if has_sink:
            for g in range(group):
                sink = sink_ref[(h * heads + hh) * group + g] * LOG2E
                m_ref[hh, :, g * t:(g + 1) * t] = jnp.full((1, t), sink, jnp.float32)
            acc_ref[hh, HEAD_DIM:] = jnp.ones((VT_ROWS - HEAD_DIM, group * t), jnp.float32)
        else:
            m_ref[hh] = jnp.full((1, group * t), M_INIT, jnp.float32)
            acc_ref[hh, HEAD_DIM:] = jnp.zeros((VT_ROWS - HEAD_DIM, group * t), jnp.float32)

    units = [(hh, o) for hh in range(heads) for o in range(n_off)]

    def scores(u):
        hh, o = units[u]
        kb, valid = block(o)
        _scores(k_ref, hh, kb * t, t, qt_ref.at[hh], s_ref, mx_ref, u % 2, bias_tile(hh, o, valid))

    scores(0)
    for u, (hh, o) in enumerate(units):
        if u + 1 < len(units):
            scores(u + 1)
        _accumulate(vt_ref[hh, block(o)[0]], s_ref, mx_ref, m_ref.at[hh], acc_ref.at[hh], u % 2)
        if o == n_off - 1:
            _store_gated(acc_ref.at[hh], g_refs[hh], o_ref, t, hh * group)


def _dense_kernel(*refs, group, tq, tk, nkv, bounded):
    it = iter(refs)
    q_refs = [next(it) for _ in range(group)]
    g_refs = [next(it) for _ in range(group)]
    k_ref, vt_ref, cos_ref, sin_ref, qg_ref, o_ref = (next(it) for _ in range(6))
    qt_ref, s_ref, mx_ref, m_ref, acc_ref = (next(it) for _ in range(5))
    qt_ref, m_ref, acc_ref = qt_ref.at[0], m_ref.at[0], acc_ref.at[0]

    _load_q(q_refs, qt_ref, tq, lambda xt: _norm_rope_t(xt, qg_ref[...], cos_ref[...], sin_ref[...]))
    acc_ref[...] = jnp.zeros(acc_ref.shape, jnp.float32)

    if bounded:
        def block(jblk):
            kt = k_ref[pl.ds(pl.multiple_of(jblk * tk, tk), tk), :]
            s = jnp.dot(kt, qt_ref[...], preferred_element_type=jnp.float32)
            return jnp.dot(vt_ref[0, jblk], jnp.exp2(s).astype(jnp.bfloat16),
                           preferred_element_type=jnp.float32)

        per_iter = math.gcd(nkv, BOUNDED_GROUP)

        def blocks(it, carry):
            part = block(it * per_iter)
            for u in range(1, per_iter):
                part = part + block(it * per_iter + u)
            acc_ref[...] += part
            return carry

        lax.fori_loop(0, nkv // per_iter, blocks, 0)
        _store_gated(acc_ref, g_refs, o_ref, tq)
        return

    m_ref[...] = jnp.full(m_ref.shape, M_INIT, jnp.float32)

    def scores(jblk, slot):
        _scores(k_ref, 0, jblk * tk, tk, qt_ref, s_ref, mx_ref, slot)

    def accumulate(jblk, slot):
        _accumulate(vt_ref[0, jblk], s_ref, mx_ref, m_ref, acc_ref, slot)

    scores(0, 0)

    def pair(j0):
        scores(j0 + 1, 1)
        accumulate(j0, 0)
        scores(j0 + 2, 0)
        accumulate(j0 + 1, 1)

    def pairs(it, carry):
        for u in range(DENSE_UNROLL):
            pair(2 * (it * DENSE_UNROLL + u))
        return carry

    n_pairs = nkv // 2 - 1
    lax.fori_loop(0, n_pairs // DENSE_UNROLL, pairs, 0)
    for jj in range(n_pairs - n_pairs % DENSE_UNROLL, n_pairs):
        pair(2 * jj)
    scores(nkv - 1, 1)
    accumulate(nkv - 2, 0)
    accumulate(nkv - 1, 1)
    _store_gated(acc_ref, g_refs, o_ref, tq)


def _attention(kern, proj, k_src, k_col, vt, q_col, g_col, n_kv_heads, group, tq, tk, extra_specs,
               extra_args, name, heads=1, bounded=False):
    s = proj.shape[0]
    parked = SUBLANES if bounded else tk
    assert s % tq == 0 and s % tk == 0 and n_kv_heads % heads == 0 and k_col % heads == 0
    in_specs, args = [], []
    for col in (q_col, g_col):
        for hh in range(heads):
            for g in range(group):
                in_specs.append(pl.BlockSpec(
                    (tq, HEAD_DIM), lambda h, qb, col=col, hh=hh, g=g: (qb, col + (h * heads + hh) * group + g)))
                args.append(proj)
    in_specs += [pl.BlockSpec((s, heads * HEAD_DIM), lambda h, qb: (0, k_col // heads + h)),
                 pl.BlockSpec((heads, s // tk, VT_ROWS, tk), lambda h, qb: (h, 0, 0, 0))] + extra_specs
    args += [k_src, vt] + extra_args
    n = group * tq
    return pl.pallas_call(
        kern,
        grid=(n_kv_heads // heads, s // tq),
        in_specs=in_specs,
        out_specs=pl.BlockSpec((tq, heads * group * HEAD_DIM), lambda h, qb: (qb, h)),
        out_shape=jax.ShapeDtypeStruct((s, n_kv_heads * group * HEAD_DIM), jnp.bfloat16),
        scratch_shapes=[pltpu.VMEM((heads, HEAD_DIM, n), jnp.bfloat16),
                        pltpu.VMEM((2, parked, n), jnp.float32),
                        pltpu.VMEM((2, 1, n), jnp.float32),
                        pltpu.VMEM((heads, 1, n), jnp.float32),
                        pltpu.VMEM((heads, VT_ROWS, n), jnp.float32)],
        compiler_params=_cparams(2),
        name=name,
    )(*args)


def _band_attention(proj, k_col, vt, q_col, g_col, n_kv_heads, group, t, reach, bias, sink, bounded,
                    heads=BAND_HEADS_PER_STEP):
    nkv = proj.shape[0] // t
    extra_specs = [pl.BlockSpec((heads * group,) + bias.shape[1:], lambda h, qb: (h, 0, 0, 0))]
    extra_args = [bias]
    if sink is not None:
        extra_specs.append(pl.BlockSpec(memory_space=pltpu.SMEM))
        extra_args.append(sink)
    kern = functools.partial(_band_kernel, heads=heads, group=group, t=t, nkv=nkv, reach=reach,
                             has_sink=sink is not None, bounded=bounded)
    name = "band_attention_g%d%s" % (group, "_bounded" if bounded else "")
    return _attention(kern, proj, proj, k_col, vt, q_col, g_col, n_kv_heads, group, t, t,
                      extra_specs, extra_args, name, heads, bounded)


def _dense_attention(proj, kc, vt, q_col, g_col, n_kv_heads, group, tk, cos_t, sin_t, q_gain_col, bounded):
    nkv = proj.shape[0] // tk
    tq = C_TQ[bounded]
    assert nkv % 2 == 0
    extra_specs = [pl.BlockSpec((HEAD_DIM, tq), lambda h, qb: (0, qb)),
                   pl.BlockSpec((HEAD_DIM, tq), lambda h, qb: (0, qb)),
                   pl.BlockSpec((HEAD_DIM, 1), lambda h, qb: (0, 0))]
    kern = functools.partial(_dense_kernel, group=group, tq=tq, tk=tk, nkv=nkv, bounded=bounded)
    name = "dense_attention_g%d%s" % (group, "_bounded" if bounded else "")
    return _attention(kern, proj, kc, 0, vt, q_col, g_col, n_kv_heads, group, tq, tk,
                      extra_specs, [cos_t, sin_t, q_gain_col], name, 1, bounded)


def _t5_bucket_np(rel):
    half_b = NUM_BUCKETS // 2
    max_exact = half_b // 2
    ret = np.where(rel > 0, half_b, 0)
    n = np.abs(rel)
    nf = np.maximum(n, 1).astype(np.float32)
    large = max_exact + (np.log(nf / np.float32(max_exact)) / np.float32(math.log(MAX_DISTANCE / max_exact))
                         * np.float32(half_b - max_exact)).astype(np.int32)
    large = np.minimum(large, half_b - 1)
    return (ret + np.where(n < max_exact, n, large)).astype(np.int32)


def _toeplitz_tiles(vec, tile, reach):
    heads = vec.shape[0]
    r = (vec.shape[1] - 1) // 2
    n_off = 2 * reach + 1
    span = 2 * tile - 1
    lo = -reach * tile - (tile - 1)
    hi = reach * tile + (tile - 1)
    padded = jnp.pad(vec, ((0, 0), (-lo - r, hi - r)), constant_values=MASKED)
    rows = [padded[:, o * tile:o * tile + span] for o in range(n_off)]
    u = jnp.stack(rows, axis=1)[:, :, ::-1]
    w = jnp.pad(u, ((0, 0), (0, 0), (0, 1)))
    flat = jnp.broadcast_to(w[:, :, None, :], (heads, n_off, tile, span + 1))
    flat = flat.reshape(heads, n_off, tile * (span + 1))[:, :, :tile * span]
    skew = flat.reshape(heads, n_off, tile, span)
    tiles = skew[:, :, :, tile - 1:]
    masked = jnp.full((tiles.shape[0], 1) + tiles.shape[2:], MASKED, tiles.dtype)
    return jnp.concatenate([tiles, masked], axis=1)


def _bias_tiles(rel_bias):
    hi = lax.Precision.HIGHEST
    d = np.arange(-A_REACH, A_REACH + 1)
    count = np.zeros(d.shape, np.float64)
    for w, dil in A_PATTERNS:
        count += (d % dil == 0) & (np.abs(d) <= w // 2)
    log_count = np.where(count > 0, np.log(np.maximum(count, 1.0)), 0.0).astype(np.float32)
    table = rel_bias.astype(jnp.float32)
    onehot_a = jnp.asarray(np.eye(NUM_BUCKETS, dtype=np.float32)[_t5_bucket_np(d)])
    vec_a = (jnp.dot(onehot_a, table[:, :A_HEADS], precision=hi).T + log_count[None]) * LOG2E
    vec_a = jnp.where(jnp.asarray(count > 0)[None], vec_a, MASKED)
    tiles_a = _toeplitz_tiles(vec_a, SUB, (-(-A_REACH // A_TILE) + 1) * (A_TILE // SUB) - 1)
    db = np.arange(-B_HALF_WINDOW, B_HALF_WINDOW + 1)
    onehot_b = jnp.asarray(np.eye(NUM_BUCKETS, dtype=np.float32)[_t5_bucket_np(db)])
    vec_b = jnp.dot(onehot_b, table[:, A_HEADS:], precision=hi).T * LOG2E
    tiles_b = _toeplitz_tiles(vec_b, SUB, (-(-B_HALF_WINDOW // B_TILE) + 1) * (B_TILE // SUB) - 1)
    top_a = jnp.max(jnp.abs(jnp.where(jnp.asarray(count > 0)[None], vec_a, 0.0)))
    return tiles_a, tiles_b, top_a, jnp.max(jnp.abs(vec_b))


def kernel(x, c, w_mod, b_mod, pre_norm, post_norm, w_in, w_out, q_norm, k_norm, sink, rel_bias):
    bsz, s, d = x.shape
    assert bsz == 1 and c.shape[0] == 1
    depth = w_in.shape[0]
    assert w_in.shape[2] == IN_WIDTH and w_out.shape[1] == A_W + B_W + C_W
    x2 = x[0]
    cos_t, sin_t = _rope_tables(s)
    tiles_a, tiles_b, top_bias_a, top_bias_b = _bias_tiles(rel_bias)
    mod = _modulation(c, w_mod, b_mod)[:, 0]
    shift, scale, gate = mod[:, :d], mod[:, d:2 * d], mod[:, 2 * d:]
    vec = lambda a, i: a[i][None, :]

    h = _prenorm(x2, vec(pre_norm, 0), vec(scale, 0), vec(shift, 0))
    for i in range(depth):
        proj = _in_projection(h, w_in, i)
        vta, vtb, vtc, kc, sq_norms = _prep(proj, cos_t, sin_t, k_norm[i][:, None])
        sq_norms = jnp.max(sq_norms, axis=(0, 2))
        qk_scale = NORM_SLACK * HEAD_DIM ** -0.5 * LOG2E
        bound_a = qk_scale * jnp.sqrt(sq_norms[0] * sq_norms[1]) + top_bias_a
        bound_b = jnp.maximum(qk_scale * jnp.sqrt(sq_norms[2] * sq_norms[3]) + top_bias_b,
                              jnp.max(jnp.abs(sink[i])) * LOG2E)
        band_a = functools.partial(_band_attention, proj, AK, vta, AQ, AG, A_HEADS, 1, A_TILE,
                                   -(-A_REACH // A_TILE), tiles_a, None)
        band_b = functools.partial(_band_attention, proj, BK, vtb, BQ, BG, B_KV_HEADS, B_HEADS // B_KV_HEADS,
                                   B_TILE, -(-B_HALF_WINDOW // B_TILE), tiles_b, sink[i])
        oa = lax.cond(bound_a <= LOGIT_BOUND, lambda: band_a(True), lambda: band_a(False))
        ob = lax.cond(bound_b <= LOGIT_BOUND, lambda: band_b(True), lambda: band_b(False))
        dense = functools.partial(_dense_attention, proj, kc, vtc, CQ, CG, C_KV_HEADS,
                                  C_HEADS // C_KV_HEADS, C_TK, cos_t, sin_t, q_norm[i][:, None])
        logit_bound = (HEAD_DIM ** 0.5 * LOG2E) * jnp.max(jnp.abs(q_norm[i])) * jnp.max(jnp.abs(k_norm[i]))
        oc = lax.cond(logit_bound <= LOGIT_BOUND, lambda: dense(True), lambda: dense(False))
        y = _out_projection(oa, ob, oc, w_out, i)
        nxt = None if i == depth - 1 else (vec(pre_norm, i + 1), vec(scale, i + 1), vec(shift, i + 1))
        x2, h = _post(x2, y, vec(post_norm, i), vec(gate, i), nxt)
    return x2[None]
```

```python
import functools
import math

import numpy as np
import jax
import jax.numpy as jnp
from jax import lax
from jax.experimental import pallas as pl
from jax.experimental.pallas import tpu as pltpu

HEAD_DIM = 128
GRID_W = 64
A_HEADS = 12
A_PATTERNS = ((128, 1), (512, 4), (2048, 16))
B_HEADS = 8
B_KV_HEADS = 2
B_HALF_WINDOW = 128
C_HEADS = 12
C_KV_HEADS = 4
ROPE_THETA = 10000.0
NUM_BUCKETS = 32
MAX_DISTANCE = 1024
EPS = 1e-6
LOG2E = math.log2(math.e)
MASKED = -1e30
M_INIT = -1e29

A_W = A_HEADS * HEAD_DIM
B_W = B_HEADS * HEAD_DIM
C_W = C_HEADS * HEAD_DIM
AQ, AK, AV, AG = 0, 12, 24, 36
BQ, BK, BV, BG = 48, 56, 58, 60
CQ, CK, CV, CG = 68, 80, 84, 88
IN_WIDTH = 100 * HEAD_DIM

A_REACH = max(w // 2 for w, _ in A_PATTERNS)
A_TILE = 512
B_TILE = 256
C_TQ = {True: 512, False: 256}
C_TK = 1024
BAND_HEADS_PER_STEP = 2
BOUNDED_GROUP = 4
NORM_SLACK = 1.02
LOGIT_BOUND = 64.0
DENSE_UNROLL = 3
VT_ROWS = HEAD_DIM + 16
V7X_VMEM_LIMIT = 56 * 1024 * 1024
SUBLANES = 8
SUB = 128


def _cparams(n_axes):
    return pltpu.CompilerParams(dimension_semantics=("arbitrary",) * n_axes,
                                vmem_limit_bytes=V7X_VMEM_LIMIT)


def _tile(n, pref):
    t = min(n, pref)
    while n % t:
        t //= 2
    return t


def _mod_kernel(c_ref, w_ref, b_ref, o_ref):
    c = c_ref[...]
    cs = jnp.broadcast_to(c * jax.nn.sigmoid(c), (SUBLANES, c.shape[1]))
    r = jnp.dot(cs, w_ref[0], preferred_element_type=jnp.float32)
    o_ref[0] = r[0:1] + b_ref[0]


def _modulation(c, w_mod, b_mod):
    depth, d, n = w_mod.shape
    tn = _tile(n, 1024)
    out = pl.pallas_call(
        _mod_kernel,
        grid=(depth, n // tn),
        in_specs=[pl.BlockSpec((1, d), lambda i, j: (0, 0)),
                  pl.BlockSpec((1, d, tn), lambda i, j: (i, 0, j)),
                  pl.BlockSpec((1, 1, tn), lambda i, j: (i, 0, j))],
        out_specs=pl.BlockSpec((1, 1, tn), lambda i, j: (i, 0, j)),
        out_shape=jax.ShapeDtypeStruct((depth, 1, n), jnp.float32),
        compiler_params=_cparams(2),
        name="modulation",
    )(c, w_mod, b_mod.reshape(depth, 1, n))
    return out


def _rms(x):
    return x * lax.rsqrt(jnp.mean(x * x, axis=-1, keepdims=True) + EPS)


def _prenorm_kernel(x_ref, g_ref, sc_ref, sh_ref, h_ref):
    y = _rms(x_ref[...]) * g_ref[...]
    h_ref[...] = (y * (1.0 + sc_ref[...]) + sh_ref[...]).astype(h_ref.dtype)


def _prenorm(x, g, scale, shift):
    s, d = x.shape
    tm = _tile(s, 512)
    vec = pl.BlockSpec((1, d), lambda i: (0, 0))
    return pl.pallas_call(
        _prenorm_kernel,
        grid=(s // tm,),
        in_specs=[pl.BlockSpec((tm, d), lambda i: (i, 0)), vec, vec, vec],
        out_specs=pl.BlockSpec((tm, d), lambda i: (i, 0)),
        out_shape=jax.ShapeDtypeStruct((s, d), jnp.bfloat16),
        compiler_params=_cparams(1),
        name="prenorm",
    )(x, g, scale, shift)


def _post_kernel(x_ref, y_ref, pg_ref, gate_ref, xo_ref):
    yn = _rms(y_ref[...].astype(jnp.float32)) * pg_ref[...]
    xo_ref[...] = x_ref[...] + gate_ref[...] * yn


def _postpre_kernel(x_ref, y_ref, pg_ref, gate_ref, g_ref, sc_ref, sh_ref, xo_ref, h_ref):
    yn = _rms(y_ref[...].astype(jnp.float32)) * pg_ref[...]
    xn = x_ref[...] + gate_ref[...] * yn
    xo_ref[...] = xn
    h = _rms(xn) * g_ref[...]
    h_ref[...] = (h * (1.0 + sc_ref[...]) + sh_ref[...]).astype(h_ref.dtype)


def _post(x, y, post_g, gate, nxt=None):
    s, d = x.shape
    tm = _tile(s, 256)
    row = pl.BlockSpec((tm, d), lambda i: (i, 0))
    vec = pl.BlockSpec((1, d), lambda i: (0, 0))
    if nxt is None:
        return pl.pallas_call(
            _post_kernel, grid=(s // tm,),
            in_specs=[row, row, vec, vec], out_specs=row,
            out_shape=jax.ShapeDtypeStruct((s, d), jnp.float32),
            compiler_params=_cparams(1), name="postnorm",
        )(x, y, post_g, gate), None
    g, scale, shift = nxt
    return pl.pallas_call(
        _postpre_kernel, grid=(s // tm,),
        in_specs=[row, row, vec, vec, vec, vec, vec], out_specs=[row, row],
        out_shape=[jax.ShapeDtypeStruct((s, d), jnp.float32),
                   jax.ShapeDtypeStruct((s, d), jnp.bfloat16)],
        compiler_params=_cparams(1), name="postnorm_prenorm",
    )(x, y, post_g, gate, g, scale, shift)


def _mm_kernel(a_ref, b_ref, o_ref):
    o_ref[...] = jnp.dot(a_ref[...], b_ref[0].astype(a_ref.dtype),
                         preferred_element_type=jnp.float32).astype(o_ref.dtype)


def _in_projection(h, w, layer):
    m, k = h.shape
    n = w.shape[2]
    tm, tn = _tile(m, 1024), _tile(n, 512)
    return pl.pallas_call(
        _mm_kernel,
        grid=(m // tm, n // tn),
        in_specs=[pl.BlockSpec((tm, k), lambda i, j: (i, 0)),
                  pl.BlockSpec((1, k, tn), lambda i, j: (layer, 0, j))],
        out_specs=pl.BlockSpec((tm, tn), lambda i, j: (i, j)),
        out_shape=jax.ShapeDtypeStruct((m, n), jnp.bfloat16),
        compiler_params=_cparams(2),
        name="in_projection",
    )(h, w)


def _mm3_kernel(a1, a2, a3, w_ref, o_ref, wb_ref):
    @pl.when(pl.program_id(1) == 0)
    def _():
        wb_ref[...] = w_ref[0].astype(wb_ref.dtype)

    acc, row = None, 0
    for a in (a1, a2, a3):
        part = jnp.dot(a[...], wb_ref[row:row + a.shape[1]], preferred_element_type=jnp.float32)
        acc = part if acc is None else acc + part
        row += a.shape[1]
    o_ref[...] = acc.astype(o_ref.dtype)


def _out_projection(oa, ob, oc, w, layer):
    m = oa.shape[0]
    k, n = w.shape[1:]
    tm, tn = _tile(m, 1024), _tile(n, 512)
    a_spec = lambda a: pl.BlockSpec((tm, a.shape[1]), lambda j, i: (i, 0))
    return pl.pallas_call(
        _mm3_kernel,
        grid=(n // tn, m // tm),
        in_specs=[a_spec(oa), a_spec(ob), a_spec(oc),
                  pl.BlockSpec((1, k, tn), lambda j, i: (layer, 0, j))],
        out_specs=pl.BlockSpec((tm, tn), lambda j, i: (i, j)),
        out_shape=jax.ShapeDtypeStruct((m, n), jnp.bfloat16),
        scratch_shapes=[pltpu.VMEM((k, tn), oa.dtype)],
        compiler_params=_cparams(2),
        name="out_projection",
    )(oa, ob, oc, w)


def _rope_tables(s):
    rows = s // GRID_W
    row = jnp.repeat(jnp.arange(rows), GRID_W)
    col = jnp.tile(jnp.arange(GRID_W), rows)
    quarter = HEAD_DIM // 4
    freqs = ROPE_THETA ** (-jnp.arange(0, 2 * quarter, 2, dtype=jnp.float32) / (2 * quarter))
    ar = row.astype(jnp.float32)[:, None] * freqs[None, :]
    ac = col.astype(jnp.float32)[:, None] * freqs[None, :]
    cr, sr, cc, sc = jnp.cos(ar), jnp.sin(ar), jnp.cos(ac), jnp.sin(ac)
    cos = jnp.concatenate([cr, cr, cc, cc], axis=-1)
    sin = jnp.concatenate([-sr, sr, -sc, sc], axis=-1)
    return cos.T, sin.T


def _norm_rope_t(xt, gain_col, cos_t, sin_t):
    quarter = HEAD_DIM // 4
    y = xt * lax.rsqrt(jnp.mean(xt * xt, axis=0, keepdims=True) + EPS) * gain_col
    partner = jnp.concatenate([y[quarter:2 * quarter], y[:quarter], y[3 * quarter:], y[2 * quarter:3 * quarter]],
                              axis=0)
    return y * cos_t + partner * sin_t


def _prep_kernel(av_ref, bv_ref, cv_ref, ck_ref, aq_ref, ak_ref, bq_ref, bk_ref, cos_ref, sin_ref, kg_ref,
                 vta_ref, vtb_ref, vtc_ref, kc_ref, nrm_ref, *, ts):
    def put_t(src_ref, n_heads, dst_ref, chunk):
        for hh in range(n_heads):
            vt = src_ref[:, hh * HEAD_DIM:(hh + 1) * HEAD_DIM].T
            for cidx in range(ts // chunk):
                dst_ref[hh, cidx, :HEAD_DIM] = vt[:, cidx * chunk:(cidx + 1) * chunk]
                dst_ref[hh, cidx, HEAD_DIM:] = jnp.ones((VT_ROWS - HEAD_DIM, chunk), dst_ref.dtype)

    put_t(av_ref, A_HEADS, vta_ref, A_TILE)
    put_t(bv_ref, B_KV_HEADS, vtb_ref, B_TILE)
    put_t(cv_ref, C_KV_HEADS, vtc_ref, C_TK)
    cos_t, sin_t, kg = cos_ref[...], sin_ref[...], kg_ref[...]
    for hh in range(C_KV_HEADS):
        kt = ck_ref[:, hh * HEAD_DIM:(hh + 1) * HEAD_DIM].astype(jnp.float32).T
        k = _norm_rope_t(kt, kg, cos_t, sin_t).T
        kc_ref[:, hh * HEAD_DIM:(hh + 1) * HEAD_DIM] = k.astype(kc_ref.dtype)

    def max_sq_norm(src_ref, n_heads):
        ones = jnp.ones((HEAD_DIM, HEAD_DIM), src_ref.dtype)
        best = None
        for hh in range(n_heads):
            x = src_ref[:, hh * HEAD_DIM:(hh + 1) * HEAD_DIM]
            row_sums = jnp.dot(x * x, ones, preferred_element_type=jnp.float32)
            top = jnp.max(row_sums, axis=0, keepdims=True)
            best = top if best is None else jnp.maximum(best, top)
        return best

    nrm_ref[0] = jnp.concatenate(
        [max_sq_norm(aq_ref, A_HEADS), max_sq_norm(ak_ref, A_HEADS), max_sq_norm(bq_ref, B_HEADS),
         max_sq_norm(bk_ref, B_KV_HEADS), jnp.zeros((SUBLANES - 4, HEAD_DIM), jnp.float32)], axis=0)


def _prep(proj, cos_t, sin_t, k_gain_col):
    s = proj.shape[0]
    ts = max(A_TILE, B_TILE, C_TK)
    assert s % ts == 0
    bf = jnp.bfloat16
    n = s // ts
    vt_spec = lambda heads, t: pl.BlockSpec((heads, ts // t, VT_ROWS, t), lambda i: (0, i, 0, 0))
    vt_shape = lambda heads, t: jax.ShapeDtypeStruct((heads, s // t, VT_ROWS, t), bf)
    return pl.pallas_call(
        functools.partial(_prep_kernel, ts=ts),
        grid=(n,),
        in_specs=[pl.BlockSpec((ts, A_W), lambda i: (i, AV * HEAD_DIM // A_W)),
                  pl.BlockSpec((ts, B_KV_HEADS * HEAD_DIM), lambda i: (i, BV // B_KV_HEADS)),
                  pl.BlockSpec((ts, C_KV_HEADS * HEAD_DIM), lambda i: (i, CV // C_KV_HEADS)),
                  pl.BlockSpec((ts, C_KV_HEADS * HEAD_DIM), lambda i: (i, CK // C_KV_HEADS)),
                  pl.BlockSpec((ts, A_W), lambda i: (i, AQ * HEAD_DIM // A_W)),
                  pl.BlockSpec((ts, A_W), lambda i: (i, AK * HEAD_DIM // A_W)),
                  pl.BlockSpec((ts, B_W), lambda i: (i, BQ * HEAD_DIM // B_W)),
                  pl.BlockSpec((ts, B_KV_HEADS * HEAD_DIM), lambda i: (i, BK // B_KV_HEADS)),
                  pl.BlockSpec((HEAD_DIM, ts), lambda i: (0, i)),
                  pl.BlockSpec((HEAD_DIM, ts), lambda i: (0, i)),
                  pl.BlockSpec((HEAD_DIM, 1), lambda i: (0, 0))],
        out_specs=[vt_spec(A_HEADS, A_TILE), vt_spec(B_KV_HEADS, B_TILE), vt_spec(C_KV_HEADS, C_TK),
                   pl.BlockSpec((ts, C_KV_HEADS * HEAD_DIM), lambda i: (i, 0)),
                   pl.BlockSpec((1, SUBLANES, HEAD_DIM), lambda i: (i, 0, 0))],
        out_shape=[vt_shape(A_HEADS, A_TILE), vt_shape(B_KV_HEADS, B_TILE), vt_shape(C_KV_HEADS, C_TK),
                   jax.ShapeDtypeStruct((s, C_KV_HEADS * HEAD_DIM), bf),
                   jax.ShapeDtypeStruct((n, SUBLANES, HEAD_DIM), jnp.float32)],
        compiler_params=_cparams(1),
        name="vt_krope_prep",
    )(proj, proj, proj, proj, proj, proj, proj, proj, cos_t, sin_t, k_gain_col)


def _scores(k_ref, head, row0, tk, qt_ref, s_ref, mx_ref, slot, bias=None):
    kt = k_ref[pl.ds(pl.multiple_of(row0, tk), tk), head * HEAD_DIM:(head + 1) * HEAD_DIM]
    s = jnp.dot(kt, qt_ref[...], preferred_element_type=jnp.float32)
    if bias is not None:
        s = s + bias
    s_ref[slot] = s
    mx_ref[slot] = jnp.max(s, axis=0, keepdims=True)


def _accumulate(vt, s_ref, mx_ref, m_ref, acc_ref, slot):
    m_old = m_ref[...]
    m_new = jnp.maximum(m_old, mx_ref[slot])
    alpha = jnp.exp2(m_old - m_new)
    p = jnp.exp2(s_ref[slot] - m_new).astype(jnp.bfloat16)
    acc_ref[...] = alpha * acc_ref[...] + jnp.dot(vt, p, preferred_element_type=jnp.float32)
    m_ref[...] = m_new


def _load_q(q_refs, qt_ref, tq, prep_t=lambda qt: qt):
    scale2 = HEAD_DIM ** -0.5 * LOG2E
    for g, q_ref in enumerate(q_refs):
        qt = prep_t(q_ref[...].astype(jnp.float32).T)
        qt_ref[:, g * tq:(g + 1) * tq] = (qt * scale2).astype(qt_ref.dtype)


def _store_gated(acc, g_refs, o_ref, tq, first=0, extra_denominator=None):
    for g, g_ref in enumerate(g_refs):
        sl = slice(g * tq, (g + 1) * tq)
        den = acc[HEAD_DIM:HEAD_DIM + 1, sl]
        if extra_denominator is not None:
            den = den + extra_denominator[g]
        o = (acc[:HEAD_DIM, sl] / den).T
        gate = g_ref[...].astype(jnp.float32)
        cols = slice((first + g) * HEAD_DIM, (first + g + 1) * HEAD_DIM)
        o_ref[:, cols] = (o * (gate * jax.nn.sigmoid(gate))).astype(o_ref.dtype)


def _band_kernel(*refs, heads, group, t, nkv, reach, has_sink, bounded):
    it = iter(refs)
    q_refs = [[next(it) for _ in range(group)] for _ in range(heads)]
    g_refs = [[next(it) for _ in range(group)] for _ in range(heads)]
    k_ref, vt_ref, bias_ref = next(it), next(it), next(it)
    sink_ref = next(it) if has_sink else None
    o_ref = next(it)
    qt_ref, s_ref, mx_ref, m_ref, acc_ref = (next(it) for _ in range(5))
    h = pl.program_id(0)
    qb = pl.program_id(1)
    n_off = 2 * reach + 1

    nsb = t // SUB
    n_sub = bias_ref.shape[1] - 1

    def block(o):
        kb = qb + (o - reach)
        return jnp.clip(kb, 0, nkv - 1), jnp.logical_and(kb >= 0, kb < nkv)

    def bias_tile(hh, o, valid):
        rows = []
        for a in range(nsb):
            cols = []
            for g in range(group):
                for b in range(nsb):
                    sub = (o - reach) * nsb + a - b + (n_sub - 1) // 2
                    cols.append(bias_ref[hh * group + g, jnp.where(valid, sub, n_sub)])
            rows.append(jnp.concatenate(cols, axis=1))
        return jnp.concatenate(rows, axis=0)

    if bounded:
        for hh in range(heads):
            _load_q(q_refs[hh], qt_ref.at[hh], t)
        width = min(n_off, nkv)
        start = jnp.clip(qb - reach, 0, nkv - width)

        def bias_window(hh):
            rows = []
            for a in range(width * nsb):
                cols = []
                for g in range(group):
                    for b in range(nsb):
                        sub = (start - qb) * nsb + a - b + (n_sub - 1) // 2
                        inside = jnp.logical_and(sub >= 0, sub < n_sub)
                        cols.append(bias_ref[hh * group + g, jnp.where(inside, sub, n_sub)])
                rows.append(jnp.concatenate(cols, axis=1))
            return jnp.concatenate(rows, axis=0)

        for hh in range(heads):
            keys = k_ref[pl.ds(pl.multiple_of(start * t, t), width * t), hh * HEAD_DIM:(hh + 1) * HEAD_DIM]
            s = jnp.dot(keys, qt_ref[hh], preferred_element_type=jnp.float32) + bias_window(hh)
            p = jnp.exp2(s).astype(jnp.bfloat16)
            total = None
            for o in range(width):
                part = jnp.dot(vt_ref[hh, start + o], p[o * t:(o + 1) * t],
                               preferred_element_type=jnp.float32)
                total = part if total is None else total + part
            sinks = None
            if has_sink:
                sinks = [jnp.exp2(jnp.full((1, t), sink_ref[(h * heads + hh) * group + g] * LOG2E, jnp.float32))
                         for g in range(group)]
            _store_gated(total, g_refs[hh], o_ref, t, hh * group, sinks)
        return

    for hh in range(heads):
        _load_q(q_refs[hh], qt_ref.at[hh], t)
        acc_ref[hh, :HEAD_DIM] = jnp.zeros((HEAD_DIM, group * t), jnp.float32)
        if has_sink:
            for g in range(group):
                sink = sink_ref[(h * heads + hh) * group + g] * LOG2E
                m_ref[hh, :, g * t:(g + 1) * t] = jnp.full((1, t), sink, jnp.float32)
            acc_ref[hh, HEAD_DIM:] = jnp.ones((VT_ROWS - HEAD_DIM, group * t), jnp.float32)
        else:
            m_ref[hh] = jnp.full((1, group * t), M_INIT, jnp.float32)
            acc_ref[hh, HEAD_DIM:] = jnp.zeros((VT_ROWS - HEAD_DIM, group * t), jnp.float32)

    units = [(hh, o) for hh in range(heads) for o in range(n_off)]

    def scores(u):
        hh, o = units[u]
        kb, valid = block(o)
        _scores(k_ref, hh, kb * t, t, qt_ref.at[hh], s_ref, mx_ref, u % 2, bias_tile(hh, o, valid))

    scores(0)
    for u, (hh, o) in enumerate(units):
        if u + 1 < len(units):
            scores(u + 1)
        _accumulate(vt_ref[hh, block(o)[0]], s_ref, mx_ref, m_ref.at[hh], acc_ref.at[hh], u % 2)
        if o == n_off - 1:
            _store_gated(acc_ref.at[hh], g_refs[hh], o_ref, t, hh * group)


def _dense_kernel(*refs, group, tq, tk, nkv, bounded):
    it = iter(refs)
    q_refs = [next(it) for _ in range(group)]
    g_refs = [next(it) for _ in range(group)]
    k_ref, vt_ref, cos_ref, sin_ref, qg_ref, o_ref = (next(it) for _ in range(6))
    qt_ref, s_ref, mx_ref, m_ref, acc_ref = (next(it) for _ in range(5))
    qt_ref, m_ref, acc_ref = qt_ref.at[0], m_ref.at[0], acc_ref.at[0]

    _load_q(q_refs, qt_ref, tq, lambda xt: _norm_rope_t(xt, qg_ref[...], cos_ref[...], sin_ref[...]))
    acc_ref[...] = jnp.zeros(acc_ref.shape, jnp.float32)

    if bounded:
        def block(jblk):
            kt = k_ref[pl.ds(pl.multiple_of(jblk * tk, tk), tk), :]
            s = jnp.dot(kt, qt_ref[...], preferred_element_type=jnp.float32)
            return jnp.dot(vt_ref[0, jblk], jnp.exp2(s).astype(jnp.bfloat16),
                           preferred_element_type=jnp.float32)

        per_iter = math.gcd(nkv, BOUNDED_GROUP)

        def blocks(it, carry):
            part = block(it * per_iter)
            for u in range(1, per_iter):
                part = part + block(it * per_iter + u)
            acc_ref[...] += part
            return carry

        lax.fori_loop(0, nkv // per_iter, blocks, 0)
        _store_gated(acc_ref, g_refs, o_ref, tq)
        return

    m_ref[...] = jnp.full(m_ref.shape, M_INIT, jnp.float32)

    def scores(jblk, slot):
        _scores(k_ref, 0, jblk * tk, tk, qt_ref, s_ref, mx_ref, slot)

    def accumulate(jblk, slot):
        _accumulate(vt_ref[0, jblk], s_ref, mx_ref, m_ref, acc_ref, slot)

    scores(0, 0)

    def pair(j0):
        scores(j0 + 1, 1)
        accumulate(j0, 0)
        scores(j0 + 2, 0)
        accumulate(j0 + 1, 1)

    def pairs(it, carry):
        for u in range(DENSE_UNROLL):
            pair(2 * (it * DENSE_UNROLL + u))
        return carry

    n_pairs = nkv // 2 - 1
    lax.fori_loop(0, n_pairs // DENSE_UNROLL, pairs, 0)
    for jj in range(n_pairs - n_pairs % DENSE_UNROLL, n_pairs):
        pair(2 * jj)
    scores(nkv - 1, 1)
    accumulate(nkv - 2, 0)
    accumulate(nkv - 1, 1)
    _store_gated(acc_ref, g_refs, o_ref, tq)


def _attention(kern, proj, k_src, k_col, vt, q_col, g_col, n_kv_heads, group, tq, tk, extra_specs,
               extra_args, name, heads=1, bounded=False):
    s = proj.shape[0]
    parked = SUBLANES if bounded else tk
    assert s % tq == 0 and s % tk == 0 and n_kv_heads % heads == 0 and k_col % heads == 0
    in_specs, args = [], []
    for col in (q_col, g_col):
        for hh in range(heads):
            for g in range(group):
                in_specs.append(pl.BlockSpec(
                    (tq, HEAD_DIM), lambda h, qb, col=col, hh=hh, g=g: (qb, col + (h * heads + hh) * group + g)))
                args.append(proj)
    in_specs += [pl.BlockSpec((s, heads * HEAD_DIM), lambda h, qb: (0, k_col // heads + h)),
                 pl.BlockSpec((heads, s // tk, VT_ROWS, tk), lambda h, qb: (h, 0, 0, 0))] + extra_specs
    args += [k_src, vt] + extra_args
    n = group * tq
    return pl.pallas_call(
        kern,
        grid=(n_kv_heads // heads, s // tq),
        in_specs=in_specs,
        out_specs=pl.BlockSpec((tq, heads * group * HEAD_DIM), lambda h, qb: (qb, h)),
        out_shape=jax.ShapeDtypeStruct((s, n_kv_heads * group * HEAD_DIM), jnp.bfloat16),
        scratch_shapes=[pltpu.VMEM((heads, HEAD_DIM, n), jnp.bfloat16),
                        pltpu.VMEM((2, parked, n), jnp.float32),
                        pltpu.VMEM((2, 1, n), jnp.float32),
                        pltpu.VMEM((heads, 1, n), jnp.float32),
                        pltpu.VMEM((heads, VT_ROWS, n), jnp.float32)],
        compiler_params=_cparams(2),
        name=name,
    )(*args)


def _band_attention(proj, k_col, vt, q_col, g_col, n_kv_heads, group, t, reach, bias, sink, bounded,
                    heads=BAND_HEADS_PER_STEP):
    nkv = proj.shape[0] // t
    extra_specs = [pl.BlockSpec((heads * group,) + bias.shape[1:], lambda h, qb: (h, 0, 0, 0))]
    extra_args = [bias]
    if sink is not None:
        extra_specs.append(pl.BlockSpec(memory_space=pltpu.SMEM))
        extra_args.append(sink)
    kern = functools.partial(_band_kernel, heads=heads, group=group, t=t, nkv=nkv, reach=reach,
                             has_sink=sink is not None, bounded=bounded)
    name = "band_attention_g%d%s" % (group, "_bounded" if bounded else "")
    return _attention(kern, proj, proj, k_col, vt, q_col, g_col, n_kv_heads, group, t, t,
                      extra_specs, extra_args, name, heads, bounded)


def _dense_attention(proj, kc, vt, q_col, g_col, n_kv_heads, group, tk, cos_t, sin_t, q_gain_col, bounded):
    nkv = proj.shape[0] // tk
    tq = C_TQ[bounded]
    assert nkv % 2 == 0
    extra_specs = [pl.BlockSpec((HEAD_DIM, tq), lambda h, qb: (0, qb)),
                   pl.BlockSpec((HEAD_DIM, tq), lambda h, qb: (0, qb)),
                   pl.BlockSpec((HEAD_DIM, 1), lambda h, qb: (0, 0))]
    kern = functools.partial(_dense_kernel, group=group, tq=tq, tk=tk, nkv=nkv, bounded=bounded)
    name = "dense_attention_g%d%s" % (group, "_bounded" if bounded else "")
    return _attention(kern, proj, kc, 0, vt, q_col, g_col, n_kv_heads, group, tq, tk,
                      extra_specs, [cos_t, sin_t, q_gain_col], name, 1, bounded)


def _t5_bucket_np(rel):
    half_b = NUM_BUCKETS // 2
    max_exact = half_b // 2
    ret = np.where(rel > 0, half_b, 0)
    n = np.abs(rel)
    nf = np.maximum(n, 1).astype(np.float32)
    large = max_exact + (np.log(nf / np.float32(max_exact)) / np.float32(math.log(MAX_DISTANCE / max_exact))
                         * np.float32(half_b - max_exact)).astype(np.int32)
    large = np.minimum(large, half_b - 1)
    return (ret + np.where(n < max_exact, n, large)).astype(np.int32)


def _toeplitz_tiles(vec, tile, reach):
    heads = vec.shape[0]
    r = (vec.shape[1] - 1) // 2
    n_off = 2 * reach + 1
    span = 2 * tile - 1
    lo = -reach * tile - (tile - 1)
    hi = reach * tile + (tile - 1)
    padded = jnp.pad(vec, ((0, 0), (-lo - r, hi - r)), constant_values=MASKED)
    rows = [padded[:, o * tile:o * tile + span] for o in range(n_off)]
    u = jnp.stack(rows, axis=1)[:, :, ::-1]
    w = jnp.pad(u, ((0, 0), (0, 0), (0, 1)))
    flat = jnp.broadcast_to(w[:, :, None, :], (heads, n_off, tile, span + 1))
    flat = flat.reshape(heads, n_off, tile * (span + 1))[:, :, :tile * span]
    skew = flat.reshape(heads, n_off, tile, span)
    tiles = skew[:, :, :, tile - 1:]
    masked = jnp.full((tiles.shape[0], 1) + tiles.shape[2:], MASKED, tiles.dtype)
    return jnp.concatenate([tiles, masked], axis=1)


def _bias_tiles(rel_bias):
    hi = lax.Precision.HIGHEST
    d = np.arange(-A_REACH, A_REACH + 1)
    count = np.zeros(d.shape, np.float64)
    for w, dil in A_PATTERNS:
        count += (d % dil == 0) & (np.abs(d) <= w // 2)
    log_count = np.where(count > 0, np.log(np.maximum(count, 1.0)), 0.0).astype(np.float32)
    table = rel_bias.astype(jnp.float32)
    onehot_a = jnp.asarray(np.eye(NUM_BUCKETS, dtype=np.float32)[_t5_bucket_np(d)])
    vec_a = (jnp.dot(onehot_a, table[:, :A_HEADS], precision=hi).T + log_count[None]) * LOG2E
    vec_a = jnp.where(jnp.asarray(count > 0)[None], vec_a, MASKED)
    tiles_a = _toeplitz_tiles(vec_a, SUB, (-(-A_REACH // A_TILE) + 1) * (A_TILE // SUB) - 1)
    db = np.arange(-B_HALF_WINDOW, B_HALF_WINDOW + 1)
    onehot_b = jnp.asarray(np.eye(NUM_BUCKETS, dtype=np.float32)[_t5_bucket_np(db)])
    vec_b = jnp.dot(onehot_b, table[:, A_HEADS:], precision=hi).T * LOG2E
    tiles_b = _toeplitz_tiles(vec_b, SUB, (-(-B_HALF_WINDOW // B_TILE) + 1) * (B_TILE // SUB) - 1)
    top_a = jnp.max(jnp.abs(jnp.where(jnp.asarray(count > 0)[None], vec_a, 0.0)))
    return tiles_a, tiles_b, top_a, jnp.max(jnp.abs(vec_b))


def kernel(x, c, w_mod, b_mod, pre_norm, post_norm, w_in, w_out, q_norm, k_norm, sink, rel_bias):
    bsz, s, d = x.shape
    assert bsz == 1 and c.shape[0] == 1
    depth = w_in.shape[0]
    assert w_in.shape[2] == IN_WIDTH and w_out.shape[1] == A_W + B_W + C_W
    x2 = x[0]
    cos_t, sin_t = _rope_tables(s)
    tiles_a, tiles_b, top_bias_a, top_bias_b = _bias_tiles(rel_bias)
    mod = _modulation(c, w_mod, b_mod)[:, 0]
    shift, scale, gate = mod[:, :d], mod[:, d:2 * d], mod[:, 2 * d:]
    vec = lambda a, i: a[i][None, :]

    h = _prenorm(x2, vec(pre_norm, 0), vec(scale, 0), vec(shift, 0))
    for i in range(depth):
        proj = _in_projection(h, w_in, i)
        vta, vtb, vtc, kc, sq_norms = _prep(proj, cos_t, sin_t, k_norm[i][:, None])
        sq_norms = jnp.max(sq_norms, axis=(0, 2))
        qk_scale = NORM_SLACK * HEAD_DIM ** -0.5 * LOG2E
        bound_a = qk_scale * jnp.sqrt(sq_norms[0] * sq_norms[1]) + top_bias_a
        bound_b = jnp.maximum(qk_scale * jnp.sqrt(sq_norms[2] * sq_norms[3]) + top_bias_b,
                              jnp.max(jnp.abs(sink[i])) * LOG2E)
        band_a = functools.partial(_band_attention, proj, AK, vta, AQ, AG, A_HEADS, 1, A_TILE,
                                   -(-A_REACH // A_TILE), tiles_a, None)
        band_b = functools.partial(_band_attention, proj, BK, vtb, BQ, BG, B_KV_HEADS, B_HEADS // B_KV_HEADS,
                                   B_TILE, -(-B_HALF_WINDOW // B_TILE), tiles_b, sink[i])
        oa = lax.cond(bound_a <= LOGIT_BOUND, lambda: band_a(True), lambda: band_a(False))
        ob = lax.cond(bound_b <= LOGIT_BOUND, lambda: band_b(True), lambda: band_b(False))
        dense = functools.partial(_dense_attention, proj, kc, vtc, CQ, CG, C_KV_HEADS,
                                  C_HEADS // C_KV_HEADS, C_TK, cos_t, sin_t, q_norm[i][:, None])
        logit_bound = (HEAD_DIM ** 0.5 * LOG2E) * jnp.max(jnp.abs(q_norm[i])) * jnp.max(jnp.abs(k_norm[i]))
        oc = lax.cond(logit_bound <= LOGIT_BOUND, lambda: dense(True), lambda: dense(False))
        y = _out_projection(oa, ob, oc, w_out, i)
        nxt = None if i == depth - 1 else (vec(pre_norm, i + 1), vec(scale, i + 1), vec(shift, i + 1))
        x2, h = _post(x2, y, vec(post_norm, i), vec(gate, i), nxt)
    return x2[None]
```

```python
import functools
import math

import numpy as np
import jax
import jax.numpy as jnp
from jax import lax
from jax.experimental import pallas as pl
from jax.experimental.pallas import tpu as pltpu

HEAD_DIM = 128
GRID_W = 64
A_HEADS = 12
A_PATTERNS = ((128, 1), (512, 4), (2048, 16))
B_HEADS = 8
B_KV_HEADS = 2
B_HALF_WINDOW = 128
C_HEADS = 12
C_KV_HEADS = 4
ROPE_THETA = 10000.0
NUM_BUCKETS = 32
MAX_DISTANCE = 1024
EPS = 1e-6
LOG2E = math.log2(math.e)
MASKED = -1e30
M_INIT = -1e29

A_W = A_HEADS * HEAD_DIM
B_W = B_HEADS * HEAD_DIM
C_W = C_HEADS * HEAD_DIM
AQ, AK, AV, AG = 0, 12, 24, 36
BQ, BK, BV, BG = 48, 56, 58, 60
CQ, CK, CV, CG = 68, 80, 84, 88
IN_WIDTH = 100 * HEAD_DIM

A_REACH = max(w // 2 for w, _ in A_PATTERNS)
A_TILE = 512
B_TILE = 256
C_TQ = {True: 512, False: 256}
C_TK = 1024
BAND_HEADS_PER_STEP = 2
BOUNDED_GROUP = 4
NORM_SLACK = 1.02
LOGIT_BOUND = -1.0
DENSE_UNROLL = 3
VT_ROWS = HEAD_DIM + 16
V7X_VMEM_LIMIT = 56 * 1024 * 1024
SUBLANES = 8
SUB = 128


def _cparams(n_axes):
    return pltpu.CompilerParams(dimension_semantics=("arbitrary",) * n_axes,
                                vmem_limit_bytes=V7X_VMEM_LIMIT)


def _tile(n, pref):
    t = min(n, pref)
    while n % t:
        t //= 2
    return t


def _mod_kernel(c_ref, w_ref, b_ref, o_ref):
    c = c_ref[...]
    cs = jnp.broadcast_to(c * jax.nn.sigmoid(c), (SUBLANES, c.shape[1]))
    r = jnp.dot(cs, w_ref[0], preferred_element_type=jnp.float32)
    o_ref[0] = r[0:1] + b_ref[0]


def _modulation(c, w_mod, b_mod):
    depth, d, n = w_mod.shape
    tn = _tile(n, 1024)
    out = pl.pallas_call(
        _mod_kernel,
        grid=(depth, n // tn),
        in_specs=[pl.BlockSpec((1, d), lambda i, j: (0, 0)),
                  pl.BlockSpec((1, d, tn), lambda i, j: (i, 0, j)),
                  pl.BlockSpec((1, 1, tn), lambda i, j: (i, 0, j))],
        out_specs=pl.BlockSpec((1, 1, tn), lambda i, j: (i, 0, j)),
        out_shape=jax.ShapeDtypeStruct((depth, 1, n), jnp.float32),
        compiler_params=_cparams(2),
        name="modulation",
    )(c, w_mod, b_mod.reshape(depth, 1, n))
    return out


def _rms(x):
    return x * lax.rsqrt(jnp.mean(x * x, axis=-1, keepdims=True) + EPS)


def _prenorm_kernel(x_ref, g_ref, sc_ref, sh_ref, h_ref):
    y = _rms(x_ref[...]) * g_ref[...]
    h_ref[...] = (y * (1.0 + sc_ref[...]) + sh_ref[...]).astype(h_ref.dtype)


def _prenorm(x, g, scale, shift):
    s, d = x.shape
    tm = _tile(s, 512)
    vec = pl.BlockSpec((1, d), lambda i: (0, 0))
    return pl.pallas_call(
        _prenorm_kernel,
        grid=(s // tm,),
        in_specs=[pl.BlockSpec((tm, d), lambda i: (i, 0)), vec, vec, vec],
        out_specs=pl.BlockSpec((tm, d), lambda i: (i, 0)),
        out_shape=jax.ShapeDtypeStruct((s, d), jnp.bfloat16),
        compiler_params=_cparams(1),
        name="prenorm",
    )(x, g, scale, shift)


def _post_kernel(x_ref, y_ref, pg_ref, gate_ref, xo_ref):
    yn = _rms(y_ref[...].astype(jnp.float32)) * pg_ref[...]
    xo_ref[...] = x_ref[...] + gate_ref[...] * yn


def _postpre_kernel(x_ref, y_ref, pg_ref, gate_ref, g_ref, sc_ref, sh_ref, xo_ref, h_ref):
    yn = _rms(y_ref[...].astype(jnp.float32)) * pg_ref[...]
    xn = x_ref[...] + gate_ref[...] * yn
    xo_ref[...] = xn
    h = _rms(xn) * g_ref[...]
    h_ref[...] = (h * (1.0 + sc_ref[...]) + sh_ref[...]).astype(h_ref.dtype)


def _post(x, y, post_g, gate, nxt=None):
    s, d = x.shape
    tm = _tile(s, 256)
    row = pl.BlockSpec((tm, d), lambda i: (i, 0))
    vec = pl.BlockSpec((1, d), lambda i: (0, 0))
    if nxt is None:
        return pl.pallas_call(
            _post_kernel, grid=(s // tm,),
            in_specs=[row, row, vec, vec], out_specs=row,
            out_shape=jax.ShapeDtypeStruct((s, d), jnp.float32),
            compiler_params=_cparams(1), name="postnorm",
        )(x, y, post_g, gate), None
    g, scale, shift = nxt
    return pl.pallas_call(
        _postpre_kernel, grid=(s // tm,),
        in_specs=[row, row, vec, vec, vec, vec, vec], out_specs=[row, row],
        out_shape=[jax.ShapeDtypeStruct((s, d), jnp.float32),
                   jax.ShapeDtypeStruct((s, d), jnp.bfloat16)],
        compiler_params=_cparams(1), name="postnorm_prenorm",
    )(x, y, post_g, gate, g, scale, shift)


def _mm_kernel(a_ref, b_ref, o_ref):
    o_ref[...] = jnp.dot(a_ref[...], b_ref[0].astype(a_ref.dtype),
                         preferred_element_type=jnp.float32).astype(o_ref.dtype)


def _in_projection(h, w, layer):
    m, k = h.shape
    n = w.shape[2]
    tm, tn = _tile(m, 1024), _tile(n, 512)
    return pl.pallas_call(
        _mm_kernel,
        grid=(m // tm, n // tn),
        in_specs=[pl.BlockSpec((tm, k), lambda i, j: (i, 0)),
                  pl.BlockSpec((1, k, tn), lambda i, j: (layer, 0, j))],
        out_specs=pl.BlockSpec((tm, tn), lambda i, j: (i, j)),
        out_shape=jax.ShapeDtypeStruct((m, n), jnp.bfloat16),
        compiler_params=_cparams(2),
        name="in_projection",
    )(h, w)


def _mm3_kernel(a1, a2, a3, w_ref, o_ref, wb_ref):
    @pl.when(pl.program_id(1) == 0)
    def _():
        wb_ref[...] = w_ref[0].astype(wb_ref.dtype)

    acc, row = None, 0
    for a in (a1, a2, a3):
        part = jnp.dot(a[...], wb_ref[row:row + a.shape[1]], preferred_element_type=jnp.float32)
        acc = part if acc is None else acc + part
        row += a.shape[1]
    o_ref[...] = acc.astype(o_ref.dtype)


def _out_projection(oa, ob, oc, w, layer):
    m = oa.shape[0]
    k, n = w.shape[1:]
    tm, tn = _tile(m, 1024), _tile(n, 512)
    a_spec = lambda a: pl.BlockSpec((tm, a.shape[1]), lambda j, i: (i, 0))
    return pl.pallas_call(
        _mm3_kernel,
        grid=(n // tn, m // tm),
        in_specs=[a_spec(oa), a_spec(ob), a_spec(oc),
                  pl.BlockSpec((1, k, tn), lambda j, i: (layer, 0, j))],
        out_specs=pl.BlockSpec((tm, tn), lambda j, i: (i, j)),
        out_shape=jax.ShapeDtypeStruct((m, n), jnp.bfloat16),
        scratch_shapes=[pltpu.VMEM((k, tn), oa.dtype)],
        compiler_params=_cparams(2),
        name="out_projection",
    )(oa, ob, oc, w)


def _rope_tables(s):
    rows = s // GRID_W
    row = jnp.repeat(jnp.arange(rows), GRID_W)
    col = jnp.tile(jnp.arange(GRID_W), rows)
    quarter = HEAD_DIM // 4
    freqs = ROPE_THETA ** (-jnp.arange(0, 2 * quarter, 2, dtype=jnp.float32) / (2 * quarter))
    ar = row.astype(jnp.float32)[:, None] * freqs[None, :]
    ac = col.astype(jnp.float32)[:, None] * freqs[None, :]
    cr, sr, cc, sc = jnp.cos(ar), jnp.sin(ar), jnp.cos(ac), jnp.sin(ac)
    cos = jnp.concatenate([cr, cr, cc, cc], axis=-1)
    sin = jnp.concatenate([-sr, sr, -sc, sc], axis=-1)
    return cos.T, sin.T


def _norm_rope_t(xt, gain_col, cos_t, sin_t):
    quarter = HEAD_DIM // 4
    y = xt * lax.rsqrt(jnp.mean(xt * xt, axis=0, keepdims=True) + EPS) * gain_col
    partner = jnp.concatenate([y[quarter:2 * quarter], y[:quarter], y[3 * quarter:], y[2 * quarter:3 * quarter]],
                              axis=0)
    return y * cos_t + partner * sin_t


def _prep_kernel(av_ref, bv_ref, cv_ref, ck_ref, aq_ref, ak_ref, bq_ref, bk_ref, cos_ref, sin_ref, kg_ref,
                 vta_ref, vtb_ref, vtc_ref, kc_ref, nrm_ref, *, ts):
    def put_t(src_ref, n_heads, dst_ref, chunk):
        for hh in range(n_heads):
            vt = src_ref[:, hh * HEAD_DIM:(hh + 1) * HEAD_DIM].T
            for cidx in range(ts // chunk):
                dst_ref[hh, cidx, :HEAD_DIM] = vt[:, cidx * chunk:(cidx + 1) * chunk]
                dst_ref[hh, cidx, HEAD_DIM:] = jnp.ones((VT_ROWS - HEAD_DIM, chunk), dst_ref.dtype)

    put_t(av_ref, A_HEADS, vta_ref, A_TILE)
    put_t(bv_ref, B_KV_HEADS, vtb_ref, B_TILE)
    put_t(cv_ref, C_KV_HEADS, vtc_ref, C_TK)
    cos_t, sin_t, kg = cos_ref[...], sin_ref[...], kg_ref[...]
    for hh in range(C_KV_HEADS):
        kt = ck_ref[:, hh * HEAD_DIM:(hh + 1) * HEAD_DIM].astype(jnp.float32).T
        k = _norm_rope_t(kt, kg, cos_t, sin_t).T
        kc_ref[:, hh * HEAD_DIM:(hh + 1) * HEAD_DIM] = k.astype(kc_ref.dtype)

    def max_sq_norm(src_ref, n_heads):
        ones = jnp.ones((HEAD_DIM, HEAD_DIM), src_ref.dtype)
        best = None
        for hh in range(n_heads):
            x = src_ref[:, hh * HEAD_DIM:(hh + 1) * HEAD_DIM]
            row_sums = jnp.dot(x * x, ones, preferred_element_type=jnp.float32)
            top = jnp.max(row_sums, axis=0, keepdims=True)
            best = top if best is None else jnp.maximum(best, top)
        return best

    nrm_ref[0] = jnp.concatenate(
        [max_sq_norm(aq_ref, A_HEADS), max_sq_norm(ak_ref, A_HEADS), max_sq_norm(bq_ref, B_HEADS),
         max_sq_norm(bk_ref, B_KV_HEADS), jnp.zeros((SUBLANES - 4, HEAD_DIM), jnp.float32)], axis=0)


def _prep(proj, cos_t, sin_t, k_gain_col):
    s = proj.shape[0]
    ts = max(A_TILE, B_TILE, C_TK)
    assert s % ts == 0
    bf = jnp.bfloat16
    n = s // ts
    vt_spec = lambda heads, t: pl.BlockSpec((heads, ts // t, VT_ROWS, t), lambda i: (0, i, 0, 0))
    vt_shape = lambda heads, t: jax.ShapeDtypeStruct((heads, s // t, VT_ROWS, t), bf)
    return pl.pallas_call(
        functools.partial(_prep_kernel, ts=ts),
        grid=(n,),
        in_specs=[pl.BlockSpec((ts, A_W), lambda i: (i, AV * HEAD_DIM // A_W)),
                  pl.BlockSpec((ts, B_KV_HEADS * HEAD_DIM), lambda i: (i, BV // B_KV_HEADS)),
                  pl.BlockSpec((ts, C_KV_HEADS * HEAD_DIM), lambda i: (i, CV // C_KV_HEADS)),
                  pl.BlockSpec((ts, C_KV_HEADS * HEAD_DIM), lambda i: (i, CK // C_KV_HEADS)),
                  pl.BlockSpec((ts, A_W), lambda i: (i, AQ * HEAD_DIM // A_W)),
                  pl.BlockSpec((ts, A_W), lambda i: (i, AK * HEAD_DIM // A_W)),
                  pl.BlockSpec((ts, B_W), lambda i: (i, BQ * HEAD_DIM // B_W)),
                  pl.BlockSpec((ts, B_KV_HEADS * HEAD_DIM), lambda i: (i, BK // B_KV_HEADS)),
                  pl.BlockSpec((HEAD_DIM, ts), lambda i: (0, i)),
                  pl.BlockSpec((HEAD_DIM, ts), lambda i: (0, i)),
                  pl.BlockSpec((HEAD_DIM, 1), lambda i: (0, 0))],
        out_specs=[vt_spec(A_HEADS, A_TILE), vt_spec(B_KV_HEADS, B_TILE), vt_spec(C_KV_HEADS, C_TK),
                   pl.BlockSpec((ts, C_KV_HEADS * HEAD_DIM), lambda i: (i, 0)),
                   pl.BlockSpec((1, SUBLANES, HEAD_DIM), lambda i: (i, 0, 0))],
        out_shape=[vt_shape(A_HEADS, A_TILE), vt_shape(B_KV_HEADS, B_TILE), vt_shape(C_KV_HEADS, C_TK),
                   jax.ShapeDtypeStruct((s, C_KV_HEADS * HEAD_DIM), bf),
                   jax.ShapeDtypeStruct((n, SUBLANES, HEAD_DIM), jnp.float32)],
        compiler_params=_cparams(1),
        name="vt_krope_prep",
    )(proj, proj, proj, proj, proj, proj, proj, proj, cos_t, sin_t, k_gain_col)


def _scores(k_ref, head, row0, tk, qt_ref, s_ref, mx_ref, slot, bias=None):
    kt = k_ref[pl.ds(pl.multiple_of(row0, tk), tk), head * HEAD_DIM:(head + 1) * HEAD_DIM]
    s = jnp.dot(kt, qt_ref[...], preferred_element_type=jnp.float32)
    if bias is not None:
        s = s + bias
    s_ref[slot] = s
    mx_ref[slot] = jnp.max(s, axis=0, keepdims=True)


def _accumulate(vt, s_ref, mx_ref, m_ref, acc_ref, slot):
    m_old = m_ref[...]
    m_new = jnp.maximum(m_old, mx_ref[slot])
    alpha = jnp.exp2(m_old - m_new)
    p = jnp.exp2(s_ref[slot] - m_new).astype(jnp.bfloat16)
    acc_ref[...] = alpha * acc_ref[...] + jnp.dot(vt, p, preferred_element_type=jnp.float32)
    m_ref[...] = m_new


def _load_q(q_refs, qt_ref, tq, prep_t=lambda qt: qt):
    scale2 = HEAD_DIM ** -0.5 * LOG2E
    for g, q_ref in enumerate(q_refs):
        qt = prep_t(q_ref[...].astype(jnp.float32).T)
        qt_ref[:, g * tq:(g + 1) * tq] = (qt * scale2).astype(qt_ref.dtype)


def _store_gated(acc, g_refs, o_ref, tq, first=0, extra_denominator=None):
    for g, g_ref in enumerate(g_refs):
        sl = slice(g * tq, (g + 1) * tq)
        den = acc[HEAD_DIM:HEAD_DIM + 1, sl]
        if extra_denominator is not None:
            den = den + extra_denominator[g]
        o = (acc[:HEAD_DIM, sl] / den).T
        gate = g_ref[...].astype(jnp.float32)
        cols = slice((first + g) * HEAD_DIM, (first + g + 1) * HEAD_DIM)
        o_ref[:, cols] = (o * (gate * jax.nn.sigmoid(gate))).astype(o_ref.dtype)


def _band_kernel(*refs, heads, group, t, nkv, reach, has_sink, bounded):
    it = iter(refs)
    q_refs = [[next(it) for _ in range(group)] for _ in range(heads)]
    g_refs = [[next(it) for _ in range(group)] for _ in range(heads)]
    k_ref, vt_ref, bias_ref = next(it), next(it), next(it)
    sink_ref = next(it) if has_sink else None
    o_ref = next(it)
    qt_ref, s_ref, mx_ref, m_ref, acc_ref = (next(it) for _ in range(5))
    h = pl.program_id(0)
    qb = pl.program_id(1)
    n_off = 2 * reach + 1

    nsb = t // SUB
    n_sub = bias_ref.shape[1] - 1

    def block(o):
        kb = qb + (o - reach)
        return jnp.clip(kb, 0, nkv - 1), jnp.logical_and(kb >= 0, kb < nkv)

    def bias_tile(hh, o, valid):
        rows = []
        for a in range(nsb):
            cols = []
            for g in range(group):
                for b in range(nsb):
                    sub = (o - reach) * nsb + a - b + (n_sub - 1) // 2
                    cols.append(bias_ref[hh * group + g, jnp.where(valid, sub, n_sub)])
            rows.append(jnp.concatenate(cols, axis=1))
        return jnp.concatenate(rows, axis=0)

    if bounded:
        for hh in range(heads):
            _load_q(q_refs[hh], qt_ref.at[hh], t)
        width = min(n_off, nkv)
        start = jnp.clip(qb - reach, 0, nkv - width)

        def bias_window(hh):
            rows = []
            for a in range(width * nsb):
                cols = []
                for g in range(group):
                    for b in range(nsb):
                        sub = (start - qb) * nsb + a - b + (n_sub - 1) // 2
                        inside = jnp.logical_and(sub >= 0, sub < n_sub)
                        cols.append(bias_ref[hh * group + g, jnp.where(inside, sub, n_sub)])
                rows.append(jnp.concatenate(cols, axis=1))
            return jnp.concatenate(rows, axis=0)

        for hh in range(heads):
            keys = k_ref[pl.ds(pl.multiple_of(start * t, t), width * t), hh * HEAD_DIM:(hh + 1) * HEAD_DIM]
            s = jnp.dot(keys, qt_ref[hh], preferred_element_type=jnp.float32) + bias_window(hh)
            p = jnp.exp2(s).astype(jnp.bfloat16)
            total = None
            for o in range(width):
                part = jnp.dot(vt_ref[hh, start + o], p[o * t:(o + 1) * t],
                               preferred_element_type=jnp.float32)
                total = part if total is None else total + part
            sinks = None
            if has_sink:
                sinks = [jnp.exp2(jnp.full((1, t), sink_ref[(h * heads + hh) * group + g] * LOG2E, jnp.float32))
                         for g in range(group)]
            _store_gated(total, g_refs[hh], o_ref, t, hh * group, sinks)
        return

    for hh in range(heads):
        _load_q(q_refs[hh], qt_ref.at[hh], t)
        acc_ref[hh, :HEAD_DIM] = jnp.zeros((HEAD_DIM, group * t), jnp.float32)
        if has_sink:
            for g in range(group):
                sink = sink_ref[(h * heads + hh) * group + g] * LOG2E
                m_ref[hh, :, g * t:(g + 1) * t] = jnp.full((1, t), sink, jnp.float32)
            acc_ref[hh, HEAD_DIM:] = jnp.ones((VT_ROWS - HEAD_DIM, group * t), jnp.float32)
        else:
            m_ref[hh] = jnp.full((1, group * t), M_INIT, jnp.float32)
            acc_ref[hh, HEAD_DIM:] = jnp.zeros((VT_ROWS - HEAD_DIM, group * t), jnp.float32)

    units = [(hh, o) for hh in range(heads) for o in range(n_off)]

    def scores(u):
        hh, o = units[u]
        kb, valid = block(o)
        _scores(k_ref, hh, kb * t, t, qt_ref.at[hh], s_ref, mx_ref, u % 2, bias_tile(hh, o, valid))

    scores(0)
    for u, (hh, o) in enumerate(units):
        if u + 1 < len(units):
            scores(u + 1)
        _accumulate(vt_ref[hh, block(o)[0]], s_ref, mx_ref, m_ref.at[hh], acc_ref.at[hh], u % 2)
        if o == n_off - 1:
            _store_gated(acc_ref.at[hh], g_refs[hh], o_ref, t, hh * group)


def _dense_kernel(*refs, group, tq, tk, nkv, bounded):
    it = iter(refs)
    q_refs = [next(it) for _ in range(group)]
    g_refs = [next(it) for _ in range(group)]
    k_ref, vt_ref, cos_ref, sin_ref, qg_ref, o_ref = (next(it) for _ in range(6))
    qt_ref, s_ref, mx_ref, m_ref, acc_ref = (next(it) for _ in range(5))
    qt_ref, m_ref, acc_ref = qt_ref.at[0], m_ref.at[0], acc_ref.at[0]

    _load_q(q_refs, qt_ref, tq, lambda xt: _norm_rope_t(xt, qg_ref[...], cos_ref[...], sin_ref[...]))
    acc_ref[...] = jnp.zeros(acc_ref.shape, jnp.float32)

    if bounded:
        def block(jblk):
            kt = k_ref[pl.ds(pl.multiple_of(jblk * tk, tk), tk), :]
            s = jnp.dot(kt, qt_ref[...], preferred_element_type=jnp.float32)
            return jnp.dot(vt_ref[0, jblk], jnp.exp2(s).astype(jnp.bfloat16),
                           preferred_element_type=jnp.float32)

        per_iter = math.gcd(nkv, BOUNDED_GROUP)

        def blocks(it, carry):
            part = block(it * per_iter)
            for u in range(1, per_iter):
                part = part + block(it * per_iter + u)
            acc_ref[...] += part
            return carry

        lax.fori_loop(0, nkv // per_iter, blocks, 0)
        _store_gated(acc_ref, g_refs, o_ref, tq)
        return

    m_ref[...] = jnp.full(m_ref.shape, M_INIT, jnp.float32)

    def scores(jblk, slot):
        _scores(k_ref, 0, jblk * tk, tk, qt_ref, s_ref, mx_ref, slot)

    def accumulate(jblk, slot):
        _accumulate(vt_ref[0, jblk], s_ref, mx_ref, m_ref, acc_ref, slot)

    scores(0, 0)

    def pair(j0):
        scores(j0 + 1, 1)
        accumulate(j0, 0)
        scores(j0 + 2, 0)
        accumulate(j0 + 1, 1)

    def pairs(it, carry):
        for u in range(DENSE_UNROLL):
            pair(2 * (it * DENSE_UNROLL + u))
        return carry

    n_pairs = nkv // 2 - 1
    lax.fori_loop(0, n_pairs // DENSE_UNROLL, pairs, 0)
    for jj in range(n_pairs - n_pairs % DENSE_UNROLL, n_pairs):
        pair(2 * jj)
    scores(nkv - 1, 1)
    accumulate(nkv - 2, 0)
    accumulate(nkv - 1, 1)
    _store_gated(acc_ref, g_refs, o_ref, tq)


def _attention(kern, proj, k_src, k_col, vt, q_col, g_col, n_kv_heads, group, tq, tk, extra_specs,
               extra_args, name, heads=1, bounded=False):
    s = proj.shape[0]
    parked = SUBLANES if bounded else tk
    assert s % tq == 0 and s % tk == 0 and n_kv_heads % heads == 0 and k_col % heads == 0
    in_specs, args = [], []
    for col in (q_col, g_col):
        for hh in range(heads):
            for g in range(group):
                in_specs.append(pl.BlockSpec(
                    (tq, HEAD_DIM), lambda h, qb, col=col, hh=hh, g=g: (qb, col + (h * heads + hh) * group + g)))
                args.append(proj)
    in_specs += [pl.BlockSpec((s, heads * HEAD_DIM), lambda h, qb: (0, k_col // heads + h)),
                 pl.BlockSpec((heads, s // tk, VT_ROWS, tk), lambda h, qb: (h, 0, 0, 0))] + extra_specs
    args += [k_src, vt] + extra_args
    n = group * tq
    return pl.pallas_call(
        kern,
        grid=(n_kv_heads // heads, s // tq),
        in_specs=in_specs,
        out_specs=pl.BlockSpec((tq, heads * group * HEAD_DIM), lambda h, qb: (qb, h)),
        out_shape=jax.ShapeDtypeStruct((s, n_kv_heads * group * HEAD_DIM), jnp.bfloat16),
        scratch_shapes=[pltpu.VMEM((heads, HEAD_DIM, n), jnp.bfloat16),
                        pltpu.VMEM((2, parked, n), jnp.float32),
                        pltpu.VMEM((2, 1, n), jnp.float32),
                        pltpu.VMEM((heads, 1, n), jnp.float32),
                        pltpu.VMEM((heads, VT_ROWS, n), jnp.float32)],
        compiler_params=_cparams(2),
        name=name,
    )(*args)


def _band_attention(proj, k_col, vt, q_col, g_col, n_kv_heads, group, t, reach, bias, sink, bounded,
                    heads=BAND_HEADS_PER_STEP):
    nkv = proj.shape[0] // t
    extra_specs = [pl.BlockSpec((heads * group,) + bias.shape[1:], lambda h, qb: (h, 0, 0, 0))]
    extra_args = [bias]
    if sink is not None:
        extra_specs.append(pl.BlockSpec(memory_space=pltpu.SMEM))
        extra_args.append(sink)
    kern = functools.partial(_band_kernel, heads=heads, group=group, t=t, nkv=nkv, reach=reach,
                             has_sink=sink is not None, bounded=bounded)
    name = "band_attention_g%d%s" % (group, "_bounded" if bounded else "")
    return _attention(kern, proj, proj, k_col, vt, q_col, g_col, n_kv_heads, group, t, t,
                      extra_specs, extra_args, name, heads, bounded)


def _dense_attention(proj, kc, vt, q_col, g_col, n_kv_heads, group, tk, cos_t, sin_t, q_gain_col, bounded):
    nkv = proj.shape[0] // tk
    tq = C_TQ[bounded]
    assert nkv % 2 == 0
    extra_specs = [pl.BlockSpec((HEAD_DIM, tq), lambda h, qb: (0, qb)),
                   pl.BlockSpec((HEAD_DIM, tq), lambda h, qb: (0, qb)),
                   pl.BlockSpec((HEAD_DIM, 1), lambda h, qb: (0, 0))]
    kern = functools.partial(_dense_kernel, group=group, tq=tq, tk=tk, nkv=nkv, bounded=bounded)
    name = "dense_attention_g%d%s" % (group, "_bounded" if bounded else "")
    return _attention(kern, proj, kc, 0, vt, q_col, g_col, n_kv_heads, group, tq, tk,
                      extra_specs, [cos_t, sin_t, q_gain_col], name, 1, bounded)


def _t5_bucket_np(rel):
    half_b = NUM_BUCKETS // 2
    max_exact = half_b // 2
    ret = np.where(rel > 0, half_b, 0)
    n = np.abs(rel)
    nf = np.maximum(n, 1).astype(np.float32)
    large = max_exact + (np.log(nf / np.float32(max_exact)) / np.float32(math.log(MAX_DISTANCE / max_exact))
                         * np.float32(half_b - max_exact)).astype(np.int32)
    large = np.minimum(large, half_b - 1)
    return (ret + np.where(n < max_exact, n, large)).astype(np.int32)


def _toeplitz_tiles(vec, tile, reach):
    heads = vec.shape[0]
    r = (vec.shape[1] - 1) // 2
    n_off = 2 * reach + 1
    span = 2 * tile - 1
    lo = -reach * tile - (tile - 1)
    hi = reach * tile + (tile - 1)
    padded = jnp.pad(vec, ((0, 0), (-lo - r, hi - r)), constant_values=MASKED)
    rows = [padded[:, o * tile:o * tile + span] for o in range(n_off)]
    u = jnp.stack(rows, axis=1)[:, :, ::-1]
    w = jnp.pad(u, ((0, 0), (0, 0), (0, 1)))
    flat = jnp.broadcast_to(w[:, :, None, :], (heads, n_off, tile, span + 1))
    flat = flat.reshape(heads, n_off, tile * (span + 1))[:, :, :tile * span]
    skew = flat.reshape(heads, n_off, tile, span)
    tiles = skew[:, :, :, tile - 1:]
    masked = jnp.full((tiles.shape[0], 1) + tiles.shape[2:], MASKED, tiles.dtype)
    return jnp.concatenate([tiles, masked], axis=1)


def _bias_tiles(rel_bias):
    hi = lax.Precision.HIGHEST
    d = np.arange(-A_REACH, A_REACH + 1)
    count = np.zeros(d.shape, np.float64)
    for w, dil in A_PATTERNS:
        count += (d % dil == 0) & (np.abs(d) <= w // 2)
    log_count = np.where(count > 0, np.log(np.maximum(count, 1.0)), 0.0).astype(np.float32)
    table = rel_bias.astype(jnp.float32)
    onehot_a = jnp.asarray(np.eye(NUM_BUCKETS, dtype=np.float32)[_t5_bucket_np(d)])
    vec_a = (jnp.dot(onehot_a, table[:, :A_HEADS], precision=hi).T + log_count[None]) * LOG2E
    vec_a = jnp.where(jnp.asarray(count > 0)[None], vec_a, MASKED)
    tiles_a = _toeplitz_tiles(vec_a, SUB, (-(-A_REACH // A_TILE) + 1) * (A_TILE // SUB) - 1)
    db = np.arange(-B_HALF_WINDOW, B_HALF_WINDOW + 1)
    onehot_b = jnp.asarray(np.eye(NUM_BUCKETS, dtype=np.float32)[_t5_bucket_np(db)])
    vec_b = jnp.dot(onehot_b, table[:, A_HEADS:], precision=hi).T * LOG2E
    tiles_b = _toeplitz_tiles(vec_b, SUB, (-(-B_HALF_WINDOW // B_TILE) + 1) * (B_TILE // SUB) - 1)
    top_a = jnp.max(jnp.abs(jnp.where(jnp.asarray(count > 0)[None], vec_a, 0.0)))
    return tiles_a, tiles_b, top_a, jnp.max(jnp.abs(vec_b))


def kernel(x, c, w_mod, b_mod, pre_norm, post_norm, w_in, w_out, q_norm, k_norm, sink, rel_bias):
    bsz, s, d = x.shape
    assert bsz == 1 and c.shape[0] == 1
    depth = w_in.shape[0]
    assert w_in.shape[2] == IN_WIDTH and w_out.shape[1] == A_W + B_W + C_W
    x2 = x[0]
    cos_t, sin_t = _rope_tables(s)
    tiles_a, tiles_b, top_bias_a, top_bias_b = _bias_tiles(rel_bias)
    mod = _modulation(c, w_mod, b_mod)[:, 0]
    shift, scale, gate = mod[:, :d], mod[:, d:2 * d], mod[:, 2 * d:]
    vec = lambda a, i: a[i][None, :]

    h = _prenorm(x2, vec(pre_norm, 0), vec(scale, 0), vec(shift, 0))
    for i in range(depth):
        proj = _in_projection(h, w_in, i)
        vta, vtb, vtc, kc, sq_norms = _prep(proj, cos_t, sin_t, k_norm[i][:, None])
        sq_norms = jnp.max(sq_norms, axis=(0, 2))
        qk_scale = NORM_SLACK * HEAD_DIM ** -0.5 * LOG2E
        bound_a = qk_scale * jnp.sqrt(sq_norms[0] * sq_norms[1]) + top_bias_a
        bound_b = jnp.maximum(qk_scale * jnp.sqrt(sq_norms[2] * sq_norms[3]) + top_bias_b,
                              jnp.max(jnp.abs(sink[i])) * LOG2E)
        band_a = functools.partial(_band_attention, proj, AK, vta, AQ, AG, A_HEADS, 1, A_TILE,
                                   -(-A_REACH // A_TILE), tiles_a, None)
        band_b = functools.partial(_band_attention, proj, BK, vtb, BQ, BG, B_KV_HEADS, B_HEADS // B_KV_HEADS,
                                   B_TILE, -(-B_HALF_WINDOW // B_TILE), tiles_b, sink[i])
        oa = lax.cond(bound_a <= LOGIT_BOUND, lambda: band_a(True), lambda: band_a(False))
        ob = lax.cond(bound_b <= LOGIT_BOUND, lambda: band_b(True), lambda: band_b(False))
        dense = functools.partial(_dense_attention, proj, kc, vtc, CQ, CG, C_KV_HEADS,
                                  C_HEADS // C_KV_HEADS, C_TK, cos_t, sin_t, q_norm[i][:, None])
        logit_bound = (HEAD_DIM ** 0.5 * LOG2E) * jnp.max(jnp.abs(q_norm[i])) * jnp.max(jnp.abs(k_norm[i]))
        oc = lax.cond(logit_bound <= LOGIT_BOUND, lambda: dense(True), lambda: dense(False))
        y = _out_projection(oa, ob, oc, w_out, i)
        nxt = None if i == depth - 1 else (vec(pre_norm, i + 1), vec(scale, i + 1), vec(shift, i + 1))
        x2, h = _post(x2, y, vec(post_norm, i), vec(gate, i), nxt)
    return x2[None]
```

```python
import functools
import math

import numpy as np
import jax
import jax.numpy as jnp
from jax import lax
from jax.experimental import pallas as pl
from jax.experimental.pallas import tpu as pltpu

HEAD_DIM = 128
GRID_W = 64
A_HEADS = 12
A_PATTERNS = ((128, 1), (512, 4), (2048, 16))
B_HEADS = 8
B_KV_HEADS = 2
B_HALF_WINDOW = 128
C_HEADS = 12
C_KV_HEADS = 4
ROPE_THETA = 10000.0
NUM_BUCKETS = 32
MAX_DISTANCE = 1024
EPS = 1e-6
LOG2E = math.log2(math.e)
MASKED = -1e30
M_INIT = -1e29

A_W = A_HEADS * HEAD_DIM
B_W = B_HEADS * HEAD_DIM
C_W = C_HEADS * HEAD_DIM
AQ, AK, AV, AG = 0, 12, 24, 36
BQ, BK, BV, BG = 48, 56, 58, 60
CQ, CK, CV, CG = 68, 80, 84, 88
IN_WIDTH = 100 * HEAD_DIM

A_REACH = max(w // 2 for w, _ in A_PATTERNS)
A_TILE = 512
B_TILE = 256
C_TQ = {True: 512, False: 256}
C_TK = 1024
BAND_HEADS_PER_STEP = 2
BOUNDED_GROUP = 4
NORM_SLACK = 1.02
LOGIT_BOUND = 64.0
DENSE_UNROLL = 3
VT_ROWS = HEAD_DIM + 16
V7X_VMEM_LIMIT = 56 * 1024 * 1024
SUBLANES = 8
SUB = 128


def _cparams(n_axes):
    return pltpu.CompilerParams(dimension_semantics=("arbitrary",) * n_axes,
                                vmem_limit_bytes=V7X_VMEM_LIMIT)


def _tile(n, pref):
    t = min(n, pref)
    while n % t:
        t //= 2
    return t


def _mod_kernel(c_ref, w_ref, b_ref, o_ref):
    c = c_ref[...]
    cs = jnp.broadcast_to(c * jax.nn.sigmoid(c), (SUBLANES, c.shape[1]))
    r = jnp.dot(cs, w_ref[0], preferred_element_type=jnp.float32)
    o_ref[0] = r[0:1] + b_ref[0]


def _modulation(c, w_mod, b_mod):
    depth, d, n = w_mod.shape
    tn = _tile(n, 1024)
    out = pl.pallas_call(
        _mod_kernel,
        grid=(depth, n // tn),
        in_specs=[pl.BlockSpec((1, d), lambda i, j: (0, 0)),
                  pl.BlockSpec((1, d, tn), lambda i, j: (i, 0, j)),
                  pl.BlockSpec((1, 1, tn), lambda i, j: (i, 0, j))],
        out_specs=pl.BlockSpec((1, 1, tn), lambda i, j: (i, 0, j)),
        out_shape=jax.ShapeDtypeStruct((depth, 1, n), jnp.float32),
        compiler_params=_cparams(2),
        name="modulation",
    )(c, w_mod, b_mod.reshape(depth, 1, n))
    return out


def _rms(x):
    return x * lax.rsqrt(jnp.mean(x * x, axis=-1, keepdims=True) + EPS)


def _prenorm_kernel(x_ref, g_ref, sc_ref, sh_ref, h_ref):
    y = _rms(x_ref[...]) * g_ref[...]
    h_ref[...] = (y * (1.0 + sc_ref[...]) + sh_ref[...]).astype(h_ref.dtype)


def _prenorm(x, g, scale, shift):
    s, d = x.shape
    tm = _tile(s, 512)
    vec = pl.BlockSpec((1, d), lambda i: (0, 0))
    return pl.pallas_call(
        _prenorm_kernel,
        grid=(s // tm,),
        in_specs=[pl.BlockSpec((tm, d), lambda i: (i, 0)), vec, vec, vec],
        out_specs=pl.BlockSpec((tm, d), lambda i: (i, 0)),
        out_shape=jax.ShapeDtypeStruct((s, d), jnp.bfloat16),
        compiler_params=_cparams(1),
        name="prenorm",
    )(x, g, scale, shift)


def _post_kernel(x_ref, y_ref, pg_ref, gate_ref, xo_ref):
    yn = _rms(y_ref[...].astype(jnp.float32)) * pg_ref[...]
    xo_ref[...] = x_ref[...] + gate_ref[...] * yn


def _postpre_kernel(x_ref, y_ref, pg_ref, gate_ref, g_ref, sc_ref, sh_ref, xo_ref, h_ref):
    yn = _rms(y_ref[...].astype(jnp.float32)) * pg_ref[...]
    xn = x_ref[...] + gate_ref[...] * yn
    xo_ref[...] = xn
    h = _rms(xn) * g_ref[...]
    h_ref[...] = (h * (1.0 + sc_ref[...]) + sh_ref[...]).astype(h_ref.dtype)


def _post(x, y, post_g, gate, nxt=None):
    s, d = x.shape
    tm = _tile(s, 256)
    row = pl.BlockSpec((tm, d), lambda i: (i, 0))
    vec = pl.BlockSpec((1, d), lambda i: (0, 0))
    if nxt is None:
        return pl.pallas_call(
            _post_kernel, grid=(s // tm,),
            in_specs=[row, row, vec, vec], out_specs=row,
            out_shape=jax.ShapeDtypeStruct((s, d), jnp.float32),
            compiler_params=_cparams(1), name="postnorm",
        )(x, y, post_g, gate), None
    g, scale, shift = nxt
    return pl.pallas_call(
        _postpre_kernel, grid=(s // tm,),
        in_specs=[row, row, vec, vec, vec, vec, vec], out_specs=[row, row],
        out_shape=[jax.ShapeDtypeStruct((s, d), jnp.float32),
                   jax.ShapeDtypeStruct((s, d), jnp.bfloat16)],
        compiler_params=_cparams(1), name="postnorm_prenorm",
    )(x, y, post_g, gate, g, scale, shift)


def _mm_kernel(a_ref, w_ref, o_ref, wb_ref):
    @pl.when(pl.program_id(1) == 0)
    def _():
        wb_ref[...] = w_ref[0].astype(wb_ref.dtype)

    o_ref[...] = jnp.dot(a_ref[...], wb_ref[...], preferred_element_type=jnp.float32).astype(o_ref.dtype)


def _in_projection(h, w, layer):
    m, k = h.shape
    n = w.shape[2]
    tm, tn = _tile(m, 1024), _tile(n, 512)
    return pl.pallas_call(
        _mm_kernel,
        grid=(n // tn, m // tm),
        in_specs=[pl.BlockSpec((tm, k), lambda j, i: (i, 0)),
                  pl.BlockSpec((1, k, tn), lambda j, i: (layer, 0, j))],
        out_specs=pl.BlockSpec((tm, tn), lambda j, i: (i, j)),
        out_shape=jax.ShapeDtypeStruct((m, n), jnp.bfloat16),
        scratch_shapes=[pltpu.VMEM((k, tn), h.dtype)],
        compiler_params=_cparams(2),
        name="in_projection",
    )(h, w)


def _mm3_kernel(a1, a2, a3, w_ref, o_ref, wb_ref):
    @pl.when(pl.program_id(1) == 0)
    def _():
        wb_ref[...] = w_ref[0].astype(wb_ref.dtype)

    acc, row = None, 0
    for a in (a1, a2, a3):
        part = jnp.dot(a[...], wb_ref[row:row + a.shape[1]], preferred_element_type=jnp.float32)
        acc = part if acc is None else acc + part
        row += a.shape[1]
    o_ref[...] = acc.astype(o_ref.dtype)


def _out_projection(oa, ob, oc, w, layer):
    m = oa.shape[0]
    k, n = w.shape[1:]
    tm, tn = _tile(m, 1024), _tile(n, 512)
    a_spec = lambda a: pl.BlockSpec((tm, a.shape[1]), lambda j, i: (i, 0))
    return pl.pallas_call(
        _mm3_kernel,
        grid=(n // tn, m // tm),
        in_specs=[a_spec(oa), a_spec(ob), a_spec(oc),
                  pl.BlockSpec((1, k, tn), lambda j, i: (layer, 0, j))],
        out_specs=pl.BlockSpec((tm, tn), lambda j, i: (i, j)),
        out_shape=jax.ShapeDtypeStruct((m, n), jnp.bfloat16),
        scratch_shapes=[pltpu.VMEM((k, tn), oa.dtype)],
        compiler_params=_cparams(2),
        name="out_projection",
    )(oa, ob, oc, w)


def _rope_tables(s):
    rows = s // GRID_W
    row = jnp.repeat(jnp.arange(rows), GRID_W)
    col = jnp.tile(jnp.arange(GRID_W), rows)
    quarter = HEAD_DIM // 4
    freqs = ROPE_THETA ** (-jnp.arange(0, 2 * quarter, 2, dtype=jnp.float32) / (2 * quarter))
    ar = row.astype(jnp.float32)[:, None] * freqs[None, :]
    ac = col.astype(jnp.float32)[:, None] * freqs[None, :]
    cr, sr, cc, sc = jnp.cos(ar), jnp.sin(ar), jnp.cos(ac), jnp.sin(ac)
    cos = jnp.concatenate([cr, cr, cc, cc], axis=-1)
    sin = jnp.concatenate([-sr, sr, -sc, sc], axis=-1)
    return cos.T, sin.T


def _norm_rope_t(xt, gain_col, cos_t, sin_t):
    quarter = HEAD_DIM // 4
    y = xt * lax.rsqrt(jnp.mean(xt * xt, axis=0, keepdims=True) + EPS) * gain_col
    partner = jnp.concatenate([y[quarter:2 * quarter], y[:quarter], y[3 * quarter:], y[2 * quarter:3 * quarter]],
                              axis=0)
    return y * cos_t + partner * sin_t


def _prep_kernel(av_ref, bv_ref, cv_ref, ck_ref, aq_ref, ak_ref, bq_ref, bk_ref, cos_ref, sin_ref, kg_ref,
                 vta_ref, vtb_ref, vtc_ref, kc_ref, nrm_ref, *, ts):
    def put_t(src_ref, n_heads, dst_ref, chunk):
        for hh in range(n_heads):
            vt = src_ref[:, hh * HEAD_DIM:(hh + 1) * HEAD_DIM].T
            for cidx in range(ts // chunk):
                dst_ref[hh, cidx, :HEAD_DIM] = vt[:, cidx * chunk:(cidx + 1) * chunk]
                dst_ref[hh, cidx, HEAD_DIM:] = jnp.ones((VT_ROWS - HEAD_DIM, chunk), dst_ref.dtype)

    put_t(av_ref, A_HEADS, vta_ref, A_TILE)
    put_t(bv_ref, B_KV_HEADS, vtb_ref, B_TILE)
    put_t(cv_ref, C_KV_HEADS, vtc_ref, C_TK)
    cos_t, sin_t, kg = cos_ref[...], sin_ref[...], kg_ref[...]
    for hh in range(C_KV_HEADS):
        kt = ck_ref[:, hh * HEAD_DIM:(hh + 1) * HEAD_DIM].astype(jnp.float32).T
        k = _norm_rope_t(kt, kg, cos_t, sin_t).T
        kc_ref[:, hh * HEAD_DIM:(hh + 1) * HEAD_DIM] = k.astype(kc_ref.dtype)

    def max_sq_norm(src_ref, n_heads):
        ones = jnp.ones((HEAD_DIM, HEAD_DIM), src_ref.dtype)
        best = None
        for hh in range(n_heads):
            x = src_ref[:, hh * HEAD_DIM:(hh + 1) * HEAD_DIM]
            row_sums = jnp.dot(x * x, ones, preferred_element_type=jnp.float32)
            top = jnp.max(row_sums, axis=0, keepdims=True)
            best = top if best is None else jnp.maximum(best, top)
        return best

    nrm_ref[0] = jnp.concatenate(
        [max_sq_norm(aq_ref, A_HEADS), max_sq_norm(ak_ref, A_HEADS), max_sq_norm(bq_ref, B_HEADS),
         max_sq_norm(bk_ref, B_KV_HEADS), jnp.zeros((SUBLANES - 4, HEAD_DIM), jnp.float32)], axis=0)


def _prep(proj, cos_t, sin_t, k_gain_col):
    s = proj.shape[0]
    ts = max(A_TILE, B_TILE, C_TK)
    assert s % ts == 0
    bf = jnp.bfloat16
    n = s // ts
    vt_spec = lambda heads, t: pl.BlockSpec((heads, ts // t, VT_ROWS, t), lambda i: (0, i, 0, 0))
    vt_shape = lambda heads, t: jax.ShapeDtypeStruct((heads, s // t, VT_ROWS, t), bf)
    return pl.pallas_call(
        functools.partial(_prep_kernel, ts=ts),
        grid=(n,),
        in_specs=[pl.BlockSpec((ts, A_W), lambda i: (i, AV * HEAD_DIM // A_W)),
                  pl.BlockSpec((ts, B_KV_HEADS * HEAD_DIM), lambda i: (i, BV // B_KV_HEADS)),
                  pl.BlockSpec((ts, C_KV_HEADS * HEAD_DIM), lambda i: (i, CV // C_KV_HEADS)),
                  pl.BlockSpec((ts, C_KV_HEADS * HEAD_DIM), lambda i: (i, CK // C_KV_HEADS)),
                  pl.BlockSpec((ts, A_W), lambda i: (i, AQ * HEAD_DIM // A_W)),
                  pl.BlockSpec((ts, A_W), lambda i: (i, AK * HEAD_DIM // A_W)),
                  pl.BlockSpec((ts, B_W), lambda i: (i, BQ * HEAD_DIM // B_W)),
                  pl.BlockSpec((ts, B_KV_HEADS * HEAD_DIM), lambda i: (i, BK // B_KV_HEADS)),
                  pl.BlockSpec((HEAD_DIM, ts), lambda i: (0, i)),
                  pl.BlockSpec((HEAD_DIM, ts), lambda i: (0, i)),
                  pl.BlockSpec((HEAD_DIM, 1), lambda i: (0, 0))],
        out_specs=[vt_spec(A_HEADS, A_TILE), vt_spec(B_KV_HEADS, B_TILE), vt_spec(C_KV_HEADS, C_TK),
                   pl.BlockSpec((ts, C_KV_HEADS * HEAD_DIM), lambda i: (i, 0)),
                   pl.BlockSpec((1, SUBLANES, HEAD_DIM), lambda i: (i, 0, 0))],
        out_shape=[vt_shape(A_HEADS, A_TILE), vt_shape(B_KV_HEADS, B_TILE), vt_shape(C_KV_HEADS, C_TK),
                   jax.ShapeDtypeStruct((s, C_KV_HEADS * HEAD_DIM), bf),
                   jax.ShapeDtypeStruct((n, SUBLANES, HEAD_DIM), jnp.float32)],
        compiler_params=_cparams(1),
        name="vt_krope_prep",
    )(proj, proj, proj, proj, proj, proj, proj, proj, cos_t, sin_t, k_gain_col)


def _scores(k_ref, head, row0, tk, qt_ref, s_ref, mx_ref, slot, bias=None):
    kt = k_ref[pl.ds(pl.multiple_of(row0, tk), tk), head * HEAD_DIM:(head + 1) * HEAD_DIM]
    s = jnp.dot(kt, qt_ref[...], preferred_element_type=jnp.float32)
    if bias is not None:
        s = s + bias
    s_ref[slot] = s
    mx_ref[slot] = jnp.max(s, axis=0, keepdims=True)


def _accumulate(vt, s_ref, mx_ref, m_ref, acc_ref, slot):
    m_old = m_ref[...]
    m_new = jnp.maximum(m_old, mx_ref[slot])
    alpha = jnp.exp2(m_old - m_new)
    p = jnp.exp2(s_ref[slot] - m_new).astype(jnp.bfloat16)
    acc_ref[...] = alpha * acc_ref[...] + jnp.dot(vt, p, preferred_element_type=jnp.float32)
    m_ref[...] = m_new


def _load_q(q_refs, qt_ref, tq, prep_t=lambda qt: qt):
    scale2 = HEAD_DIM ** -0.5 * LOG2E
    for g, q_ref in enumerate(q_refs):
        qt = prep_t(q_ref[...].astype(jnp.float32).T)
        qt_ref[:, g * tq:(g + 1) * tq] = (qt * scale2).astype(qt_ref.dtype)


def _store_gated(acc, g_refs, o_ref, tq, first=0, extra_denominator=None):
    for g, g_ref in enumerate(g_refs):
        sl = slice(g * tq, (g + 1) * tq)
        den = acc[HEAD_DIM:HEAD_DIM + 1, sl]
        if extra_denominator is not None:
            den = den + extra_denominator[g]
        o = (acc[:HEAD_DIM, sl] / den).T
        gate = g_ref[...].astype(jnp.float32)
        cols = slice((first + g) * HEAD_DIM, (first + g + 1) * HEAD_DIM)
        o_ref[:, cols] = (o * (gate * jax.nn.sigmoid(gate))).astype(o_ref.dtype)


def _band_kernel(*refs, heads, group, t, nkv, reach, has_sink, bounded):
    it = iter(refs)
    q_refs = [[next(it) for _ in range(group)] for _ in range(heads)]
    g_refs = [[next(it) for _ in range(group)] for _ in range(heads)]
    k_ref, vt_ref, bias_ref = next(it), next(it), next(it)
    sink_ref = next(it) if has_sink else None
    o_ref = next(it)
    qt_ref, s_ref, mx_ref, m_ref, acc_ref = (next(it) for _ in range(5))
    h = pl.program_id(0)
    qb = pl.program_id(1)
    n_off = 2 * reach + 1

    nsb = t // SUB
    n_sub = bias_ref.shape[1] - 1

    def block(o):
        kb = qb + (o - reach)
        return jnp.clip(kb, 0, nkv - 1), jnp.logical_and(kb >= 0, kb < nkv)

    def bias_tile(hh, o, valid):
        rows = []
        for a in range(nsb):
            cols = []
            for g in range(group):
                for b in range(nsb):
                    sub = (o - reach) * nsb + a - b + (n_sub - 1) // 2
                    cols.append(bias_ref[hh * group + g, jnp.where(valid, sub, n_sub)])
            rows.append(jnp.concatenate(cols, axis=1))
        return jnp.concatenate(rows, axis=0)

    if bounded:
        for hh in range(heads):
            _load_q(q_refs[hh], qt_ref.at[hh], t)
        width = min(n_off, nkv)
        start = jnp.clip(qb - reach, 0, nkv - width)

        def bias_window(hh):
            rows = []
            for a in range(width * nsb):
                cols = []
                for g in range(group):
                    for b in range(nsb):
                        sub = (start - qb) * nsb + a - b + (n_sub - 1) // 2
                        inside = jnp.logical_and(sub >= 0, sub < n_sub)
                        cols.append(bias_ref[hh * group + g, jnp.where(inside, sub, n_sub)])
                rows.append(jnp.concatenate(cols, axis=1))
            return jnp.concatenate(rows, axis=0)

        for hh in range(heads):
            keys = k_ref[pl.ds(pl.multiple_of(start * t, t), width * t), hh * HEAD_DIM:(hh + 1) * HEAD_DIM]
            s = jnp.dot(keys, qt_ref[hh], preferred_element_type=jnp.float32) + bias_window(hh)
            p = jnp.exp2(s).astype(jnp.bfloat16)
            total = None
            for o in range(width):
                part = jnp.dot(vt_ref[hh, start + o], p[o * t:(o + 1) * t],
                               preferred_element_type=jnp.float32)
                total = part if total is None else total + part
            sinks = None
            if has_sink:
                sinks = [jnp.exp2(jnp.full((1, t), sink_ref[(h * heads + hh) * group + g] * LOG2E, jnp.float32))
                         for g in range(group)]
            _store_gated(total, g_refs[hh], o_ref, t, hh * group, sinks)
        return

    for hh in range(heads):
        _load_q(q_refs[hh], qt_ref.at[hh], t)
        acc_ref[hh, :HEAD_DIM] = jnp.zeros((HEAD_DIM, group * t), jnp.float32)
        if has_sink:
            for g in range(group):
                sink = sink_ref[(h * heads + hh) * group + g] * LOG2E
                m_ref[hh, :, g * t:(g + 1) * t] = jnp.full((1, t), sink, jnp.float32)
            acc_ref[hh, HEAD_DIM:] = jnp.ones((VT_ROWS - HEAD_DIM, group * t), jnp.float32)
        else:
            m_ref[hh] = jnp.full((1, group * t), M_INIT, jnp.float32)
            acc_ref[hh, HEAD_DIM:] = jnp.zeros((VT_ROWS - HEAD_DIM, group * t), jnp.float32)

    units = [(hh, o) for hh in range(heads) for o in range(n_off)]

    def scores(u):
        hh, o = units[u]
        kb, valid = block(o)
        _scores(k_ref, hh, kb * t, t, qt_ref.at[hh], s_ref, mx_ref, u % 2, bias_tile(hh, o, valid))

    scores(0)
    for u, (hh, o) in enumerate(units):
        if u + 1 < len(units):
            scores(u + 1)
        _accumulate(vt_ref[hh, block(o)[0]], s_ref, mx_ref, m_ref.at[hh], acc_ref.at[hh], u % 2)
        if o == n_off - 1:
            _store_gated(acc_ref.at[hh], g_refs[hh], o_ref, t, hh * group)


def _dense_kernel(*refs, group, tq, tk, nkv, bounded):
    it = iter(refs)
    q_refs = [next(it) for _ in range(group)]
    g_refs = [next(it) for _ in range(group)]
    k_ref, vt_ref, cos_ref, sin_ref, qg_ref, o_ref = (next(it) for _ in range(6))
    qt_ref, s_ref, mx_ref, m_ref, acc_ref = (next(it) for _ in range(5))
    qt_ref, m_ref, acc_ref = qt_ref.at[0], m_ref.at[0], acc_ref.at[0]

    _load_q(q_refs, qt_ref, tq, lambda xt: _norm_rope_t(xt, qg_ref[...], cos_ref[...], sin_ref[...]))
    acc_ref[...] = jnp.zeros(acc_ref.shape, jnp.float32)

    if bounded:
        def block(jblk):
            kt = k_ref[pl.ds(pl.multiple_of(jblk * tk, tk), tk), :]
            s = jnp.dot(kt, qt_ref[...], preferred_element_type=jnp.float32)
            return jnp.dot(vt_ref[0, jblk], jnp.exp2(s).astype(jnp.bfloat16),
                           preferred_element_type=jnp.float32)

        per_iter = math.gcd(nkv, BOUNDED_GROUP)

        def blocks(it, carry):
            part = block(it * per_iter)
            for u in range(1, per_iter):
                part = part + block(it * per_iter + u)
            acc_ref[...] += part
            return carry

        lax.fori_loop(0, nkv // per_iter, blocks, 0)
        _store_gated(acc_ref, g_refs, o_ref, tq)
        return

    m_ref[...] = jnp.full(m_ref.shape, M_INIT, jnp.float32)

    def scores(jblk, slot):
        _scores(k_ref, 0, jblk * tk, tk, qt_ref, s_ref, mx_ref, slot)

    def accumulate(jblk, slot):
        _accumulate(vt_ref[0, jblk], s_ref, mx_ref, m_ref, acc_ref, slot)

    scores(0, 0)

    def pair(j0):
        scores(j0 + 1, 1)
        accumulate(j0, 0)
        scores(j0 + 2, 0)
        accumulate(j0 + 1, 1)

    def pairs(it, carry):
        for u in range(DENSE_UNROLL):
            pair(2 * (it * DENSE_UNROLL + u))
        return carry

    n_pairs = nkv // 2 - 1
    lax.fori_loop(0, n_pairs // DENSE_UNROLL, pairs, 0)
    for jj in range(n_pairs - n_pairs % DENSE_UNROLL, n_pairs):
        pair(2 * jj)
    scores(nkv - 1, 1)
    accumulate(nkv - 2, 0)
    accumulate(nkv - 1, 1)
    _store_gated(acc_ref, g_refs, o_ref, tq)


def _attention(kern, proj, k_src, k_col, vt, q_col, g_col, n_kv_heads, group, tq, tk, extra_specs,
               extra_args, name, heads=1, bounded=False):
    s = proj.shape[0]
    parked = SUBLANES if bounded else tk
    assert s % tq == 0 and s % tk == 0 and n_kv_heads % heads == 0 and k_col % heads == 0
    in_specs, args = [], []
    for col in (q_col, g_col):
        for hh in range(heads):
            for g in range(group):
                in_specs.append(pl.BlockSpec(
                    (tq, HEAD_DIM), lambda h, qb, col=col, hh=hh, g=g: (qb, col + (h * heads + hh) * group + g)))
                args.append(proj)
    in_specs += [pl.BlockSpec((s, heads * HEAD_DIM), lambda h, qb: (0, k_col // heads + h)),
                 pl.BlockSpec((heads, s // tk, VT_ROWS, tk), lambda h, qb: (h, 0, 0, 0))] + extra_specs
    args += [k_src, vt] + extra_args
    n = group * tq
    return pl.pallas_call(
        kern,
        grid=(n_kv_heads // heads, s // tq),
        in_specs=in_specs,
        out_specs=pl.BlockSpec((tq, heads * group * HEAD_DIM), lambda h, qb: (qb, h)),
        out_shape=jax.ShapeDtypeStruct((s, n_kv_heads * group * HEAD_DIM), jnp.bfloat16),
        scratch_shapes=[pltpu.VMEM((heads, HEAD_DIM, n), jnp.bfloat16),
                        pltpu.VMEM((2, parked, n), jnp.float32),
                        pltpu.VMEM((2, 1, n), jnp.float32),
                        pltpu.VMEM((heads, 1, n), jnp.float32),
                        pltpu.VMEM((heads, VT_ROWS, n), jnp.float32)],
        compiler_params=_cparams(2),
        name=name,
    )(*args)


def _band_attention(proj, k_col, vt, q_col, g_col, n_kv_heads, group, t, reach, bias, sink, bounded,
                    heads=BAND_HEADS_PER_STEP):
    nkv = proj.shape[0] // t
    extra_specs = [pl.BlockSpec((heads * group,) + bias.shape[1:], lambda h, qb: (h, 0, 0, 0))]
    extra_args = [bias]
    if sink is not None:
        extra_specs.append(pl.BlockSpec(memory_space=pltpu.SMEM))
        extra_args.append(sink)
    kern = functools.partial(_band_kernel, heads=heads, group=group, t=t, nkv=nkv, reach=reach,
                             has_sink=sink is not None, bounded=bounded)
    name = "band_attention_g%d%s" % (group, "_bounded" if bounded else "")
    return _attention(kern, proj, proj, k_col, vt, q_col, g_col, n_kv_heads, group, t, t,
                      extra_specs, extra_args, name, heads, bounded)


def _dense_attention(proj, kc, vt, q_col, g_col, n_kv_heads, group, tk, cos_t, sin_t, q_gain_col, bounded):
    nkv = proj.shape[0] // tk
    tq = C_TQ[bounded]
    assert nkv % 2 == 0
    extra_specs = [pl.BlockSpec((HEAD_DIM, tq), lambda h, qb: (0, qb)),
                   pl.BlockSpec((HEAD_DIM, tq), lambda h, qb: (0, qb)),
                   pl.BlockSpec((HEAD_DIM, 1), lambda h, qb: (0, 0))]
    kern = functools.partial(_dense_kernel, group=group, tq=tq, tk=tk, nkv=nkv, bounded=bounded)
    name = "dense_attention_g%d%s" % (group, "_bounded" if bounded else "")
    return _attention(kern, proj, kc, 0, vt, q_col, g_col, n_kv_heads, group, tq, tk,
                      extra_specs, [cos_t, sin_t, q_gain_col], name, 1, bounded)


def _t5_bucket_np(rel):
    half_b = NUM_BUCKETS // 2
    max_exact = half_b // 2
    ret = np.where(rel > 0, half_b, 0)
    n = np.abs(rel)
    nf = np.maximum(n, 1).astype(np.float32)
    large = max_exact + (np.log(nf / np.float32(max_exact)) / np.float32(math.log(MAX_DISTANCE / max_exact))
                         * np.float32(half_b - max_exact)).astype(np.int32)
    large = np.minimum(large, half_b - 1)
    return (ret + np.where(n < max_exact, n, large)).astype(np.int32)


def _toeplitz_tiles(vec, tile, reach):
    heads = vec.shape[0]
    r = (vec.shape[1] - 1) // 2
    n_off = 2 * reach + 1
    span = 2 * tile - 1
    lo = -reach * tile - (tile - 1)
    hi = reach * tile + (tile - 1)
    padded = jnp.pad(vec, ((0, 0), (-lo - r, hi - r)), constant_values=MASKED)
    rows = [padded[:, o * tile:o * tile + span] for o in range(n_off)]
    u = jnp.stack(rows, axis=1)[:, :, ::-1]
    w = jnp.pad(u, ((0, 0), (0, 0), (0, 1)))
    flat = jnp.broadcast_to(w[:, :, None, :], (heads, n_off, tile, span + 1))
    flat = flat.reshape(heads, n_off, tile * (span + 1))[:, :, :tile * span]
    skew = flat.reshape(heads, n_off, tile, span)
    tiles = skew[:, :, :, tile - 1:]
    masked = jnp.full((tiles.shape[0], 1) + tiles.shape[2:], MASKED, tiles.dtype)
    return jnp.concatenate([tiles, masked], axis=1)


def _bias_tiles(rel_bias):
    hi = lax.Precision.HIGHEST
    d = np.arange(-A_REACH, A_REACH + 1)
    count = np.zeros(d.shape, np.float64)
    for w, dil in A_PATTERNS:
        count += (d % dil == 0) & (np.abs(d) <= w // 2)
    log_count = np.where(count > 0, np.log(np.maximum(count, 1.0)), 0.0).astype(np.float32)
    table = rel_bias.astype(jnp.float32)
    onehot_a = jnp.asarray(np.eye(NUM_BUCKETS, dtype=np.float32)[_t5_bucket_np(d)])
    vec_a = (jnp.dot(onehot_a, table[:, :A_HEADS], precision=hi).T + log_count[None]) * LOG2E
    vec_a = jnp.where(jnp.asarray(count > 0)[None], vec_a, MASKED)
    tiles_a = _toeplitz_tiles(vec_a, SUB, (-(-A_REACH // A_TILE) + 1) * (A_TILE // SUB) - 1)
    db = np.arange(-B_HALF_WINDOW, B_HALF_WINDOW + 1)
    onehot_b = jnp.asarray(np.eye(NUM_BUCKETS, dtype=np.float32)[_t5_bucket_np(db)])
    vec_b = jnp.dot(onehot_b, table[:, A_HEADS:], precision=hi).T * LOG2E
    tiles_b = _toeplitz_tiles(vec_b, SUB, (-(-B_HALF_WINDOW // B_TILE) + 1) * (B_TILE // SUB) - 1)
    top_a = jnp.max(jnp.abs(jnp.where(jnp.asarray(count > 0)[None], vec_a, 0.0)))
    return tiles_a, tiles_b, top_a, jnp.max(jnp.abs(vec_b))


def kernel(x, c, w_mod, b_mod, pre_norm, post_norm, w_in, w_out, q_norm, k_norm, sink, rel_bias):
    bsz, s, d = x.shape
    assert bsz == 1 and c.shape[0] == 1
    depth = w_in.shape[0]
    assert w_in.shape[2] == IN_WIDTH and w_out.shape[1] == A_W + B_W + C_W
    x2 = x[0]
    cos_t, sin_t = _rope_tables(s)
    tiles_a, tiles_b, top_bias_a, top_bias_b = _bias_tiles(rel_bias)
    mod = _modulation(c, w_mod, b_mod)[:, 0]
    shift, scale, gate = mod[:, :d], mod[:, d:2 * d], mod[:, 2 * d:]
    vec = lambda a, i: a[i][None, :]

    h = _prenorm(x2, vec(pre_norm, 0), vec(scale, 0), vec(shift, 0))
    for i in range(depth):
        proj = _in_projection(h, w_in, i)
        vta, vtb, vtc, kc, sq_norms = _prep(proj, cos_t, sin_t, k_norm[i][:, None])
        sq_norms = jnp.max(sq_norms, axis=(0, 2))
        qk_scale = NORM_SLACK * HEAD_DIM ** -0.5 * LOG2E
        bound_a = qk_scale * jnp.sqrt(sq_norms[0] * sq_norms[1]) + top_bias_a
        bound_b = jnp.maximum(qk_scale * jnp.sqrt(sq_norms[2] * sq_norms[3]) + top_bias_b,
                              jnp.max(jnp.abs(sink[i])) * LOG2E)
        band_a = functools.partial(_band_attention, proj, AK, vta, AQ, AG, A_HEADS, 1, A_TILE,
                                   -(-A_REACH // A_TILE), tiles_a, None)
        band_b = functools.partial(_band_attention, proj, BK, vtb, BQ, BG, B_KV_HEADS, B_HEADS // B_KV_HEADS,
                                   B_TILE, -(-B_HALF_WINDOW // B_TILE), tiles_b, sink[i])
        oa = lax.cond(bound_a <= LOGIT_BOUND, lambda: band_a(True), lambda: band_a(False))
        ob = lax.cond(bound_b <= LOGIT_BOUND, lambda: band_b(True), lambda: band_b(False))
        dense = functools.partial(_dense_attention, proj, kc, vtc, CQ, CG, C_KV_HEADS,
                                  C_HEADS // C_KV_HEADS, C_TK, cos_t, sin_t, q_norm[i][:, None])
        logit_bound = (HEAD_DIM ** 0.5 * LOG2E) * jnp.max(jnp.abs(q_norm[i])) * jnp.max(jnp.abs(k_norm[i]))
        oc = lax.cond(logit_bound <= LOGIT_BOUND, lambda: dense(True), lambda: dense(False))
        y = _out_projection(oa, ob, oc, w_out, i)
        nxt = None if i == depth - 1 else (vec(pre_norm, i + 1), vec(scale, i + 1), vec(shift, i + 1))
        x2, h = _post(x2, y, vec(post_norm, i), vec(gate, i), nxt)
    return x2[None]
```

```python
import functools
import math

import numpy as np
import jax
import jax.numpy as jnp
from jax import lax
from jax.experimental import pallas as pl
from jax.experimental.pallas import tpu as pltpu

HEAD_DIM = 128
GRID_W = 64
A_HEADS = 12
A_PATTERNS = ((128, 1), (512, 4), (2048, 16))
B_HEADS = 8
B_KV_HEADS = 2
B_HALF_WINDOW = 128
C_HEADS = 12
C_KV_HEADS = 4
ROPE_THETA = 10000.0
NUM_BUCKETS = 32
MAX_DISTANCE = 1024
EPS = 1e-6
LOG2E = math.log2(math.e)
MASKED = -1e30
M_INIT = -1e29

A_W = A_HEADS * HEAD_DIM
B_W = B_HEADS * HEAD_DIM
C_W = C_HEADS * HEAD_DIM
AQ, AK, AV, AG = 0, 12, 24, 36
BQ, BK, BV, BG = 48, 56, 58, 60
CQ, CK, CV, CG = 68, 80, 84, 88
IN_WIDTH = 100 * HEAD_DIM

A_REACH = max(w // 2 for w, _ in A_PATTERNS)
A_TILE = 512
B_TILE = 256
C_TQ = {True: 1024, False: 256}
C_TK = 1024
BAND_HEADS_PER_STEP = 2
BOUNDED_GROUP = 4
NORM_SLACK = 1.02
LOGIT_BOUND = 64.0
DENSE_UNROLL = 3
VT_ROWS = HEAD_DIM + 16
V7X_VMEM_LIMIT = 56 * 1024 * 1024
SUBLANES = 8
SUB = 128


def _cparams(n_axes):
    return pltpu.CompilerParams(dimension_semantics=("arbitrary",) * n_axes,
                                vmem_limit_bytes=V7X_VMEM_LIMIT)


def _tile(n, pref):
    t = min(n, pref)
    while n % t:
        t //= 2
    return t


def _mod_kernel(c_ref, w_ref, b_ref, o_ref):
    c = c_ref[...]
    cs = jnp.broadcast_to(c * jax.nn.sigmoid(c), (SUBLANES, c.shape[1]))
    r = jnp.dot(cs, w_ref[0], preferred_element_type=jnp.float32)
    o_ref[0] = r[0:1] + b_ref[0]


def _modulation(c, w_mod, b_mod):
    depth, d, n = w_mod.shape
    tn = _tile(n, 1024)
    out = pl.pallas_call(
        _mod_kernel,
        grid=(depth, n // tn),
        in_specs=[pl.BlockSpec((1, d), lambda i, j: (0, 0)),
                  pl.BlockSpec((1, d, tn), lambda i, j: (i, 0, j)),
                  pl.BlockSpec((1, 1, tn), lambda i, j: (i, 0, j))],
        out_specs=pl.BlockSpec((1, 1, tn), lambda i, j: (i, 0, j)),
        out_shape=jax.ShapeDtypeStruct((depth, 1, n), jnp.float32),
        compiler_params=_cparams(2),
        name="modulation",
    )(c, w_mod, b_mod.reshape(depth, 1, n))
    return out


def _rms(x):
    return x * lax.rsqrt(jnp.mean(x * x, axis=-1, keepdims=True) + EPS)


def _prenorm_kernel(x_ref, g_ref, sc_ref, sh_ref, h_ref):
    y = _rms(x_ref[...]) * g_ref[...]
    h_ref[...] = (y * (1.0 + sc_ref[...]) + sh_ref[...]).astype(h_ref.dtype)


def _prenorm(x, g, scale, shift):
    s, d = x.shape
    tm = _tile(s, 512)
    vec = pl.BlockSpec((1, d), lambda i: (0, 0))
    return pl.pallas_call(
        _prenorm_kernel,
        grid=(s // tm,),
        in_specs=[pl.BlockSpec((tm, d), lambda i: (i, 0)), vec, vec, vec],
        out_specs=pl.BlockSpec((tm, d), lambda i: (i, 0)),
        out_shape=jax.ShapeDtypeStruct((s, d), jnp.bfloat16),
        compiler_params=_cparams(1),
        name="prenorm",
    )(x, g, scale, shift)


def _post_kernel(x_ref, y_ref, pg_ref, gate_ref, xo_ref):
    yn = _rms(y_ref[...].astype(jnp.float32)) * pg_ref[...]
    xo_ref[...] = x_ref[...] + gate_ref[...] * yn


def _postpre_kernel(x_ref, y_ref, pg_ref, gate_ref, g_ref, sc_ref, sh_ref, xo_ref, h_ref):
    yn = _rms(y_ref[...].astype(jnp.float32)) * pg_ref[...]
    xn = x_ref[...] + gate_ref[...] * yn
    xo_ref[...] = xn
    h = _rms(xn) * g_ref[...]
    h_ref[...] = (h * (1.0 + sc_ref[...]) + sh_ref[...]).astype(h_ref.dtype)


def _post(x, y, post_g, gate, nxt=None):
    s, d = x.shape
    tm = _tile(s, 256)
    row = pl.BlockSpec((tm, d), lambda i: (i, 0))
    vec = pl.BlockSpec((1, d), lambda i: (0, 0))
    if nxt is None:
        return pl.pallas_call(
            _post_kernel, grid=(s // tm,),
            in_specs=[row, row, vec, vec], out_specs=row,
            out_shape=jax.ShapeDtypeStruct((s, d), jnp.float32),
            compiler_params=_cparams(1), name="postnorm",
        )(x, y, post_g, gate), None
    g, scale, shift = nxt
    return pl.pallas_call(
        _postpre_kernel, grid=(s // tm,),
        in_specs=[row, row, vec, vec, vec, vec, vec], out_specs=[row, row],
        out_shape=[jax.ShapeDtypeStruct((s, d), jnp.float32),
                   jax.ShapeDtypeStruct((s, d), jnp.bfloat16)],
        compiler_params=_cparams(1), name="postnorm_prenorm",
    )(x, y, post_g, gate, g, scale, shift)


def _mm_kernel(a_ref, b_ref, o_ref):
    o_ref[...] = jnp.dot(a_ref[...], b_ref[0].astype(a_ref.dtype),
                         preferred_element_type=jnp.float32).astype(o_ref.dtype)


def _in_projection(h, w, layer):
    m, k = h.shape
    n = w.shape[2]
    tm, tn = _tile(m, 1024), _tile(n, 512)
    return pl.pallas_call(
        _mm_kernel,
        grid=(m // tm, n // tn),
        in_specs=[pl.BlockSpec((tm, k), lambda i, j: (i, 0)),
                  pl.BlockSpec((1, k, tn), lambda i, j: (layer, 0, j))],
        out_specs=pl.BlockSpec((tm, tn), lambda i, j: (i, j)),
        out_shape=jax.ShapeDtypeStruct((m, n), jnp.bfloat16),
        compiler_params=_cparams(2),
        name="in_projection",
    )(h, w)


def _mm3_kernel(a1, a2, a3, w_ref, o_ref, wb_ref):
    @pl.when(pl.program_id(1) == 0)
    def _():
        wb_ref[...] = w_ref[0].astype(wb_ref.dtype)

    acc, row = None, 0
    for a in (a1, a2, a3):
        part = jnp.dot(a[...], wb_ref[row:row + a.shape[1]], preferred_element_type=jnp.float32)
        acc = part if acc is None else acc + part
        row += a.shape[1]
    o_ref[...] = acc.astype(o_ref.dtype)


def _out_projection(oa, ob, oc, w, layer):
    m = oa.shape[0]
    k, n = w.shape[1:]
    tm, tn = _tile(m, 1024), _tile(n, 512)
    a_spec = lambda a: pl.BlockSpec((tm, a.shape[1]), lambda j, i: (i, 0))
    return pl.pallas_call(
        _mm3_kernel,
        grid=(n // tn, m // tm),
        in_specs=[a_spec(oa), a_spec(ob), a_spec(oc),
                  pl.BlockSpec((1, k, tn), lambda j, i: (layer, 0, j))],
        out_specs=pl.BlockSpec((tm, tn), lambda j, i: (i, j)),
        out_shape=jax.ShapeDtypeStruct((m, n), jnp.bfloat16),
        scratch_shapes=[pltpu.VMEM((k, tn), oa.dtype)],
        compiler_params=_cparams(2),
        name="out_projection",
    )(oa, ob, oc, w)


def _rope_tables(s):
    rows = s // GRID_W
    row = jnp.repeat(jnp.arange(rows), GRID_W)
    col = jnp.tile(jnp.arange(GRID_W), rows)
    quarter = HEAD_DIM // 4
    freqs = ROPE_THETA ** (-jnp.arange(0, 2 * quarter, 2, dtype=jnp.float32) / (2 * quarter))
    ar = row.astype(jnp.float32)[:, None] * freqs[None, :]
    ac = col.astype(jnp.float32)[:, None] * freqs[None, :]
    cr, sr, cc, sc = jnp.cos(ar), jnp.sin(ar), jnp.cos(ac), jnp.sin(ac)
    cos = jnp.concatenate([cr, cr, cc, cc], axis=-1)
    sin = jnp.concatenate([-sr, sr, -sc, sc], axis=-1)
    return cos.T, sin.T


def _norm_rope_t(xt, gain_col, cos_t, sin_t):
    quarter = HEAD_DIM // 4
    y = xt * lax.rsqrt(jnp.mean(xt * xt, axis=0, keepdims=True) + EPS) * gain_col
    partner = jnp.concatenate([y[quarter:2 * quarter], y[:quarter], y[3 * quarter:], y[2 * quarter:3 * quarter]],
                              axis=0)
    return y * cos_t + partner * sin_t


def _prep_kernel(av_ref, bv_ref, cv_ref, ck_ref, aq_ref, ak_ref, bq_ref, bk_ref, cos_ref, sin_ref, kg_ref,
                 vta_ref, vtb_ref, vtc_ref, kc_ref, nrm_ref, *, ts):
    def put_t(src_ref, n_heads, dst_ref, chunk):
        for hh in range(n_heads):
            vt = src_ref[:, hh * HEAD_DIM:(hh + 1) * HEAD_DIM].T
            for cidx in range(ts // chunk):
                dst_ref[hh, cidx, :HEAD_DIM] = vt[:, cidx * chunk:(cidx + 1) * chunk]
                dst_ref[hh, cidx, HEAD_DIM:] = jnp.ones((VT_ROWS - HEAD_DIM, chunk), dst_ref.dtype)

    put_t(av_ref, A_HEADS, vta_ref, A_TILE)
    put_t(bv_ref, B_KV_HEADS, vtb_ref, B_TILE)
    put_t(cv_ref, C_KV_HEADS, vtc_ref, C_TK)
    cos_t, sin_t, kg = cos_ref[...], sin_ref[...], kg_ref[...]
    for hh in range(C_KV_HEADS):
        kt = ck_ref[:, hh * HEAD_DIM:(hh + 1) * HEAD_DIM].astype(jnp.float32).T
        k = _norm_rope_t(kt, kg, cos_t, sin_t).T
        kc_ref[:, hh * HEAD_DIM:(hh + 1) * HEAD_DIM] = k.astype(kc_ref.dtype)

    def max_sq_norm(src_ref, n_heads):
        ones = jnp.ones((HEAD_DIM, HEAD_DIM), src_ref.dtype)
        best = None
        for hh in range(n_heads):
            x = src_ref[:, hh * HEAD_DIM:(hh + 1) * HEAD_DIM]
            row_sums = jnp.dot(x * x, ones, preferred_element_type=jnp.float32)
            top = jnp.max(row_sums, axis=0, keepdims=True)
            best = top if best is None else jnp.maximum(best, top)
        return best

    nrm_ref[0] = jnp.concatenate(
        [max_sq_norm(aq_ref, A_HEADS), max_sq_norm(ak_ref, A_HEADS), max_sq_norm(bq_ref, B_HEADS),
         max_sq_norm(bk_ref, B_KV_HEADS), jnp.zeros((SUBLANES - 4, HEAD_DIM), jnp.float32)], axis=0)


def _prep(proj, cos_t, sin_t, k_gain_col):
    s = proj.shape[0]
    ts = max(A_TILE, B_TILE, C_TK)
    assert s % ts == 0
    bf = jnp.bfloat16
    n = s // ts
    vt_spec = lambda heads, t: pl.BlockSpec((heads, ts // t, VT_ROWS, t), lambda i: (0, i, 0, 0))
    vt_shape = lambda heads, t: jax.ShapeDtypeStruct((heads, s // t, VT_ROWS, t), bf)
    return pl.pallas_call(
        functools.partial(_prep_kernel, ts=ts),
        grid=(n,),
        in_specs=[pl.BlockSpec((ts, A_W), lambda i: (i, AV * HEAD_DIM // A_W)),
                  pl.BlockSpec((ts, B_KV_HEADS * HEAD_DIM), lambda i: (i, BV // B_KV_HEADS)),
                  pl.BlockSpec((ts, C_KV_HEADS * HEAD_DIM), lambda i: (i, CV // C_KV_HEADS)),
                  pl.BlockSpec((ts, C_KV_HEADS * HEAD_DIM), lambda i: (i, CK // C_KV_HEADS)),
                  pl.BlockSpec((ts, A_W), lambda i: (i, AQ * HEAD_DIM // A_W)),
                  pl.BlockSpec((ts, A_W), lambda i: (i, AK * HEAD_DIM // A_W)),
                  pl.BlockSpec((ts, B_W), lambda i: (i, BQ * HEAD_DIM // B_W)),
                  pl.BlockSpec((ts, B_KV_HEADS * HEAD_DIM), lambda i: (i, BK // B_KV_HEADS)),
                  pl.BlockSpec((HEAD_DIM, ts), lambda i: (0, i)),
                  pl.BlockSpec((HEAD_DIM, ts), lambda i: (0, i)),
                  pl.BlockSpec((HEAD_DIM, 1), lambda i: (0, 0))],
        out_specs=[vt_spec(A_HEADS, A_TILE), vt_spec(B_KV_HEADS, B_TILE), vt_spec(C_KV_HEADS, C_TK),
                   pl.BlockSpec((ts, C_KV_HEADS * HEAD_DIM), lambda i: (i, 0)),
                   pl.BlockSpec((1, SUBLANES, HEAD_DIM), lambda i: (i, 0, 0))],
        out_shape=[vt_shape(A_HEADS, A_TILE), vt_shape(B_KV_HEADS, B_TILE), vt_shape(C_KV_HEADS, C_TK),
                   jax.ShapeDtypeStruct((s, C_KV_HEADS * HEAD_DIM), bf),
                   jax.ShapeDtypeStruct((n, SUBLANES, HEAD_DIM), jnp.float32)],
        compiler_params=_cparams(1),
        name="vt_krope_prep",
    )(proj, proj, proj, proj, proj, proj, proj, proj, cos_t, sin_t, k_gain_col)


def _scores(k_ref, head, row0, tk, qt_ref, s_ref, mx_ref, slot, bias=None):
    kt = k_ref[pl.ds(pl.multiple_of(row0, tk), tk), head * HEAD_DIM:(head + 1) * HEAD_DIM]
    s = jnp.dot(kt, qt_ref[...], preferred_element_type=jnp.float32)
    if bias is not None:
        s = s + bias
    s_ref[slot] = s
    mx_ref[slot] = jnp.max(s, axis=0, keepdims=True)


def _accumulate(vt, s_ref, mx_ref, m_ref, acc_ref, slot):
    m_old = m_ref[...]
    m_new = jnp.maximum(m_old, mx_ref[slot])
    alpha = jnp.exp2(m_old - m_new)
    p = jnp.exp2(s_ref[slot] - m_new).astype(jnp.bfloat16)
    acc_ref[...] = alpha * acc_ref[...] + jnp.dot(vt, p, preferred_element_type=jnp.float32)
    m_ref[...] = m_new


def _load_q(q_refs, qt_ref, tq, prep_t=lambda qt: qt):
    scale2 = HEAD_DIM ** -0.5 * LOG2E
    for g, q_ref in enumerate(q_refs):
        qt = prep_t(q_ref[...].astype(jnp.float32).T)
        qt_ref[:, g * tq:(g + 1) * tq] = (qt * scale2).astype(qt_ref.dtype)


def _store_gated(acc, g_refs, o_ref, tq, first=0, extra_denominator=None):
    for g, g_ref in enumerate(g_refs):
        sl = slice(g * tq, (g + 1) * tq)
        den = acc[HEAD_DIM:HEAD_DIM + 1, sl]
        if extra_denominator is not None:
            den = den + extra_denominator[g]
        o = (acc[:HEAD_DIM, sl] / den).T
        gate = g_ref[...].astype(jnp.float32)
        cols = slice((first + g) * HEAD_DIM, (first + g + 1) * HEAD_DIM)
        o_ref[:, cols] = (o * (gate * jax.nn.sigmoid(gate))).astype(o_ref.dtype)


def _band_kernel(*refs, heads, group, t, nkv, reach, has_sink, bounded):
    it = iter(refs)
    q_refs = [[next(it) for _ in range(group)] for _ in range(heads)]
    g_refs = [[next(it) for _ in range(group)] for _ in range(heads)]
    k_ref, vt_ref, bias_ref = next(it), next(it), next(it)
    sink_ref = next(it) if has_sink else None
    o_ref = next(it)
    qt_ref, s_ref, mx_ref, m_ref, acc_ref = (next(it) for _ in range(5))
    h = pl.program_id(0)
    qb = pl.program_id(1)
    n_off = 2 * reach + 1

    nsb = t // SUB
    n_sub = bias_ref.shape[1] - 1

    def block(o):
        kb = qb + (o - reach)
        return jnp.clip(kb, 0, nkv - 1), jnp.logical_and(kb >= 0, kb < nkv)

    def bias_tile(hh, o, valid):
        rows = []
        for a in range(nsb):
            cols = []
            for g in range(group):
                for b in range(nsb):
                    sub = (o - reach) * nsb + a - b + (n_sub - 1) // 2
                    cols.append(bias_ref[hh * group + g, jnp.where(valid, sub, n_sub)])
            rows.append(jnp.concatenate(cols, axis=1))
        return jnp.concatenate(rows, axis=0)

    if bounded:
        for hh in range(heads):
            _load_q(q_refs[hh], qt_ref.at[hh], t)
        width = min(n_off, nkv)
        start = jnp.clip(qb - reach, 0, nkv - width)

        def bias_window(hh):
            rows = []
            for a in range(width * nsb):
                cols = []
                for g in range(group):
                    for b in range(nsb):
                        sub = (start - qb) * nsb + a - b + (n_sub - 1) // 2
                        inside = jnp.logical_and(sub >= 0, sub < n_sub)
                        cols.append(bias_ref[hh * group + g, jnp.where(inside, sub, n_sub)])
                rows.append(jnp.concatenate(cols, axis=1))
            return jnp.concatenate(rows, axis=0)

        for hh in range(heads):
            keys = k_ref[pl.ds(pl.multiple_of(start * t, t), width * t), hh * HEAD_DIM:(hh + 1) * HEAD_DIM]
            s = jnp.dot(keys, qt_ref[hh], preferred_element_type=jnp.float32) + bias_window(hh)
            p = jnp.exp2(s).astype(jnp.bfloat16)
            total = None
            for o in range(width):
                part = jnp.dot(vt_ref[hh, start + o], p[o * t:(o + 1) * t],
                               preferred_element_type=jnp.float32)
                total = part if total is None else total + part
            sinks = None
            if has_sink:
                sinks = [jnp.exp2(jnp.full((1, t), sink_ref[(h * heads + hh) * group + g] * LOG2E, jnp.float32))
                         for g in range(group)]
            _store_gated(total, g_refs[hh], o_ref, t, hh * group, sinks)
        return

    for hh in range(heads):
        _load_q(q_refs[hh], qt_ref.at[hh], t)
        acc_ref[hh, :HEAD_DIM] = jnp.zeros((HEAD_DIM, group * t), jnp.float32)
        if has_sink:
            for g in range(group):
                sink = sink_ref[(h * heads + hh) * group + g] * LOG2E
                m_ref[hh, :, g * t:(g + 1) * t] = jnp.full((1, t), sink, jnp.float32)
            acc_ref[hh, HEAD_DIM:] = jnp.ones((VT_ROWS - HEAD_DIM, group * t), jnp.float32)
        else:
            m_ref[hh] = jnp.full((1, group * t), M_INIT, jnp.float32)
            acc_ref[hh, HEAD_DIM:] = jnp.zeros((VT_ROWS - HEAD_DIM, group * t), jnp.float32)

    units = [(hh, o) for hh in range(heads) for o in range(n_off)]

    def scores(u):
        hh, o = units[u]
        kb, valid = block(o)
        _scores(k_ref, hh, kb * t, t, qt_ref.at[hh], s_ref, mx_ref, u % 2, bias_tile(hh, o, valid))

    scores(0)
    for u, (hh, o) in enumerate(units):
        if u + 1 < len(units):
            scores(u + 1)
        _accumulate(vt_ref[hh, block(o)[0]], s_ref, mx_ref, m_ref.at[hh], acc_ref.at[hh], u % 2)
        if o == n_off - 1:
            _store_gated(acc_ref.at[hh], g_refs[hh], o_ref, t, hh * group)


def _dense_kernel(*refs, group, tq, tk, nkv, bounded):
    it = iter(refs)
    q_refs = [next(it) for _ in range(group)]
    g_refs = [next(it) for _ in range(group)]
    k_ref, vt_ref, cos_ref, sin_ref, qg_ref, o_ref = (next(it) for _ in range(6))
    qt_ref, s_ref, mx_ref, m_ref, acc_ref = (next(it) for _ in range(5))
    qt_ref, m_ref, acc_ref = qt_ref.at[0], m_ref.at[0], acc_ref.at[0]

    _load_q(q_refs, qt_ref, tq, lambda xt: _norm_rope_t(xt, qg_ref[...], cos_ref[...], sin_ref[...]))
    acc_ref[...] = jnp.zeros(acc_ref.shape, jnp.float32)

    if bounded:
        def block(jblk):
            kt = k_ref[pl.ds(pl.multiple_of(jblk * tk, tk), tk), :]
            s = jnp.dot(kt, qt_ref[...], preferred_element_type=jnp.float32)
            return jnp.dot(vt_ref[0, jblk], jnp.exp2(s).astype(jnp.bfloat16),
                           preferred_element_type=jnp.float32)

        per_iter = math.gcd(nkv, BOUNDED_GROUP)

        def blocks(it, carry):
            part = block(it * per_iter)
            for u in range(1, per_iter):
                part = part + block(it * per_iter + u)
            acc_ref[...] += part
            return carry

        lax.fori_loop(0, nkv // per_iter, blocks, 0)
        _store_gated(acc_ref, g_refs, o_ref, tq)
        return

    m_ref[...] = jnp.full(m_ref.shape, M_INIT, jnp.float32)

    def scores(jblk, slot):
        _scores(k_ref, 0, jblk * tk, tk, qt_ref, s_ref, mx_ref, slot)

    def accumulate(jblk, slot):
        _accumulate(vt_ref[0, jblk], s_ref, mx_ref, m_ref, acc_ref, slot)

    scores(0, 0)

    def pair(j0):
        scores(j0 + 1, 1)
        accumulate(j0, 0)
        scores(j0 + 2, 0)
        accumulate(j0 + 1, 1)

    def pairs(it, carry):
        for u in range(DENSE_UNROLL):
            pair(2 * (it * DENSE_UNROLL + u))
        return carry

    n_pairs = nkv // 2 - 1
    lax.fori_loop(0, n_pairs // DENSE_UNROLL, pairs, 0)
    for jj in range(n_pairs - n_pairs % DENSE_UNROLL, n_pairs):
        pair(2 * jj)
    scores(nkv - 1, 1)
    accumulate(nkv - 2, 0)
    accumulate(nkv - 1, 1)
    _store_gated(acc_ref, g_refs, o_ref, tq)


def _attention(kern, proj, k_src, k_col, vt, q_col, g_col, n_kv_heads, group, tq, tk, extra_specs,
               extra_args, name, heads=1, bounded=False):
    s = proj.shape[0]
    parked = SUBLANES if bounded else tk
    assert s % tq == 0 and s % tk == 0 and n_kv_heads % heads == 0 and k_col % heads == 0
    in_specs, args = [], []
    for col in (q_col, g_col):
        for hh in range(heads):
            for g in range(group):
                in_specs.append(pl.BlockSpec(
                    (tq, HEAD_DIM), lambda h, qb, col=col, hh=hh, g=g: (qb, col + (h * heads + hh) * group + g)))
                args.append(proj)
    in_specs += [pl.BlockSpec((s, heads * HEAD_DIM), lambda h, qb: (0, k_col // heads + h)),
                 pl.BlockSpec((heads, s // tk, VT_ROWS, tk), lambda h, qb: (h, 0, 0, 0))] + extra_specs
    args += [k_src, vt] + extra_args
    n = group * tq
    return pl.pallas_call(
        kern,
        grid=(n_kv_heads // heads, s // tq),
        in_specs=in_specs,
        out_specs=pl.BlockSpec((tq, heads * group * HEAD_DIM), lambda h, qb: (qb, h)),
        out_shape=jax.ShapeDtypeStruct((s, n_kv_heads * group * HEAD_DIM), jnp.bfloat16),
        scratch_shapes=[pltpu.VMEM((heads, HEAD_DIM, n), jnp.bfloat16),
                        pltpu.VMEM((2, parked, n), jnp.float32),
                        pltpu.VMEM((2, 1, n), jnp.float32),
                        pltpu.VMEM((heads, 1, n), jnp.float32),
                        pltpu.VMEM((heads, VT_ROWS, n), jnp.float32)],
        compiler_params=_cparams(2),
        name=name,
    )(*args)


def _band_attention(proj, k_col, vt, q_col, g_col, n_kv_heads, group, t, reach, bias, sink, bounded,
                    heads=BAND_HEADS_PER_STEP):
    nkv = proj.shape[0] // t
    extra_specs = [pl.BlockSpec((heads * group,) + bias.shape[1:], lambda h, qb: (h, 0, 0, 0))]
    extra_args = [bias]
    if sink is not None:
        extra_specs.append(pl.BlockSpec(memory_space=pltpu.SMEM))
        extra_args.append(sink)
    kern = functools.partial(_band_kernel, heads=heads, group=group, t=t, nkv=nkv, reach=reach,
                             has_sink=sink is not None, bounded=bounded)
    name = "band_attention_g%d%s" % (group, "_bounded" if bounded else "")
    return _attention(kern, proj, proj, k_col, vt, q_col, g_col, n_kv_heads, group, t, t,
                      extra_specs, extra_args, name, heads, bounded)


def _dense_attention(proj, kc, vt, q_col, g_col, n_kv_heads, group, tk, cos_t, sin_t, q_gain_col, bounded):
    nkv = proj.shape[0] // tk
    tq = C_TQ[bounded]
    assert nkv % 2 == 0
    extra_specs = [pl.BlockSpec((HEAD_DIM, tq), lambda h, qb: (0, qb)),
                   pl.BlockSpec((HEAD_DIM, tq), lambda h, qb: (0, qb)),
                   pl.BlockSpec((HEAD_DIM, 1), lambda h, qb: (0, 0))]
    kern = functools.partial(_dense_kernel, group=group, tq=tq, tk=tk, nkv=nkv, bounded=bounded)
    name = "dense_attention_g%d%s" % (group, "_bounded" if bounded else "")
    return _attention(kern, proj, kc, 0, vt, q_col, g_col, n_kv_heads, group, tq, tk,
                      extra_specs, [cos_t, sin_t, q_gain_col], name, 1, bounded)


def _t5_bucket_np(rel):
    half_b = NUM_BUCKETS // 2
    max_exact = half_b // 2
    ret = np.where(rel > 0, half_b, 0)
    n = np.abs(rel)
    nf = np.maximum(n, 1).astype(np.float32)
    large = max_exact + (np.log(nf / np.float32(max_exact)) / np.float32(math.log(MAX_DISTANCE / max_exact))
                         * np.float32(half_b - max_exact)).astype(np.int32)
    large = np.minimum(large, half_b - 1)
    return (ret + np.where(n < max_exact, n, large)).astype(np.int32)


def _toeplitz_tiles(vec, tile, reach):
    heads = vec.shape[0]
    r = (vec.shape[1] - 1) // 2
    n_off = 2 * reach + 1
    span = 2 * tile - 1
    lo = -reach * tile - (tile - 1)
    hi = reach * tile + (tile - 1)
    padded = jnp.pad(vec, ((0, 0), (-lo - r, hi - r)), constant_values=MASKED)
    rows = [padded[:, o * tile:o * tile + span] for o in range(n_off)]
    u = jnp.stack(rows, axis=1)[:, :, ::-1]
    w = jnp.pad(u, ((0, 0), (0, 0), (0, 1)))
    flat = jnp.broadcast_to(w[:, :, None, :], (heads, n_off, tile, span + 1))
    flat = flat.reshape(heads, n_off, tile * (span + 1))[:, :, :tile * span]
    skew = flat.reshape(heads, n_off, tile, span)
    tiles = skew[:, :, :, tile - 1:]
    masked = jnp.full((tiles.shape[0], 1) + tiles.shape[2:], MASKED, tiles.dtype)
    return jnp.concatenate([tiles, masked], axis=1)


def _bias_tiles(rel_bias):
    hi = lax.Precision.HIGHEST
    d = np.arange(-A_REACH, A_REACH + 1)
    count = np.zeros(d.shape, np.float64)
    for w, dil in A_PATTERNS:
        count += (d % dil == 0) & (np.abs(d) <= w // 2)
    log_count = np.where(count > 0, np.log(np.maximum(count, 1.0)), 0.0).astype(np.float32)
    table = rel_bias.astype(jnp.float32)
    onehot_a = jnp.asarray(np.eye(NUM_BUCKETS, dtype=np.float32)[_t5_bucket_np(d)])
    vec_a = (jnp.dot(onehot_a, table[:, :A_HEADS], precision=hi).T + log_count[None]) * LOG2E
    vec_a = jnp.where(jnp.asarray(count > 0)[None], vec_a, MASKED)
    tiles_a = _toeplitz_tiles(vec_a, SUB, (-(-A_REACH // A_TILE) + 1) * (A_TILE // SUB) - 1)
    db = np.arange(-B_HALF_WINDOW, B_HALF_WINDOW + 1)
    onehot_b = jnp.asarray(np.eye(NUM_BUCKETS, dtype=np.float32)[_t5_bucket_np(db)])
    vec_b = jnp.dot(onehot_b, table[:, A_HEADS:], precision=hi).T * LOG2E
    tiles_b = _toeplitz_tiles(vec_b, SUB, (-(-B_HALF_WINDOW // B_TILE) + 1) * (B_TILE // SUB) - 1)
    top_a = jnp.max(jnp.abs(jnp.where(jnp.asarray(count > 0)[None], vec_a, 0.0)))
    return tiles_a, tiles_b, top_a, jnp.max(jnp.abs(vec_b))


def kernel(x, c, w_mod, b_mod, pre_norm, post_norm, w_in, w_out, q_norm, k_norm, sink, rel_bias):
    bsz, s, d = x.shape
    assert bsz == 1 and c.shape[0] == 1
    depth = w_in.shape[0]
    assert w_in.shape[2] == IN_WIDTH and w_out.shape[1] == A_W + B_W + C_W
    x2 = x[0]
    cos_t, sin_t = _rope_tables(s)
    tiles_a, tiles_b, top_bias_a, top_bias_b = _bias_tiles(rel_bias)
    mod = _modulation(c, w_mod, b_mod)[:, 0]
    shift, scale, gate = mod[:, :d], mod[:, d:2 * d], mod[:, 2 * d:]
    vec = lambda a, i: a[i][None, :]

    h = _prenorm(x2, vec(pre_norm, 0), vec(scale, 0), vec(shift, 0))
    for i in range(depth):
        proj = _in_projection(h, w_in, i)
        vta, vtb, vtc, kc, sq_norms = _prep(proj, cos_t, sin_t, k_norm[i][:, None])
        sq_norms = jnp.max(sq_norms, axis=(0, 2))
        qk_scale = NORM_SLACK * HEAD_DIM ** -0.5 * LOG2E
        bound_a = qk_scale * jnp.sqrt(sq_norms[0] * sq_norms[1]) + top_bias_a
        bound_b = jnp.maximum(qk_scale * jnp.sqrt(sq_norms[2] * sq_norms[3]) + top_bias_b,
                              jnp.max(jnp.abs(sink[i])) * LOG2E)
        band_a = functools.partial(_band_attention, proj, AK, vta, AQ, AG, A_HEADS, 1, A_TILE,
                                   -(-A_REACH // A_TILE), tiles_a, None)
        band_b = functools.partial(_band_attention, proj, BK, vtb, BQ, BG, B_KV_HEADS, B_HEADS // B_KV_HEADS,
                                   B_TILE, -(-B_HALF_WINDOW // B_TILE), tiles_b, sink[i])
        oa = lax.cond(bound_a <= LOGIT_BOUND, lambda: band_a(True), lambda: band_a(False))
        ob = lax.cond(bound_b <= LOGIT_BOUND, lambda: band_b(True), lambda: band_b(False))
        dense = functools.partial(_dense_attention, proj, kc, vtc, CQ, CG, C_KV_HEADS,
                                  C_HEADS // C_KV_HEADS, C_TK, cos_t, sin_t, q_norm[i][:, None])
        logit_bound = (HEAD_DIM ** 0.5 * LOG2E) * jnp.max(jnp.abs(q_norm[i])) * jnp.max(jnp.abs(k_norm[i]))
        oc = lax.cond(logit_bound <= LOGIT_BOUND, lambda: dense(True), lambda: dense(False))
        y = _out_projection(oa, ob, oc, w_out, i)
        nxt = None if i == depth - 1 else (vec(pre_norm, i + 1), vec(scale, i + 1), vec(shift, i + 1))
        x2, h = _post(x2, y, vec(post_norm, i), vec(gate, i), nxt)
    return x2[None]
```

```python
import functools
import math

import numpy as np
import jax
import jax.numpy as jnp
from jax import lax
from jax.experimental import pallas as pl
from jax.experimental.pallas import tpu as pltpu

HEAD_DIM = 128
GRID_W = 64
A_HEADS = 12
A_PATTERNS = ((128, 1), (512, 4), (2048, 16))
B_HEADS = 8
B_KV_HEADS = 2
B_HALF_WINDOW = 128
C_HEADS = 12
C_KV_HEADS = 4
ROPE_THETA = 10000.0
NUM_BUCKETS = 32
MAX_DISTANCE = 1024
EPS = 1e-6
LOG2E = math.log2(math.e)
MASKED = -1e30
M_INIT = -1e29

A_W = A_HEADS * HEAD_DIM
B_W = B_HEADS * HEAD_DIM
C_W = C_HEADS * HEAD_DIM
AQ, AK, AV, AG = 0, 12, 24, 36
BQ, BK, BV, BG = 48, 56, 58, 60
CQ, CK, CV, CG = 68, 80, 84, 88
IN_WIDTH = 100 * HEAD_DIM

A_REACH = max(w // 2 for w, _ in A_PATTERNS)
A_TILE = 512
B_TILE = 256
C_TQ = {True: 1024, False: 256}
C_TK = 1024
BAND_HEADS_PER_STEP = 2
A_HEADS_PER_STEP = 3
BOUNDED_GROUP = 4
NORM_SLACK = 1.02
LOGIT_BOUND = 64.0
DENSE_UNROLL = 3
VT_ROWS = HEAD_DIM + 16
V7X_VMEM_LIMIT = 56 * 1024 * 1024
SUBLANES = 8
SUB = 128


def _cparams(n_axes):
    return pltpu.CompilerParams(dimension_semantics=("arbitrary",) * n_axes,
                                vmem_limit_bytes=V7X_VMEM_LIMIT)


def _tile(n, pref):
    t = min(n, pref)
    while n % t:
        t //= 2
    return t


def _mod_kernel(c_ref, w_ref, b_ref, o_ref):
    c = c_ref[...]
    cs = jnp.broadcast_to(c * jax.nn.sigmoid(c), (SUBLANES, c.shape[1]))
    r = jnp.dot(cs, w_ref[0], preferred_element_type=jnp.float32)
    o_ref[0] = r[0:1] + b_ref[0]


def _modulation(c, w_mod, b_mod):
    depth, d, n = w_mod.shape
    tn = _tile(n, 1024)
    out = pl.pallas_call(
        _mod_kernel,
        grid=(depth, n // tn),
        in_specs=[pl.BlockSpec((1, d), lambda i, j: (0, 0)),
                  pl.BlockSpec((1, d, tn), lambda i, j: (i, 0, j)),
                  pl.BlockSpec((1, 1, tn), lambda i, j: (i, 0, j))],
        out_specs=pl.BlockSpec((1, 1, tn), lambda i, j: (i, 0, j)),
        out_shape=jax.ShapeDtypeStruct((depth, 1, n), jnp.float32),
        compiler_params=_cparams(2),
        name="modulation",
    )(c, w_mod, b_mod.reshape(depth, 1, n))
    return out


def _rms(x):
    return x * lax.rsqrt(jnp.mean(x * x, axis=-1, keepdims=True) + EPS)


def _prenorm_kernel(x_ref, g_ref, sc_ref, sh_ref, h_ref):
    y = _rms(x_ref[...]) * g_ref[...]
    h_ref[...] = (y * (1.0 + sc_ref[...]) + sh_ref[...]).astype(h_ref.dtype)


def _prenorm(x, g, scale, shift):
    s, d = x.shape
    tm = _tile(s, 512)
    vec = pl.BlockSpec((1, d), lambda i: (0, 0))
    return pl.pallas_call(
        _prenorm_kernel,
        grid=(s // tm,),
        in_specs=[pl.BlockSpec((tm, d), lambda i: (i, 0)), vec, vec, vec],
        out_specs=pl.BlockSpec((tm, d), lambda i: (i, 0)),
        out_shape=jax.ShapeDtypeStruct((s, d), jnp.bfloat16),
        compiler_params=_cparams(1),
        name="prenorm",
    )(x, g, scale, shift)


def _post_kernel(x_ref, y_ref, pg_ref, gate_ref, xo_ref):
    yn = _rms(y_ref[...].astype(jnp.float32)) * pg_ref[...]
    xo_ref[...] = x_ref[...] + gate_ref[...] * yn


def _postpre_kernel(x_ref, y_ref, pg_ref, gate_ref, g_ref, sc_ref, sh_ref, xo_ref, h_ref):
    yn = _rms(y_ref[...].astype(jnp.float32)) * pg_ref[...]
    xn = x_ref[...] + gate_ref[...] * yn
    xo_ref[...] = xn
    h = _rms(xn) * g_ref[...]
    h_ref[...] = (h * (1.0 + sc_ref[...]) + sh_ref[...]).astype(h_ref.dtype)


def _post(x, y, post_g, gate, nxt=None):
    s, d = x.shape
    tm = _tile(s, 256)
    row = pl.BlockSpec((tm, d), lambda i: (i, 0))
    vec = pl.BlockSpec((1, d), lambda i: (0, 0))
    if nxt is None:
        return pl.pallas_call(
            _post_kernel, grid=(s // tm,),
            in_specs=[row, row, vec, vec], out_specs=row,
            out_shape=jax.ShapeDtypeStruct((s, d), jnp.float32),
            compiler_params=_cparams(1), name="postnorm",
        )(x, y, post_g, gate), None
    g, scale, shift = nxt
    return pl.pallas_call(
        _postpre_kernel, grid=(s // tm,),
        in_specs=[row, row, vec, vec, vec, vec, vec], out_specs=[row, row],
        out_shape=[jax.ShapeDtypeStruct((s, d), jnp.float32),
                   jax.ShapeDtypeStruct((s, d), jnp.bfloat16)],
        compiler_params=_cparams(1), name="postnorm_prenorm",
    )(x, y, post_g, gate, g, scale, shift)


def _mm_kernel(a_ref, b_ref, o_ref):
    o_ref[...] = jnp.dot(a_ref[...], b_ref[0].astype(a_ref.dtype),
                         preferred_element_type=jnp.float32).astype(o_ref.dtype)


def _in_projection(h, w, layer):
    m, k = h.shape
    n = w.shape[2]
    tm, tn = _tile(m, 1024), _tile(n, 512)
    return pl.pallas_call(
        _mm_kernel,
        grid=(m // tm, n // tn),
        in_specs=[pl.BlockSpec((tm, k), lambda i, j: (i, 0)),
                  pl.BlockSpec((1, k, tn), lambda i, j: (layer, 0, j))],
        out_specs=pl.BlockSpec((tm, tn), lambda i, j: (i, j)),
        out_shape=jax.ShapeDtypeStruct((m, n), jnp.bfloat16),
        compiler_params=_cparams(2),
        name="in_projection",
    )(h, w)


def _mm3_kernel(a1, a2, a3, w_ref, o_ref, wb_ref):
    @pl.when(pl.program_id(1) == 0)
    def _():
        wb_ref[...] = w_ref[0].astype(wb_ref.dtype)

    acc, row = None, 0
    for a in (a1, a2, a3):
        part = jnp.dot(a[...], wb_ref[row:row + a.shape[1]], preferred_element_type=jnp.float32)
        acc = part if acc is None else acc + part
        row += a.shape[1]
    o_ref[...] = acc.astype(o_ref.dtype)


def _out_projection(oa, ob, oc, w, layer):
    m = oa.shape[0]
    k, n = w.shape[1:]
    tm, tn = _tile(m, 1024), _tile(n, 512)
    a_spec = lambda a: pl.BlockSpec((tm, a.shape[1]), lambda j, i: (i, 0))
    return pl.pallas_call(
        _mm3_kernel,
        grid=(n // tn, m // tm),
        in_specs=[a_spec(oa), a_spec(ob), a_spec(oc),
                  pl.BlockSpec((1, k, tn), lambda j, i: (layer, 0, j))],
        out_specs=pl.BlockSpec((tm, tn), lambda j, i: (i, j)),
        out_shape=jax.ShapeDtypeStruct((m, n), jnp.bfloat16),
        scratch_shapes=[pltpu.VMEM((k, tn), oa.dtype)],
        compiler_params=_cparams(2),
        name="out_projection",
    )(oa, ob, oc, w)


def _rope_tables(s):
    rows = s // GRID_W
    row = jnp.repeat(jnp.arange(rows), GRID_W)
    col = jnp.tile(jnp.arange(GRID_W), rows)
    quarter = HEAD_DIM // 4
    freqs = ROPE_THETA ** (-jnp.arange(0, 2 * quarter, 2, dtype=jnp.float32) / (2 * quarter))
    ar = row.astype(jnp.float32)[:, None] * freqs[None, :]
    ac = col.astype(jnp.float32)[:, None] * freqs[None, :]
    cr, sr, cc, sc = jnp.cos(ar), jnp.sin(ar), jnp.cos(ac), jnp.sin(ac)
    cos = jnp.concatenate([cr, cr, cc, cc], axis=-1)
    sin = jnp.concatenate([-sr, sr, -sc, sc], axis=-1)
    return cos.T, sin.T


def _norm_rope_t(xt, gain_col, cos_t, sin_t):
    quarter = HEAD_DIM // 4
    y = xt * lax.rsqrt(jnp.mean(xt * xt, axis=0, keepdims=True) + EPS) * gain_col
    partner = jnp.concatenate([y[quarter:2 * quarter], y[:quarter], y[3 * quarter:], y[2 * quarter:3 * quarter]],
                              axis=0)
    return y * cos_t + partner * sin_t


def _prep_kernel(av_ref, bv_ref, cv_ref, ck_ref, aq_ref, ak_ref, bq_ref, bk_ref, cos_ref, sin_ref, kg_ref,
                 vta_ref, vtb_ref, vtc_ref, kc_ref, nrm_ref, *, ts):
    def put_t(src_ref, n_heads, dst_ref, chunk):
        for hh in range(n_heads):
            vt = src_ref[:, hh * HEAD_DIM:(hh + 1) * HEAD_DIM].T
            for cidx in range(ts // chunk):
                dst_ref[hh, cidx, :HEAD_DIM] = vt[:, cidx * chunk:(cidx + 1) * chunk]
                dst_ref[hh, cidx, HEAD_DIM:] = jnp.ones((VT_ROWS - HEAD_DIM, chunk), dst_ref.dtype)

    put_t(av_ref, A_HEADS, vta_ref, A_TILE)
    put_t(bv_ref, B_KV_HEADS, vtb_ref, B_TILE)
    put_t(cv_ref, C_KV_HEADS, vtc_ref, C_TK)
    cos_t, sin_t, kg = cos_ref[...], sin_ref[...], kg_ref[...]
    for hh in range(C_KV_HEADS):
        kt = ck_ref[:, hh * HEAD_DIM:(hh + 1) * HEAD_DIM].astype(jnp.float32).T
        k = _norm_rope_t(kt, kg, cos_t, sin_t).T
        kc_ref[:, hh * HEAD_DIM:(hh + 1) * HEAD_DIM] = k.astype(kc_ref.dtype)

    def max_sq_norm(src_ref, n_heads):
        ones = jnp.ones((HEAD_DIM, HEAD_DIM), src_ref.dtype)
        best = None
        for hh in range(n_heads):
            x = src_ref[:, hh * HEAD_DIM:(hh + 1) * HEAD_DIM]
            row_sums = jnp.dot(x * x, ones, preferred_element_type=jnp.float32)
            top = jnp.max(row_sums, axis=0, keepdims=True)
            best = top if best is None else jnp.maximum(best, top)
        return best

    nrm_ref[0] = jnp.concatenate(
        [max_sq_norm(aq_ref, A_HEADS), max_sq_norm(ak_ref, A_HEADS), max_sq_norm(bq_ref, B_HEADS),
         max_sq_norm(bk_ref, B_KV_HEADS), jnp.zeros((SUBLANES - 4, HEAD_DIM), jnp.float32)], axis=0)


def _prep(proj, cos_t, sin_t, k_gain_col):
    s = proj.shape[0]
    ts = max(A_TILE, B_TILE, C_TK)
    assert s % ts == 0
    bf = jnp.bfloat16
    n = s // ts
    vt_spec = lambda heads, t: pl.BlockSpec((heads, ts // t, VT_ROWS, t), lambda i: (0, i, 0, 0))
    vt_shape = lambda heads, t: jax.ShapeDtypeStruct((heads, s // t, VT_ROWS, t), bf)
    return pl.pallas_call(
        functools.partial(_prep_kernel, ts=ts),
        grid=(n,),
        in_specs=[pl.BlockSpec((ts, A_W), lambda i: (i, AV * HEAD_DIM // A_W)),
                  pl.BlockSpec((ts, B_KV_HEADS * HEAD_DIM), lambda i: (i, BV // B_KV_HEADS)),
                  pl.BlockSpec((ts, C_KV_HEADS * HEAD_DIM), lambda i: (i, CV // C_KV_HEADS)),
                  pl.BlockSpec((ts, C_KV_HEADS * HEAD_DIM), lambda i: (i, CK // C_KV_HEADS)),
                  pl.BlockSpec((ts, A_W), lambda i: (i, AQ * HEAD_DIM // A_W)),
                  pl.BlockSpec((ts, A_W), lambda i: (i, AK * HEAD_DIM // A_W)),
                  pl.BlockSpec((ts, B_W), lambda i: (i, BQ * HEAD_DIM // B_W)),
                  pl.BlockSpec((ts, B_KV_HEADS * HEAD_DIM), lambda i: (i, BK // B_KV_HEADS)),
                  pl.BlockSpec((HEAD_DIM, ts), lambda i: (0, i)),
                  pl.BlockSpec((HEAD_DIM, ts), lambda i: (0, i)),
                  pl.BlockSpec((HEAD_DIM, 1), lambda i: (0, 0))],
        out_specs=[vt_spec(A_HEADS, A_TILE), vt_spec(B_KV_HEADS, B_TILE), vt_spec(C_KV_HEADS, C_TK),
                   pl.BlockSpec((ts, C_KV_HEADS * HEAD_DIM), lambda i: (i, 0)),
                   pl.BlockSpec((1, SUBLANES, HEAD_DIM), lambda i: (i, 0, 0))],
        out_shape=[vt_shape(A_HEADS, A_TILE), vt_shape(B_KV_HEADS, B_TILE), vt_shape(C_KV_HEADS, C_TK),
                   jax.ShapeDtypeStruct((s, C_KV_HEADS * HEAD_DIM), bf),
                   jax.ShapeDtypeStruct((n, SUBLANES, HEAD_DIM), jnp.float32)],
        compiler_params=_cparams(1),
        name="vt_krope_prep",
    )(proj, proj, proj, proj, proj, proj, proj, proj, cos_t, sin_t, k_gain_col)


def _scores(k_ref, head, row0, tk, qt_ref, s_ref, mx_ref, slot, bias=None):
    kt = k_ref[pl.ds(pl.multiple_of(row0, tk), tk), head * HEAD_DIM:(head + 1) * HEAD_DIM]
    s = jnp.dot(kt, qt_ref[...], preferred_element_type=jnp.float32)
    if bias is not None:
        s = s + bias
    s_ref[slot] = s
    mx_ref[slot] = jnp.max(s, axis=0, keepdims=True)


def _accumulate(vt, s_ref, mx_ref, m_ref, acc_ref, slot):
    m_old = m_ref[...]
    m_new = jnp.maximum(m_old, mx_ref[slot])
    alpha = jnp.exp2(m_old - m_new)
    p = jnp.exp2(s_ref[slot] - m_new).astype(jnp.bfloat16)
    acc_ref[...] = alpha * acc_ref[...] + jnp.dot(vt, p, preferred_element_type=jnp.float32)
    m_ref[...] = m_new


def _load_q(q_refs, qt_ref, tq, prep_t=lambda qt: qt):
    scale2 = HEAD_DIM ** -0.5 * LOG2E
    for g, q_ref in enumerate(q_refs):
        qt = prep_t(q_ref[...].astype(jnp.float32).T)
        qt_ref[:, g * tq:(g + 1) * tq] = (qt * scale2).astype(qt_ref.dtype)


def _store_gated(acc, g_refs, o_ref, tq, first=0, extra_denominator=None):
    for g, g_ref in enumerate(g_refs):
        sl = slice(g * tq, (g + 1) * tq)
        den = acc[HEAD_DIM:HEAD_DIM + 1, sl]
        if extra_denominator is not None:
            den = den + extra_denominator[g]
        o = (acc[:HEAD_DIM, sl] / den).T
        gate = g_ref[...].astype(jnp.float32)
        cols = slice((first + g) * HEAD_DIM, (first + g + 1) * HEAD_DIM)
        o_ref[:, cols] = (o * (gate * jax.nn.sigmoid(gate))).astype(o_ref.dtype)


def _band_kernel(*refs, heads, group, t, nkv, reach, has_sink, bounded):
    it = iter(refs)
    q_refs = [[next(it) for _ in range(group)] for _ in range(heads)]
    g_refs = [[next(it) for _ in range(group)] for _ in range(heads)]
    k_ref, vt_ref, bias_ref = next(it), next(it), next(it)
    sink_ref = next(it) if has_sink else None
    o_ref = next(it)
    qt_ref, s_ref, mx_ref, m_ref, acc_ref = (next(it) for _ in range(5))
    h = pl.program_id(0)
    qb = pl.program_id(1)
    n_off = 2 * reach + 1

    nsb = t // SUB
    n_sub = bias_ref.shape[1] - 1

    def block(o):
        kb = qb + (o - reach)
        return jnp.clip(kb, 0, nkv - 1), jnp.logical_and(kb >= 0, kb < nkv)

    def bias_tile(hh, o, valid):
        rows = []
        for a in range(nsb):
            cols = []
            for g in range(group):
                for b in range(nsb):
                    sub = (o - reach) * nsb + a - b + (n_sub - 1) // 2
                    cols.append(bias_ref[hh * group + g, jnp.where(valid, sub, n_sub)])
            rows.append(jnp.concatenate(cols, axis=1))
        return jnp.concatenate(rows, axis=0)

    if bounded:
        for hh in range(heads):
            _load_q(q_refs[hh], qt_ref.at[hh], t)
        width = min(n_off, nkv)
        start = jnp.clip(qb - reach, 0, nkv - width)

        def bias_window(hh):
            rows = []
            for a in range(width * nsb):
                cols = []
                for g in range(group):
                    for b in range(nsb):
                        sub = (start - qb) * nsb + a - b + (n_sub - 1) // 2
                        inside = jnp.logical_and(sub >= 0, sub < n_sub)
                        cols.append(bias_ref[hh * group + g, jnp.where(inside, sub, n_sub)])
                rows.append(jnp.concatenate(cols, axis=1))
            return jnp.concatenate(rows, axis=0)

        for hh in range(heads):
            keys = k_ref[pl.ds(pl.multiple_of(start * t, t), width * t), hh * HEAD_DIM:(hh + 1) * HEAD_DIM]
            s = jnp.dot(keys, qt_ref[hh], preferred_element_type=jnp.float32) + bias_window(hh)
            p = jnp.exp2(s).astype(jnp.bfloat16)
            total = None
            for o in range(width):
                part = jnp.dot(vt_ref[hh, start + o], p[o * t:(o + 1) * t],
                               preferred_element_type=jnp.float32)
                total = part if total is None else total + part
            sinks = None
            if has_sink:
                sinks = [jnp.exp2(jnp.full((1, t), sink_ref[(h * heads + hh) * group + g] * LOG2E, jnp.float32))
                         for g in range(group)]
            _store_gated(total, g_refs[hh], o_ref, t, hh * group, sinks)
        return

    for hh in range(heads):
        _load_q(q_refs[hh], qt_ref.at[hh], t)
        acc_ref[hh, :HEAD_DIM] = jnp.zeros((HEAD_DIM, group * t), jnp.float32)
        if has_sink:
            for g in range(group):
                sink = sink_ref[(h * heads + hh) * group + g] * LOG2E
                m_ref[hh, :, g * t:(g + 1) * t] = jnp.full((1, t), sink, jnp.float32)
            acc_ref[hh, HEAD_DIM:] = jnp.ones((VT_ROWS - HEAD_DIM, group * t), jnp.float32)
        else:
            m_ref[hh] = jnp.full((1, group * t), M_INIT, jnp.float32)
            acc_ref[hh, HEAD_DIM:] = jnp.zeros((VT_ROWS - HEAD_DIM, group * t), jnp.float32)

    units = [(hh, o) for hh in range(heads) for o in range(n_off)]

    def scores(u):
        hh, o = units[u]
        kb, valid = block(o)
        _scores(k_ref, hh, kb * t, t, qt_ref.at[hh], s_ref, mx_ref, u % 2, bias_tile(hh, o, valid))

    scores(0)
    for u, (hh, o) in enumerate(units):
        if u + 1 < len(units):
            scores(u + 1)
        _accumulate(vt_ref[hh, block(o)[0]], s_ref, mx_ref, m_ref.at[hh], acc_ref.at[hh], u % 2)
        if o == n_off - 1:
            _store_gated(acc_ref.at[hh], g_refs[hh], o_ref, t, hh * group)


def _dense_kernel(*refs, group, tq, tk, nkv, bounded):
    it = iter(refs)
    q_refs = [next(it) for _ in range(group)]
    g_refs = [next(it) for _ in range(group)]
    k_ref, vt_ref, cos_ref, sin_ref, qg_ref, o_ref = (next(it) for _ in range(6))
    qt_ref, s_ref, mx_ref, m_ref, acc_ref = (next(it) for _ in range(5))
    qt_ref, m_ref, acc_ref = qt_ref.at[0], m_ref.at[0], acc_ref.at[0]

    _load_q(q_refs, qt_ref, tq, lambda xt: _norm_rope_t(xt, qg_ref[...], cos_ref[...], sin_ref[...]))
    acc_ref[...] = jnp.zeros(acc_ref.shape, jnp.float32)

    if bounded:
        def block(jblk):
            kt = k_ref[pl.ds(pl.multiple_of(jblk * tk, tk), tk), :]
            s = jnp.dot(kt, qt_ref[...], preferred_element_type=jnp.float32)
            return jnp.dot(vt_ref[0, jblk], jnp.exp2(s).astype(jnp.bfloat16),
                           preferred_element_type=jnp.float32)

        per_iter = math.gcd(nkv, BOUNDED_GROUP)

        def blocks(it, carry):
            part = block(it * per_iter)
            for u in range(1, per_iter):
                part = part + block(it * per_iter + u)
            acc_ref[...] += part
            return carry

        lax.fori_loop(0, nkv // per_iter, blocks, 0)
        _store_gated(acc_ref, g_refs, o_ref, tq)
        return

    m_ref[...] = jnp.full(m_ref.shape, M_INIT, jnp.float32)

    def scores(jblk, slot):
        _scores(k_ref, 0, jblk * tk, tk, qt_ref, s_ref, mx_ref, slot)

    def accumulate(jblk, slot):
        _accumulate(vt_ref[0, jblk], s_ref, mx_ref, m_ref, acc_ref, slot)

    scores(0, 0)

    def pair(j0):
        scores(j0 + 1, 1)
        accumulate(j0, 0)
        scores(j0 + 2, 0)
        accumulate(j0 + 1, 1)

    def pairs(it, carry):
        for u in range(DENSE_UNROLL):
            pair(2 * (it * DENSE_UNROLL + u))
        return carry

    n_pairs = nkv // 2 - 1
    lax.fori_loop(0, n_pairs // DENSE_UNROLL, pairs, 0)
    for jj in range(n_pairs - n_pairs % DENSE_UNROLL, n_pairs):
        pair(2 * jj)
    scores(nkv - 1, 1)
    accumulate(nkv - 2, 0)
    accumulate(nkv - 1, 1)
    _store_gated(acc_ref, g_refs, o_ref, tq)


def _resident(heads):
    return dict(pipeline_mode=pl.Buffered(1)) if heads > 2 else {}


def _attention(kern, proj, k_src, k_col, vt, q_col, g_col, n_kv_heads, group, tq, tk, extra_specs,
               extra_args, name, heads=1, bounded=False):
    s = proj.shape[0]
    parked = SUBLANES if bounded else tk
    assert s % tq == 0 and s % tk == 0 and n_kv_heads % heads == 0 and k_col % heads == 0
    in_specs, args = [], []
    for col in (q_col, g_col):
        for hh in range(heads):
            for g in range(group):
                in_specs.append(pl.BlockSpec(
                    (tq, HEAD_DIM), lambda h, qb, col=col, hh=hh, g=g: (qb, col + (h * heads + hh) * group + g)))
                args.append(proj)
    in_specs += [pl.BlockSpec((s, heads * HEAD_DIM), lambda h, qb: (0, k_col // heads + h),
                              **_resident(heads)),
                 pl.BlockSpec((heads, s // tk, VT_ROWS, tk), lambda h, qb: (h, 0, 0, 0),
                              **_resident(heads))] + extra_specs
    args += [k_src, vt] + extra_args
    n = group * tq
    return pl.pallas_call(
        kern,
        grid=(n_kv_heads // heads, s // tq),
        in_specs=in_specs,
        out_specs=pl.BlockSpec((tq, heads * group * HEAD_DIM), lambda h, qb: (qb, h)),
        out_shape=jax.ShapeDtypeStruct((s, n_kv_heads * group * HEAD_DIM), jnp.bfloat16),
        scratch_shapes=[pltpu.VMEM((heads, HEAD_DIM, n), jnp.bfloat16),
                        pltpu.VMEM((2, parked, n), jnp.float32),
                        pltpu.VMEM((2, 1, n), jnp.float32),
                        pltpu.VMEM((heads, 1, n), jnp.float32),
                        pltpu.VMEM((heads, VT_ROWS, n), jnp.float32)],
        compiler_params=_cparams(2),
        name=name,
    )(*args)


def _band_attention(proj, k_col, vt, q_col, g_col, n_kv_heads, group, t, reach, bias, sink, bounded,
                    heads=BAND_HEADS_PER_STEP):
    nkv = proj.shape[0] // t
    extra_specs = [pl.BlockSpec((heads * group,) + bias.shape[1:], lambda h, qb: (h, 0, 0, 0),
                                **_resident(heads))]
    extra_args = [bias]
    if sink is not None:
        extra_specs.append(pl.BlockSpec(memory_space=pltpu.SMEM))
        extra_args.append(sink)
    kern = functools.partial(_band_kernel, heads=heads, group=group, t=t, nkv=nkv, reach=reach,
                             has_sink=sink is not None, bounded=bounded)
    name = "band_attention_g%d%s" % (group, "_bounded" if bounded else "")
    return _attention(kern, proj, proj, k_col, vt, q_col, g_col, n_kv_heads, group, t, t,
                      extra_specs, extra_args, name, heads, bounded)


def _dense_attention(proj, kc, vt, q_col, g_col, n_kv_heads, group, tk, cos_t, sin_t, q_gain_col, bounded):
    nkv = proj.shape[0] // tk
    tq = C_TQ[bounded]
    assert nkv % 2 == 0
    extra_specs = [pl.BlockSpec((HEAD_DIM, tq), lambda h, qb: (0, qb)),
                   pl.BlockSpec((HEAD_DIM, tq), lambda h, qb: (0, qb)),
                   pl.BlockSpec((HEAD_DIM, 1), lambda h, qb: (0, 0))]
    kern = functools.partial(_dense_kernel, group=group, tq=tq, tk=tk, nkv=nkv, bounded=bounded)
    name = "dense_attention_g%d%s" % (group, "_bounded" if bounded else "")
    return _attention(kern, proj, kc, 0, vt, q_col, g_col, n_kv_heads, group, tq, tk,
                      extra_specs, [cos_t, sin_t, q_gain_col], name, 1, bounded)


def _t5_bucket_np(rel):
    half_b = NUM_BUCKETS // 2
    max_exact = half_b // 2
    ret = np.where(rel > 0, half_b, 0)
    n = np.abs(rel)
    nf = np.maximum(n, 1).astype(np.float32)
    large = max_exact + (np.log(nf / np.float32(max_exact)) / np.float32(math.log(MAX_DISTANCE / max_exact))
                         * np.float32(half_b - max_exact)).astype(np.int32)
    large = np.minimum(large, half_b - 1)
    return (ret + np.where(n < max_exact, n, large)).astype(np.int32)


def _toeplitz_tiles(vec, tile, reach):
    heads = vec.shape[0]
    r = (vec.shape[1] - 1) // 2
    n_off = 2 * reach + 1
    span = 2 * tile - 1
    lo = -reach * tile - (tile - 1)
    hi = reach * tile + (tile - 1)
    padded = jnp.pad(vec, ((0, 0), (-lo - r, hi - r)), constant_values=MASKED)
    rows = [padded[:, o * tile:o * tile + span] for o in range(n_off)]
    u = jnp.stack(rows, axis=1)[:, :, ::-1]
    w = jnp.pad(u, ((0, 0), (0, 0), (0, 1)))
    flat = jnp.broadcast_to(w[:, :, None, :], (heads, n_off, tile, span + 1))
    flat = flat.reshape(heads, n_off, tile * (span + 1))[:, :, :tile * span]
    skew = flat.reshape(heads, n_off, tile, span)
    tiles = skew[:, :, :, tile - 1:]
    masked = jnp.full((tiles.shape[0], 1) + tiles.shape[2:], MASKED, tiles.dtype)
    return jnp.concatenate([tiles, masked], axis=1)


def _bias_tiles(rel_bias):
    hi = lax.Precision.HIGHEST
    d = np.arange(-A_REACH, A_REACH + 1)
    count = np.zeros(d.shape, np.float64)
    for w, dil in A_PATTERNS:
        count += (d % dil == 0) & (np.abs(d) <= w // 2)
    log_count = np.where(count > 0, np.log(np.maximum(count, 1.0)), 0.0).astype(np.float32)
    table = rel_bias.astype(jnp.float32)
    onehot_a = jnp.asarray(np.eye(NUM_BUCKETS, dtype=np.float32)[_t5_bucket_np(d)])
    vec_a = (jnp.dot(onehot_a, table[:, :A_HEADS], precision=hi).T + log_count[None]) * LOG2E
    vec_a = jnp.where(jnp.asarray(count > 0)[None], vec_a, MASKED)
    tiles_a = _toeplitz_tiles(vec_a, SUB, (-(-A_REACH // A_TILE) + 1) * (A_TILE // SUB) - 1)
    db = np.arange(-B_HALF_WINDOW, B_HALF_WINDOW + 1)
    onehot_b = jnp.asarray(np.eye(NUM_BUCKETS, dtype=np.float32)[_t5_bucket_np(db)])
    vec_b = jnp.dot(onehot_b, table[:, A_HEADS:], precision=hi).T * LOG2E
    tiles_b = _toeplitz_tiles(vec_b, SUB, (-(-B_HALF_WINDOW // B_TILE) + 1) * (B_TILE // SUB) - 1)
    top_a = jnp.max(jnp.abs(jnp.where(jnp.asarray(count > 0)[None], vec_a, 0.0)))
    return tiles_a, tiles_b, top_a, jnp.max(jnp.abs(vec_b))


def kernel(x, c, w_mod, b_mod, pre_norm, post_norm, w_in, w_out, q_norm, k_norm, sink, rel_bias):
    bsz, s, d = x.shape
    assert bsz == 1 and c.shape[0] == 1
    depth = w_in.shape[0]
    assert w_in.shape[2] == IN_WIDTH and w_out.shape[1] == A_W + B_W + C_W
    x2 = x[0]
    cos_t, sin_t = _rope_tables(s)
    tiles_a, tiles_b, top_bias_a, top_bias_b = _bias_tiles(rel_bias)
    mod = _modulation(c, w_mod, b_mod)[:, 0]
    shift, scale, gate = mod[:, :d], mod[:, d:2 * d], mod[:, 2 * d:]
    vec = lambda a, i: a[i][None, :]

    h = _prenorm(x2, vec(pre_norm, 0), vec(scale, 0), vec(shift, 0))
    for i in range(depth):
        proj = _in_projection(h, w_in, i)
        vta, vtb, vtc, kc, sq_norms = _prep(proj, cos_t, sin_t, k_norm[i][:, None])
        sq_norms = jnp.max(sq_norms, axis=(0, 2))
        qk_scale = NORM_SLACK * HEAD_DIM ** -0.5 * LOG2E
        bound_a = qk_scale * jnp.sqrt(sq_norms[0] * sq_norms[1]) + top_bias_a
        bound_b = jnp.maximum(qk_scale * jnp.sqrt(sq_norms[2] * sq_norms[3]) + top_bias_b,
                              jnp.max(jnp.abs(sink[i])) * LOG2E)
        band_a = functools.partial(_band_attention, proj, AK, vta, AQ, AG, A_HEADS, 1, A_TILE,
                                   -(-A_REACH // A_TILE), tiles_a, None, heads=A_HEADS_PER_STEP)
        band_b = functools.partial(_band_attention, proj, BK, vtb, BQ, BG, B_KV_HEADS, B_HEADS // B_KV_HEADS,
                                   B_TILE, -(-B_HALF_WINDOW // B_TILE), tiles_b, sink[i])
        oa = lax.cond(bound_a <= LOGIT_BOUND, lambda: band_a(True), lambda: band_a(False))
        ob = lax.cond(bound_b <= LOGIT_BOUND, lambda: band_b(True), lambda: band_b(False))
        dense = functools.partial(_dense_attention, proj, kc, vtc, CQ, CG, C_KV_HEADS,
                                  C_HEADS // C_KV_HEADS, C_TK, cos_t, sin_t, q_norm[i][:, None])
        logit_bound = (HEAD_DIM ** 0.5 * LOG2E) * jnp.max(jnp.abs(q_norm[i])) * jnp.max(jnp.abs(k_norm[i]))
        oc = lax.cond(logit_bound <= LOGIT_BOUND, lambda: dense(True), lambda: dense(False))
        y = _out_projection(oa, ob, oc, w_out, i)
        nxt = None if i == depth - 1 else (vec(pre_norm, i + 1), vec(scale, i + 1), vec(shift, i + 1))
        x2, h = _post(x2, y, vec(post_norm, i), vec(gate, i), nxt)
    return x2[None]
```

```python
import functools
import math

import numpy as np
import jax
import jax.numpy as jnp
from jax import lax
from jax.experimental import pallas as pl
from jax.experimental.pallas import tpu as pltpu

HEAD_DIM = 128
GRID_W = 64
A_HEADS = 12
A_PATTERNS = ((128, 1), (512, 4), (2048, 16))
B_HEADS = 8
B_KV_HEADS = 2
B_HALF_WINDOW = 128
C_HEADS = 12
C_KV_HEADS = 4
ROPE_THETA = 10000.0
NUM_BUCKETS = 32
MAX_DISTANCE = 1024
EPS = 1e-6
LOG2E = math.log2(math.e)
MASKED = -1e30
M_INIT = -1e29

A_W = A_HEADS * HEAD_DIM
B_W = B_HEADS * HEAD_DIM
C_W = C_HEADS * HEAD_DIM
AQ, AK, AV, AG = 0, 12, 24, 36
BQ, BK, BV, BG = 48, 56, 58, 60
CQ, CK, CV, CG = 68, 80, 84, 88
IN_WIDTH = 100 * HEAD_DIM

A_REACH = max(w // 2 for w, _ in A_PATTERNS)
A_TILE = 512
B_TILE = 256
C_TQ = {True: 1024, False: 256}
C_TK = 1024
BAND_HEADS_PER_STEP = 2
A_HEADS_PER_STEP = 4
BOUNDED_GROUP = 4
NORM_SLACK = 1.02
LOGIT_BOUND = 64.0
DENSE_UNROLL = 3
VT_ROWS = HEAD_DIM + 16
V7X_VMEM_LIMIT = 56 * 1024 * 1024
SUBLANES = 8
SUB = 128


def _cparams(n_axes):
    return pltpu.CompilerParams(dimension_semantics=("arbitrary",) * n_axes,
                                vmem_limit_bytes=V7X_VMEM_LIMIT)


def _tile(n, pref):
    t = min(n, pref)
    while n % t:
        t //= 2
    return t


def _mod_kernel(c_ref, w_ref, b_ref, o_ref):
    c = c_ref[...]
    cs = jnp.broadcast_to(c * jax.nn.sigmoid(c), (SUBLANES, c.shape[1]))
    r = jnp.dot(cs, w_ref[0], preferred_element_type=jnp.float32)
    o_ref[0] = r[0:1] + b_ref[0]


def _modulation(c, w_mod, b_mod):
    depth, d, n = w_mod.shape
    tn = _tile(n, 1024)
    out = pl.pallas_call(
        _mod_kernel,
        grid=(depth, n // tn),
        in_specs=[pl.BlockSpec((1, d), lambda i, j: (0, 0)),
                  pl.BlockSpec((1, d, tn), lambda i, j: (i, 0, j)),
                  pl.BlockSpec((1, 1, tn), lambda i, j: (i, 0, j))],
        out_specs=pl.BlockSpec((1, 1, tn), lambda i, j: (i, 0, j)),
        out_shape=jax.ShapeDtypeStruct((depth, 1, n), jnp.float32),
        compiler_params=_cparams(2),
        name="modulation",
    )(c, w_mod, b_mod.reshape(depth, 1, n))
    return out


def _rms(x):
    return x * lax.rsqrt(jnp.mean(x * x, axis=-1, keepdims=True) + EPS)


def _prenorm_kernel(x_ref, g_ref, sc_ref, sh_ref, h_ref):
    y = _rms(x_ref[...]) * g_ref[...]
    h_ref[...] = (y * (1.0 + sc_ref[...]) + sh_ref[...]).astype(h_ref.dtype)


def _prenorm(x, g, scale, shift):
    s, d = x.shape
    tm = _tile(s, 512)
    vec = pl.BlockSpec((1, d), lambda i: (0, 0))
    return pl.pallas_call(
        _prenorm_kernel,
        grid=(s // tm,),
        in_specs=[pl.BlockSpec((tm, d), lambda i: (i, 0)), vec, vec, vec],
        out_specs=pl.BlockSpec((tm, d), lambda i: (i, 0)),
        out_shape=jax.ShapeDtypeStruct((s, d), jnp.bfloat16),
        compiler_params=_cparams(1),
        name="prenorm",
    )(x, g, scale, shift)


def _post_kernel(x_ref, y_ref, pg_ref, gate_ref, xo_ref):
    yn = _rms(y_ref[...].astype(jnp.float32)) * pg_ref[...]
    xo_ref[...] = x_ref[...] + gate_ref[...] * yn


def _postpre_kernel(x_ref, y_ref, pg_ref, gate_ref, g_ref, sc_ref, sh_ref, xo_ref, h_ref):
    yn = _rms(y_ref[...].astype(jnp.float32)) * pg_ref[...]
    xn = x_ref[...] + gate_ref[...] * yn
    xo_ref[...] = xn
    h = _rms(xn) * g_ref[...]
    h_ref[...] = (h * (1.0 + sc_ref[...]) + sh_ref[...]).astype(h_ref.dtype)


def _post(x, y, post_g, gate, nxt=None):
    s, d = x.shape
    tm = _tile(s, 256)
    row = pl.BlockSpec((tm, d), lambda i: (i, 0))
    vec = pl.BlockSpec((1, d), lambda i: (0, 0))
    if nxt is None:
        return pl.pallas_call(
            _post_kernel, grid=(s // tm,),
            in_specs=[row, row, vec, vec], out_specs=row,
            out_shape=jax.ShapeDtypeStruct((s, d), jnp.float32),
            compiler_params=_cparams(1), name="postnorm",
        )(x, y, post_g, gate), None
    g, scale, shift = nxt
    return pl.pallas_call(
        _postpre_kernel, grid=(s // tm,),
        in_specs=[row, row, vec, vec, vec, vec, vec], out_specs=[row, row],
        out_shape=[jax.ShapeDtypeStruct((s, d), jnp.float32),
                   jax.ShapeDtypeStruct((s, d), jnp.bfloat16)],
        compiler_params=_cparams(1), name="postnorm_prenorm",
    )(x, y, post_g, gate, g, scale, shift)


def _mm_kernel(a_ref, b_ref, o_ref):
    o_ref[...] = jnp.dot(a_ref[...], b_ref[0].astype(a_ref.dtype),
                         preferred_element_type=jnp.float32).astype(o_ref.dtype)


def _in_projection(h, w, layer):
    m, k = h.shape
    n = w.shape[2]
    tm, tn = _tile(m, 1024), _tile(n, 512)
    return pl.pallas_call(
        _mm_kernel,
        grid=(m // tm, n // tn),
        in_specs=[pl.BlockSpec((tm, k), lambda i, j: (i, 0)),
                  pl.BlockSpec((1, k, tn), lambda i, j: (layer, 0, j))],
        out_specs=pl.BlockSpec((tm, tn), lambda i, j: (i, j)),
        out_shape=jax.ShapeDtypeStruct((m, n), jnp.bfloat16),
        compiler_params=_cparams(2),
        name="in_projection",
    )(h, w)


def _mm3_kernel(a1, a2, a3, w_ref, o_ref, wb_ref):
    @pl.when(pl.program_id(1) == 0)
    def _():
        wb_ref[...] = w_ref[0].astype(wb_ref.dtype)

    acc, row = None, 0
    for a in (a1, a2, a3):
        part = jnp.dot(a[...], wb_ref[row:row + a.shape[1]], preferred_element_type=jnp.float32)
        acc = part if acc is None else acc + part
        row += a.shape[1]
    o_ref[...] = acc.astype(o_ref.dtype)


def _out_projection(oa, ob, oc, w, layer):
    m = oa.shape[0]
    k, n = w.shape[1:]
    tm, tn = _tile(m, 1024), _tile(n, 512)
    a_spec = lambda a: pl.BlockSpec((tm, a.shape[1]), lambda j, i: (i, 0))
    return pl.pallas_call(
        _mm3_kernel,
        grid=(n // tn, m // tm),
        in_specs=[a_spec(oa), a_spec(ob), a_spec(oc),
                  pl.BlockSpec((1, k, tn), lambda j, i: (layer, 0, j))],
        out_specs=pl.BlockSpec((tm, tn), lambda j, i: (i, j)),
        out_shape=jax.ShapeDtypeStruct((m, n), jnp.bfloat16),
        scratch_shapes=[pltpu.VMEM((k, tn), oa.dtype)],
        compiler_params=_cparams(2),
        name="out_projection",
    )(oa, ob, oc, w)


def _rope_tables(s):
    rows = s // GRID_W
    row = jnp.repeat(jnp.arange(rows), GRID_W)
    col = jnp.tile(jnp.arange(GRID_W), rows)
    quarter = HEAD_DIM // 4
    freqs = ROPE_THETA ** (-jnp.arange(0, 2 * quarter, 2, dtype=jnp.float32) / (2 * quarter))
    ar = row.astype(jnp.float32)[:, None] * freqs[None, :]
    ac = col.astype(jnp.float32)[:, None] * freqs[None, :]
    cr, sr, cc, sc = jnp.cos(ar), jnp.sin(ar), jnp.cos(ac), jnp.sin(ac)
    cos = jnp.concatenate([cr, cr, cc, cc], axis=-1)
    sin = jnp.concatenate([-sr, sr, -sc, sc], axis=-1)
    return cos.T, sin.T


def _norm_rope_t(xt, gain_col, cos_t, sin_t):
    quarter = HEAD_DIM // 4
    y = xt * lax.rsqrt(jnp.mean(xt * xt, axis=0, keepdims=True) + EPS) * gain_col
    partner = jnp.concatenate([y[quarter:2 * quarter], y[:quarter], y[3 * quarter:], y[2 * quarter:3 * quarter]],
                              axis=0)
    return y * cos_t + partner * sin_t


def _prep_kernel(av_ref, bv_ref, cv_ref, ck_ref, aq_ref, ak_ref, bq_ref, bk_ref, cos_ref, sin_ref, kg_ref,
                 vta_ref, vtb_ref, vtc_ref, kc_ref, nrm_ref, *, ts):
    def put_t(src_ref, n_heads, dst_ref, chunk):
        for hh in range(n_heads):
            vt = src_ref[:, hh * HEAD_DIM:(hh + 1) * HEAD_DIM].T
            for cidx in range(ts // chunk):
                dst_ref[hh, cidx, :HEAD_DIM] = vt[:, cidx * chunk:(cidx + 1) * chunk]
                dst_ref[hh, cidx, HEAD_DIM:] = jnp.ones((VT_ROWS - HEAD_DIM, chunk), dst_ref.dtype)

    put_t(av_ref, A_HEADS, vta_ref, A_TILE)
    put_t(bv_ref, B_KV_HEADS, vtb_ref, B_TILE)
    put_t(cv_ref, C_KV_HEADS, vtc_ref, C_TK)
    cos_t, sin_t, kg = cos_ref[...], sin_ref[...], kg_ref[...]
    for hh in range(C_KV_HEADS):
        kt = ck_ref[:, hh * HEAD_DIM:(hh + 1) * HEAD_DIM].astype(jnp.float32).T
        k = _norm_rope_t(kt, kg, cos_t, sin_t).T
        kc_ref[:, hh * HEAD_DIM:(hh + 1) * HEAD_DIM] = k.astype(kc_ref.dtype)

    def max_sq_norm(src_ref, n_heads):
        ones = jnp.ones((HEAD_DIM, HEAD_DIM), src_ref.dtype)
        best = None
        for hh in range(n_heads):
            x = src_ref[:, hh * HEAD_DIM:(hh + 1) * HEAD_DIM]
            row_sums = jnp.dot(x * x, ones, preferred_element_type=jnp.float32)
            top = jnp.max(row_sums, axis=0, keepdims=True)
            best = top if best is None else jnp.maximum(best, top)
        return best

    nrm_ref[0] = jnp.concatenate(
        [max_sq_norm(aq_ref, A_HEADS), max_sq_norm(ak_ref, A_HEADS), max_sq_norm(bq_ref, B_HEADS),
         max_sq_norm(bk_ref, B_KV_HEADS), jnp.zeros((SUBLANES - 4, HEAD_DIM), jnp.float32)], axis=0)


def _prep(proj, cos_t, sin_t, k_gain_col):
    s = proj.shape[0]
    ts = max(A_TILE, B_TILE, C_TK)
    assert s % ts == 0
    bf = jnp.bfloat16
    n = s // ts
    vt_spec = lambda heads, t: pl.BlockSpec((heads, ts // t, VT_ROWS, t), lambda i: (0, i, 0, 0))
    vt_shape = lambda heads, t: jax.ShapeDtypeStruct((heads, s // t, VT_ROWS, t), bf)
    return pl.pallas_call(
        functools.partial(_prep_kernel, ts=ts),
        grid=(n,),
        in_specs=[pl.BlockSpec((ts, A_W), lambda i: (i, AV * HEAD_DIM // A_W)),
                  pl.BlockSpec((ts, B_KV_HEADS * HEAD_DIM), lambda i: (i, BV // B_KV_HEADS)),
                  pl.BlockSpec((ts, C_KV_HEADS * HEAD_DIM), lambda i: (i, CV // C_KV_HEADS)),
                  pl.BlockSpec((ts, C_KV_HEADS * HEAD_DIM), lambda i: (i, CK // C_KV_HEADS)),
                  pl.BlockSpec((ts, A_W), lambda i: (i, AQ * HEAD_DIM // A_W)),
                  pl.BlockSpec((ts, A_W), lambda i: (i, AK * HEAD_DIM // A_W)),
                  pl.BlockSpec((ts, B_W), lambda i: (i, BQ * HEAD_DIM // B_W)),
                  pl.BlockSpec((ts, B_KV_HEADS * HEAD_DIM), lambda i: (i, BK // B_KV_HEADS)),
                  pl.BlockSpec((HEAD_DIM, ts), lambda i: (0, i)),
                  pl.BlockSpec((HEAD_DIM, ts), lambda i: (0, i)),
                  pl.BlockSpec((HEAD_DIM, 1), lambda i: (0, 0))],
        out_specs=[vt_spec(A_HEADS, A_TILE), vt_spec(B_KV_HEADS, B_TILE), vt_spec(C_KV_HEADS, C_TK),
                   pl.BlockSpec((ts, C_KV_HEADS * HEAD_DIM), lambda i: (i, 0)),
                   pl.BlockSpec((1, SUBLANES, HEAD_DIM), lambda i: (i, 0, 0))],
        out_shape=[vt_shape(A_HEADS, A_TILE), vt_shape(B_KV_HEADS, B_TILE), vt_shape(C_KV_HEADS, C_TK),
                   jax.ShapeDtypeStruct((s, C_KV_HEADS * HEAD_DIM), bf),
                   jax.ShapeDtypeStruct((n, SUBLANES, HEAD_DIM), jnp.float32)],
        compiler_params=_cparams(1),
        name="vt_krope_prep",
    )(proj, proj, proj, proj, proj, proj, proj, proj, cos_t, sin_t, k_gain_col)


def _scores(k_ref, head, row0, tk, qt_ref, s_ref, mx_ref, slot, bias=None):
    kt = k_ref[pl.ds(pl.multiple_of(row0, tk), tk), head * HEAD_DIM:(head + 1) * HEAD_DIM]
    s = jnp.dot(kt, qt_ref[...], preferred_element_type=jnp.float32)
    if bias is not None:
        s = s + bias
    s_ref[slot] = s
    mx_ref[slot] = jnp.max(s, axis=0, keepdims=True)


def _accumulate(vt, s_ref, mx_ref, m_ref, acc_ref, slot):
    m_old = m_ref[...]
    m_new = jnp.maximum(m_old, mx_ref[slot])
    alpha = jnp.exp2(m_old - m_new)
    p = jnp.exp2(s_ref[slot] - m_new).astype(jnp.bfloat16)
    acc_ref[...] = alpha * acc_ref[...] + jnp.dot(vt, p, preferred_element_type=jnp.float32)
    m_ref[...] = m_new


def _load_q(q_refs, qt_ref, tq, prep_t=lambda qt: qt):
    scale2 = HEAD_DIM ** -0.5 * LOG2E
    for g, q_ref in enumerate(q_refs):
        qt = prep_t(q_ref[...].astype(jnp.float32).T)
        qt_ref[:, g * tq:(g + 1) * tq] = (qt * scale2).astype(qt_ref.dtype)


def _store_gated(acc, g_refs, o_ref, tq, first=0, extra_denominator=None):
    for g, g_ref in enumerate(g_refs):
        sl = slice(g * tq, (g + 1) * tq)
        den = acc[HEAD_DIM:HEAD_DIM + 1, sl]
        if extra_denominator is not None:
            den = den + extra_denominator[g]
        o = (acc[:HEAD_DIM, sl] / den).T
        gate = g_ref[...].astype(jnp.float32)
        cols = slice((first + g) * HEAD_DIM, (first + g + 1) * HEAD_DIM)
        o_ref[:, cols] = (o * (gate * jax.nn.sigmoid(gate))).astype(o_ref.dtype)


def _band_kernel(*refs, heads, group, t, nkv, reach, has_sink, bounded):
    it = iter(refs)
    q_refs = [[next(it) for _ in range(group)] for _ in range(heads)]
    g_refs = [[next(it) for _ in range(group)] for _ in range(heads)]
    k_ref, vt_ref, bias_ref = next(it), next(it), next(it)
    sink_ref = next(it) if has_sink else None
    o_ref = next(it)
    qt_ref, s_ref, mx_ref, m_ref, acc_ref = (next(it) for _ in range(5))
    h = pl.program_id(0)
    qb = pl.program_id(1)
    n_off = 2 * reach + 1

    nsb = t // SUB
    n_sub = bias_ref.shape[1] - 1

    def block(o):
        kb = qb + (o - reach)
        return jnp.clip(kb, 0, nkv - 1), jnp.logical_and(kb >= 0, kb < nkv)

    def bias_tile(hh, o, valid):
        rows = []
        for a in range(nsb):
            cols = []
            for g in range(group):
                for b in range(nsb):
                    sub = (o - reach) * nsb + a - b + (n_sub - 1) // 2
                    cols.append(bias_ref[hh * group + g, jnp.where(valid, sub, n_sub)])
            rows.append(jnp.concatenate(cols, axis=1))
        return jnp.concatenate(rows, axis=0)

    if bounded:
        for hh in range(heads):
            _load_q(q_refs[hh], qt_ref.at[hh], t)
        width = min(n_off, nkv)
        start = jnp.clip(qb - reach, 0, nkv - width)

        def bias_window(hh):
            rows = []
            for a in range(width * nsb):
                cols = []
                for g in range(group):
                    for b in range(nsb):
                        sub = (start - qb) * nsb + a - b + (n_sub - 1) // 2
                        inside = jnp.logical_and(sub >= 0, sub < n_sub)
                        cols.append(bias_ref[hh * group + g, jnp.where(inside, sub, n_sub)])
                rows.append(jnp.concatenate(cols, axis=1))
            return jnp.concatenate(rows, axis=0)

        for hh in range(heads):
            keys = k_ref[pl.ds(pl.multiple_of(start * t, t), width * t), hh * HEAD_DIM:(hh + 1) * HEAD_DIM]
            s = jnp.dot(keys, qt_ref[hh], preferred_element_type=jnp.float32) + bias_window(hh)
            p = jnp.exp2(s).astype(jnp.bfloat16)
            total = None
            for o in range(width):
                part = jnp.dot(vt_ref[hh, start + o], p[o * t:(o + 1) * t],
                               preferred_element_type=jnp.float32)
                total = part if total is None else total + part
            sinks = None
            if has_sink:
                sinks = [jnp.exp2(jnp.full((1, t), sink_ref[(h * heads + hh) * group + g] * LOG2E, jnp.float32))
                         for g in range(group)]
            _store_gated(total, g_refs[hh], o_ref, t, hh * group, sinks)
        return

    for hh in range(heads):
        _load_q(q_refs[hh], qt_ref.at[hh], t)
        acc_ref[hh, :HEAD_DIM] = jnp.zeros((HEAD_DIM, group * t), jnp.float32)
        if has_sink:
            for g in range(group):
                sink = sink_ref[(h * heads + hh) * group + g] * LOG2E
                m_ref[hh, :, g * t:(g + 1) * t] = jnp.full((1, t), sink, jnp.float32)
            acc_ref[hh, HEAD_DIM:] = jnp.ones((VT_ROWS - HEAD_DIM, group * t), jnp.float32)
        else:
            m_ref[hh] = jnp.full((1, group * t), M_INIT, jnp.float32)
            acc_ref[hh, HEAD_DIM:] = jnp.zeros((VT_ROWS - HEAD_DIM, group * t), jnp.float32)

    units = [(hh, o) for hh in range(heads) for o in range(n_off)]

    def scores(u):
        hh, o = units[u]
        kb, valid = block(o)
        _scores(k_ref, hh, kb * t, t, qt_ref.at[hh], s_ref, mx_ref, u % 2, bias_tile(hh, o, valid))

    scores(0)
    for u, (hh, o) in enumerate(units):
        if u + 1 < len(units):
            scores(u + 1)
        _accumulate(vt_ref[hh, block(o)[0]], s_ref, mx_ref, m_ref.at[hh], acc_ref.at[hh], u % 2)
        if o == n_off - 1:
            _store_gated(acc_ref.at[hh], g_refs[hh], o_ref, t, hh * group)


def _dense_kernel(*refs, group, tq, tk, nkv, bounded):
    it = iter(refs)
    q_refs = [next(it) for _ in range(group)]
    g_refs = [next(it) for _ in range(group)]
    k_ref, vt_ref, cos_ref, sin_ref, qg_ref, o_ref = (next(it) for _ in range(6))
    qt_ref, s_ref, mx_ref, m_ref, acc_ref = (next(it) for _ in range(5))
    qt_ref, m_ref, acc_ref = qt_ref.at[0], m_ref.at[0], acc_ref.at[0]

    _load_q(q_refs, qt_ref, tq, lambda xt: _norm_rope_t(xt, qg_ref[...], cos_ref[...], sin_ref[...]))
    acc_ref[...] = jnp.zeros(acc_ref.shape, jnp.float32)

    if bounded:
        def block(jblk):
            kt = k_ref[pl.ds(pl.multiple_of(jblk * tk, tk), tk), :]
            s = jnp.dot(kt, qt_ref[...], preferred_element_type=jnp.float32)
            return jnp.dot(vt_ref[0, jblk], jnp.exp2(s).astype(jnp.bfloat16),
                           preferred_element_type=jnp.float32)

        per_iter = math.gcd(nkv, BOUNDED_GROUP)

        def blocks(it, carry):
            part = block(it * per_iter)
            for u in range(1, per_iter):
                part = part + block(it * per_iter + u)
            acc_ref[...] += part
            return carry

        lax.fori_loop(0, nkv // per_iter, blocks, 0)
        _store_gated(acc_ref, g_refs, o_ref, tq)
        return

    m_ref[...] = jnp.full(m_ref.shape, M_INIT, jnp.float32)

    def scores(jblk, slot):
        _scores(k_ref, 0, jblk * tk, tk, qt_ref, s_ref, mx_ref, slot)

    def accumulate(jblk, slot):
        _accumulate(vt_ref[0, jblk], s_ref, mx_ref, m_ref, acc_ref, slot)

    scores(0, 0)

    def pair(j0):
        scores(j0 + 1, 1)
        accumulate(j0, 0)
        scores(j0 + 2, 0)
        accumulate(j0 + 1, 1)

    def pairs(it, carry):
        for u in range(DENSE_UNROLL):
            pair(2 * (it * DENSE_UNROLL + u))
        return carry

    n_pairs = nkv // 2 - 1
    lax.fori_loop(0, n_pairs // DENSE_UNROLL, pairs, 0)
    for jj in range(n_pairs - n_pairs % DENSE_UNROLL, n_pairs):
        pair(2 * jj)
    scores(nkv - 1, 1)
    accumulate(nkv - 2, 0)
    accumulate(nkv - 1, 1)
    _store_gated(acc_ref, g_refs, o_ref, tq)


def _resident(heads):
    return dict(pipeline_mode=pl.Buffered(1)) if heads > 2 else {}


def _attention(kern, proj, k_src, k_col, vt, q_col, g_col, n_kv_heads, group, tq, tk, extra_specs,
               extra_args, name, heads=1, bounded=False):
    s = proj.shape[0]
    parked = SUBLANES if bounded else tk
    assert s % tq == 0 and s % tk == 0 and n_kv_heads % heads == 0 and k_col % heads == 0
    in_specs, args = [], []
    for col in (q_col, g_col):
        for hh in range(heads):
            for g in range(group):
                in_specs.append(pl.BlockSpec(
                    (tq, HEAD_DIM), lambda h, qb, col=col, hh=hh, g=g: (qb, col + (h * heads + hh) * group + g)))
                args.append(proj)
    in_specs += [pl.BlockSpec((s, heads * HEAD_DIM), lambda h, qb: (0, k_col // heads + h),
                              **_resident(heads)),
                 pl.BlockSpec((heads, s // tk, VT_ROWS, tk), lambda h, qb: (h, 0, 0, 0),
                              **_resident(heads))] + extra_specs
    args += [k_src, vt] + extra_args
    n = group * tq
    return pl.pallas_call(
        kern,
        grid=(n_kv_heads // heads, s // tq),
        in_specs=in_specs,
        out_specs=pl.BlockSpec((tq, heads * group * HEAD_DIM), lambda h, qb: (qb, h)),
        out_shape=jax.ShapeDtypeStruct((s, n_kv_heads * group * HEAD_DIM), jnp.bfloat16),
        scratch_shapes=[pltpu.VMEM((heads, HEAD_DIM, n), jnp.bfloat16),
                        pltpu.VMEM((2, parked, n), jnp.float32),
                        pltpu.VMEM((2, 1, n), jnp.float32),
                        pltpu.VMEM((heads, 1, n), jnp.float32),
                        pltpu.VMEM((heads, VT_ROWS, n), jnp.float32)],
        compiler_params=_cparams(2),
        name=name,
    )(*args)


def _band_attention(proj, k_col, vt, q_col, g_col, n_kv_heads, group, t, reach, bias, sink, bounded,
                    heads=BAND_HEADS_PER_STEP):
    nkv = proj.shape[0] // t
    extra_specs = [pl.BlockSpec((heads * group,) + bias.shape[1:], lambda h, qb: (h, 0, 0, 0),
                                **_resident(heads))]
    extra_args = [bias]
    if sink is not None:
        extra_specs.append(pl.BlockSpec(memory_space=pltpu.SMEM))
        extra_args.append(sink)
    kern = functools.partial(_band_kernel, heads=heads, group=group, t=t, nkv=nkv, reach=reach,
                             has_sink=sink is not None, bounded=bounded)
    name = "band_attention_g%d%s" % (group, "_bounded" if bounded else "")
    return _attention(kern, proj, proj, k_col, vt, q_col, g_col, n_kv_heads, group, t, t,
                      extra_specs, extra_args, name, heads, bounded)


def _dense_attention(proj, kc, vt, q_col, g_col, n_kv_heads, group, tk, cos_t, sin_t, q_gain_col, bounded):
    nkv = proj.shape[0] // tk
    tq = C_TQ[bounded]
    assert nkv % 2 == 0
    extra_specs = [pl.BlockSpec((HEAD_DIM, tq), lambda h, qb: (0, qb)),
                   pl.BlockSpec((HEAD_DIM, tq), lambda h, qb: (0, qb)),
                   pl.BlockSpec((HEAD_DIM, 1), lambda h, qb: (0, 0))]
    kern = functools.partial(_dense_kernel, group=group, tq=tq, tk=tk, nkv=nkv, bounded=bounded)
    name = "dense_attention_g%d%s" % (group, "_bounded" if bounded else "")
    return _attention(kern, proj, kc, 0, vt, q_col, g_col, n_kv_heads, group, tq, tk,
                      extra_specs, [cos_t, sin_t, q_gain_col], name, 1, bounded)


def _t5_bucket_np(rel):
    half_b = NUM_BUCKETS // 2
    max_exact = half_b // 2
    ret = np.where(rel > 0, half_b, 0)
    n = np.abs(rel)
    nf = np.maximum(n, 1).astype(np.float32)
    large = max_exact + (np.log(nf / np.float32(max_exact)) / np.float32(math.log(MAX_DISTANCE / max_exact))
                         * np.float32(half_b - max_exact)).astype(np.int32)
    large = np.minimum(large, half_b - 1)
    return (ret + np.where(n < max_exact, n, large)).astype(np.int32)


def _toeplitz_tiles(vec, tile, reach):
    heads = vec.shape[0]
    r = (vec.shape[1] - 1) // 2
    n_off = 2 * reach + 1
    span = 2 * tile - 1
    lo = -reach * tile - (tile - 1)
    hi = reach * tile + (tile - 1)
    padded = jnp.pad(vec, ((0, 0), (-lo - r, hi - r)), constant_values=MASKED)
    rows = [padded[:, o * tile:o * tile + span] for o in range(n_off)]
    u = jnp.stack(rows, axis=1)[:, :, ::-1]
    w = jnp.pad(u, ((0, 0), (0, 0), (0, 1)))
    flat = jnp.broadcast_to(w[:, :, None, :], (heads, n_off, tile, span + 1))
    flat = flat.reshape(heads, n_off, tile * (span + 1))[:, :, :tile * span]
    skew = flat.reshape(heads, n_off, tile, span)
    tiles = skew[:, :, :, tile - 1:]
    masked = jnp.full((tiles.shape[0], 1) + tiles.shape[2:], MASKED, tiles.dtype)
    return jnp.concatenate([tiles, masked], axis=1)


def _bias_tiles(rel_bias):
    hi = lax.Precision.HIGHEST
    d = np.arange(-A_REACH, A_REACH + 1)
    count = np.zeros(d.shape, np.float64)
    for w, dil in A_PATTERNS:
        count += (d % dil == 0) & (np.abs(d) <= w // 2)
    log_count = np.where(count > 0, np.log(np.maximum(count, 1.0)), 0.0).astype(np.float32)
    table = rel_bias.astype(jnp.float32)
    onehot_a = jnp.asarray(np.eye(NUM_BUCKETS, dtype=np.float32)[_t5_bucket_np(d)])
    vec_a = (jnp.dot(onehot_a, table[:, :A_HEADS], precision=hi).T + log_count[None]) * LOG2E
    vec_a = jnp.where(jnp.asarray(count > 0)[None], vec_a, MASKED)
    tiles_a = _toeplitz_tiles(vec_a, SUB, (-(-A_REACH // A_TILE) + 1) * (A_TILE // SUB) - 1)
    db = np.arange(-B_HALF_WINDOW, B_HALF_WINDOW + 1)
    onehot_b = jnp.asarray(np.eye(NUM_BUCKETS, dtype=np.float32)[_t5_bucket_np(db)])
    vec_b = jnp.dot(onehot_b, table[:, A_HEADS:], precision=hi).T * LOG2E
    tiles_b = _toeplitz_tiles(vec_b, SUB, (-(-B_HALF_WINDOW // B_TILE) + 1) * (B_TILE // SUB) - 1)
    top_a = jnp.max(jnp.abs(jnp.where(jnp.asarray(count > 0)[None], vec_a, 0.0)))
    return tiles_a, tiles_b, top_a, jnp.max(jnp.abs(vec_b))


def kernel(x, c, w_mod, b_mod, pre_norm, post_norm, w_in, w_out, q_norm, k_norm, sink, rel_bias):
    bsz, s, d = x.shape
    assert bsz == 1 and c.shape[0] == 1
    depth = w_in.shape[0]
    assert w_in.shape[2] == IN_WIDTH and w_out.shape[1] == A_W + B_W + C_W
    x2 = x[0]
    cos_t, sin_t = _rope_tables(s)
    tiles_a, tiles_b, top_bias_a, top_bias_b = _bias_tiles(rel_bias)
    mod = _modulation(c, w_mod, b_mod)[:, 0]
    shift, scale, gate = mod[:, :d], mod[:, d:2 * d], mod[:, 2 * d:]
    vec = lambda a, i: a[i][None, :]

    h = _prenorm(x2, vec(pre_norm, 0), vec(scale, 0), vec(shift, 0))
    for i in range(depth):
        proj = _in_projection(h, w_in, i)
        vta, vtb, vtc, kc, sq_norms = _prep(proj, cos_t, sin_t, k_norm[i][:, None])
        sq_norms = jnp.max(sq_norms, axis=(0, 2))
        qk_scale = NORM_SLACK * HEAD_DIM ** -0.5 * LOG2E
        bound_a = qk_scale * jnp.sqrt(sq_norms[0] * sq_norms[1]) + top_bias_a
        bound_b = jnp.maximum(qk_scale * jnp.sqrt(sq_norms[2] * sq_norms[3]) + top_bias_b,
                              jnp.max(jnp.abs(sink[i])) * LOG2E)
        band_a = functools.partial(_band_attention, proj, AK, vta, AQ, AG, A_HEADS, 1, A_TILE,
                                   -(-A_REACH // A_TILE), tiles_a, None, heads=A_HEADS_PER_STEP)
        band_b = functools.partial(_band_attention, proj, BK, vtb, BQ, BG, B_KV_HEADS, B_HEADS // B_KV_HEADS,
                                   B_TILE, -(-B_HALF_WINDOW // B_TILE), tiles_b, sink[i])
        oa = lax.cond(bound_a <= LOGIT_BOUND, lambda: band_a(True), lambda: band_a(False))
        ob = lax.cond(bound_b <= LOGIT_BOUND, lambda: band_b(True), lambda: band_b(False))
        dense = functools.partial(_dense_attention, proj, kc, vtc, CQ, CG, C_KV_HEADS,
                                  C_HEADS // C_KV_HEADS, C_TK, cos_t, sin_t, q_norm[i][:, None])
        logit_bound = (HEAD_DIM ** 0.5 * LOG2E) * jnp.max(jnp.abs(q_norm[i])) * jnp.max(jnp.abs(k_norm[i]))
        oc = lax.cond(logit_bound <= LOGIT_BOUND, lambda: dense(True), lambda: dense(False))
        y = _out_projection(oa, ob, oc, w_out, i)
        nxt = None if i == depth - 1 else (vec(pre_norm, i + 1), vec(scale, i + 1), vec(shift, i + 1))
        x2, h = _post(x2, y, vec(post_norm, i), vec(gate, i), nxt)
    return x2[None]
```
